```python
import math
import jax
import jax.numpy as jnp
from jax import lax
import numpy as np

D_MODEL = 1024
BATCH = 4
SEQ = 8192
DEPTH = 1
DEC_BATCH = 32
DEC_SEQ = 1
PAST_LEN = 16384
PAGE_SIZE = 128

NSA_HEADS = 8
NSA_KV_HEADS = 2
NSA_GROUP = NSA_HEADS // NSA_KV_HEADS
HEAD_DIM = 64
ROPE_DIM = HEAD_DIM // 4
ROPE_THETA = 500000.0
CMP_BLOCK = 32
CMP_STRIDE = 16
SEL_BLOCK = 64
SEL_TOPN = 16
WINDOW = 512
Q_BLOCK = 128
FORCE_BONUS = 1000.0
N_NSA_ROWS = 4
DN_HEADS = 4
DN_HEAD_DIM = 128
DN_WIDTH = DN_HEADS * DN_HEAD_DIM
DN_CONV = 4
DN_CHUNK = 64
N_GROUPS = 4
EXPERTS_PER_GROUP = 8
N_EXPERTS = N_GROUPS * EXPERTS_PER_GROUP
TOP_K = 2
D_EXPERT = 512
MOE_BLOCK = 128
EPS = 1e-6

NSA_Q_COLS = NSA_HEADS * HEAD_DIM
NSA_KV_COLS = NSA_KV_HEADS * HEAD_DIM
SPLIT_SIZES = (NSA_Q_COLS,) + (NSA_KV_COLS,) * 6 + (NSA_HEADS * 3, 3 * DN_WIDTH, DN_HEADS, DN_HEADS, DN_WIDTH, D_MODEL, D_MODEL)
IN_COLS = sum(SPLIT_SIZES)

kernel_name = 'hybrid_nsa_gdn_hmoe_step'


def rmsnorm(x, g):
    xf = x.astype(jnp.float32)
    xf = xf * lax.rsqrt(jnp.mean(xf * xf, axis=-1, keepdims=True) + EPS)
    return xf.astype(x.dtype) * g


def l2norm(x):
    xf = x.astype(jnp.float32)
    return xf * lax.rsqrt(jnp.sum(xf * xf, axis=-1, keepdims=True) + EPS)


def rope_partial(x, pos):
    half = ROPE_DIM // 2
    inv_freq = ROPE_THETA ** (-jnp.arange(half, dtype=jnp.float32) / half)
    ang = pos.astype(jnp.float32)[:, None] * inv_freq[None, :]
    cos = jnp.cos(ang)[:, None, :].astype(x.dtype)
    sin = jnp.sin(ang)[:, None, :].astype(x.dtype)
    x1, x2, rest = x[..., :half], x[..., half:ROPE_DIM], x[..., ROPE_DIM:]
    return jnp.concatenate([x1 * cos - x2 * sin, x2 * cos + x1 * sin, rest], axis=-1)


def masked_softmax(s, mask):
    s = jnp.where(mask, s.astype(jnp.float32), jnp.finfo(jnp.float32).min)
    return jnp.where(mask, jax.nn.softmax(s, axis=-1), 0.0)


def split_projection(h, w_in):
    offs = np.cumsum(SPLIT_SIZES)[:-1].tolist()
    return jnp.split(h @ w_in, offs, axis=-1)


def nsa_rows(q_raw, kc_raw, vc_raw, ks_raw, vs_raw, kw_raw, vw_raw, gate_raw, pos, q_norm, k_norm):
    B, T, _ = q_raw.shape
    kv_shape = (B, T, NSA_KV_HEADS, HEAD_DIM)
    q = rope_partial(rmsnorm(q_raw.reshape(B, T, NSA_HEADS, HEAD_DIM), q_norm), pos) * (HEAD_DIM ** -0.5)
    q = q.reshape(B, T, NSA_KV_HEADS, NSA_GROUP, HEAD_DIM).transpose(0, 2, 3, 1, 4)
    gates = jax.nn.sigmoid(gate_raw).reshape(B, T, NSA_KV_HEADS, NSA_GROUP, 3).transpose(0, 2, 3, 1, 4)
    ks = rope_partial(rmsnorm(ks_raw.reshape(kv_shape), k_norm[0]), pos)
    kw = rope_partial(rmsnorm(kw_raw.reshape(kv_shape), k_norm[1]), pos)
    rows = jnp.stack([kc_raw.reshape(kv_shape), vc_raw.reshape(kv_shape), ks, vs_raw.reshape(kv_shape)], axis=2)
    win = jnp.stack([kw, vw_raw.reshape(kv_shape)], axis=2)
    return q, gates, rows, win


def compress_blocks(k_raw, v_raw, cmp_pos, w_cmp, k_gain):
    L = k_raw.shape[1]
    n_cmp = (L - CMP_BLOCK) // CMP_STRIDE + 1
    starts = jnp.arange(n_cmp) * CMP_STRIDE
    idx = starts[:, None] + jnp.arange(CMP_BLOCK)[None, :]
    kb = k_raw[:, idx] + cmp_pos[0][None, None, :, None, :]
    vb = v_raw[:, idx] + cmp_pos[1][None, None, :, None, :]
    kc = jnp.einsum('bnlgd,lde->bnge', kb, w_cmp[0])
    vc = jnp.einsum('bnlgd,lde->bnge', vb, w_cmp[1])
    pos_c = starts + CMP_BLOCK - 1
    kc = rope_partial(rmsnorm(kc, k_gain), pos_c)
    return kc.transpose(0, 2, 1, 3), vc.transpose(0, 2, 1, 3), pos_c


def to_sel_blocks(rows):
    B, L, G, dk = rows.shape
    n_sel = -(-L // SEL_BLOCK)
    rows = jnp.pad(rows, ((0, 0), (0, n_sel * SEL_BLOCK - L), (0, 0), (0, 0)))
    return rows.reshape(B, n_sel, SEL_BLOCK, G, dk).transpose(0, 3, 1, 2, 4)


def cmp_sel_overlap(n_cmp, n_sel):
    cs = jnp.arange(n_cmp) * CMP_STRIDE
    ss = jnp.arange(n_sel) * SEL_BLOCK
    ov = (cs[:, None] <= ss[None, :] + SEL_BLOCK - 1) & (cs[:, None] + CMP_BLOCK - 1 >= ss[None, :])
    return ov.astype(jnp.float32)


def nsa_core(q, pos_q, gates, kc, vc, pos_c, ks_blk, vs_blk, kw, vw, pos_w, ovl):
    B, G, R, Tq, dk = q.shape
    p_c = masked_softmax(jnp.einsum('bgrqd,bgnd->bgrqn', q, kc), pos_c[None, :] <= pos_q[:, None])
    o_c = jnp.einsum('bgrqn,bgnd->bgrqd', p_c.astype(vc.dtype), vc)
    n_sel = ks_blk.shape[2]
    imp = jnp.einsum('bgrqn,ns->bgqs', p_c, ovl)
    blk = jnp.arange(n_sel)[None, :]
    cur = (pos_q // SEL_BLOCK)[:, None]
    readable = blk * SEL_BLOCK <= pos_q[:, None]
    forced = (blk == 0) | (blk == cur) | (blk == cur - 1)
    score = jnp.where(readable, imp + FORCE_BONUS * forced.astype(jnp.float32), -jnp.inf)
    top_s, top_i = lax.top_k(score, min(SEL_TOPN, n_sel))
    b_ix = jnp.arange(B)[:, None, None, None]
    g_ix = jnp.arange(G)[None, :, None, None]
    k_sel = ks_blk[b_ix, g_ix, top_i].reshape(B, G, Tq, -1, dk)
    v_sel = vs_blk[b_ix, g_ix, top_i].reshape(B, G, Tq, -1, dk)
    pos_s = top_i[..., None] * SEL_BLOCK + jnp.arange(SEL_BLOCK)
    m_s = jnp.isfinite(top_s)[..., None] & (pos_s <= pos_q[None, None, :, None, None])
    p_s = masked_softmax(jnp.einsum('bgrqd,bgqkd->bgrqk', q, k_sel), m_s.reshape(B, G, 1, Tq, -1))
    o_s = jnp.einsum('bgrqk,bgqkd->bgrqd', p_s.astype(v_sel.dtype), v_sel)
    dpos = pos_q[:, None] - pos_w[None, :]
    m_w = (dpos >= 0) & (dpos < WINDOW) & (pos_w[None, :] >= 0)
    p_w = masked_softmax(jnp.einsum('bgrqd,bgkd->bgrqk', q, kw), m_w)
    o_w = jnp.einsum('bgrqk,bgkd->bgrqd', p_w.astype(vw.dtype), vw)
    return gates[..., 0:1] * o_c + gates[..., 1:2] * o_s + gates[..., 2:3] * o_w


def nsa_prompt(q, gates, rows, win, cmp_pos, w_cmp, k_norm):
    B, G, R, T, dk = q.shape
    kc, vc, pos_c = compress_blocks(rows[:, :, 0], rows[:, :, 1], cmp_pos, w_cmp, k_norm[2])
    ks_blk = to_sel_blocks(rows[:, :, 2])
    vs_blk = to_sel_blocks(rows[:, :, 3])
    ovl = cmp_sel_overlap(kc.shape[2], ks_blk.shape[2])
    win_pad = jnp.pad(win, ((0, 0), (WINDOW, 0), (0, 0), (0, 0), (0, 0))).transpose(0, 2, 3, 1, 4)

    def block(i):
        s = i * Q_BLOCK
        pos_q = s + jnp.arange(Q_BLOCK)
        qb = lax.dynamic_slice_in_dim(q, s, Q_BLOCK, axis=3)
        gb = lax.dynamic_slice_in_dim(gates, s, Q_BLOCK, axis=3)
        wb = lax.dynamic_slice_in_dim(win_pad, s, WINDOW + Q_BLOCK, axis=3)
        pos_w = s - WINDOW + jnp.arange(WINDOW + Q_BLOCK)
        return nsa_core(qb, pos_q, gb, kc, vc, pos_c, ks_blk, vs_blk, wb[:, 0], wb[:, 1], pos_w, ovl)

    o = lax.map(block, jnp.arange(T // Q_BLOCK))
    return o.transpose(1, 0, 4, 2, 3, 5).reshape(B, T, G * R * dk)


def nsa_sample(q, gates, rows_new, win_new, cache_pool, page_table, win_buf, cmp_pos, w_cmp, k_norm):
    B, G, R, Tn, dk = q.shape
    past_len = page_table.shape[1] * cache_pool.shape[1]
    past = cache_pool[page_table].reshape(B, past_len, N_NSA_ROWS, G, dk)
    rows = jnp.concatenate([past, rows_new], axis=1)
    kc, vc, pos_c = compress_blocks(rows[:, :, 0], rows[:, :, 1], cmp_pos, w_cmp, k_norm[2])
    ks_blk = to_sel_blocks(rows[:, :, 2])
    vs_blk = to_sel_blocks(rows[:, :, 3])
    ovl = cmp_sel_overlap(kc.shape[2], ks_blk.shape[2])
    win_rows = win_buf.shape[1]
    win_all = jnp.concatenate([win_buf, win_new], axis=1)
    wt = win_all.transpose(0, 2, 3, 1, 4)
    pos_w = past_len - win_rows + jnp.arange(win_rows + Tn)
    pos_q = past_len + jnp.arange(Tn)
    o = nsa_core(q, pos_q, gates, kc, vc, pos_c, ks_blk, vs_blk, wt[:, 0], wt[:, 1], pos_w, ovl)
    return o.transpose(0, 3, 1, 2, 4).reshape(B, Tn, G * R * dk), win_all[:, -win_rows:]


def gated_delta_rule(q, k, v, g, beta, S0):
    B, T, H, dk = k.shape
    dv = v.shape[-1]
    C = min(DN_CHUNK, T)
    n = -(-T // C)
    pad = n * C - T

    def chunks(a):
        a = jnp.pad(a, ((0, 0), (0, pad)) + ((0, 0),) * (a.ndim - 2))
        return jnp.moveaxis(a.reshape((B, n, C) + a.shape[2:]), 3, 1)

    q, k, v, g, beta = (chunks(a) for a in (q, k, v, g, beta))
    Gc = jnp.cumsum(g, axis=-1)
    i = jnp.arange(C)
    incl = i[:, None] >= i[None, :]
    strict = i[:, None] > i[None, :]
    diff = Gc[..., :, None] - Gc[..., None, :]
    dmask = jnp.where(incl, jnp.exp(jnp.where(incl, diff, 0.0)), 0.0)
    kb = k * beta[..., None]
    A = jnp.where(strict, jnp.einsum('bhncd,bhnsd->bhncs', kb, k) * dmask, 0.0) + jnp.eye(C, dtype=k.dtype)
    u = lax.linalg.triangular_solve(A, v * beta[..., None], left_side=True, lower=True, unit_diagonal=True)
    w = lax.linalg.triangular_solve(A, kb * jnp.exp(Gc)[..., None], left_side=True, lower=True, unit_diagonal=True)
    qk = jnp.einsum('bhncd,bhnsd->bhncs', q, k) * dmask
    qg = q * jnp.exp(Gc)[..., None]
    g_last = Gc[..., -1]
    kd = k * jnp.exp(g_last[..., None] - Gc)[..., None]

    def step(S, xs):
        u_c, w_c, qg_c, qk_c, kd_c, gl_c = xs
        v_new = u_c - jnp.einsum('bhcd,bhde->bhce', w_c, S)
        o_c = jnp.einsum('bhcd,bhde->bhce', qg_c, S) + jnp.einsum('bhcs,bhse->bhce', qk_c, v_new)
        S = S * jnp.exp(gl_c)[..., None, None] + jnp.einsum('bhcd,bhce->bhde', kd_c, v_new)
        return S, o_c

    xs = tuple(jnp.moveaxis(a, 2, 0) for a in (u, w, qg, qk, kd, g_last))
    S, o = lax.scan(step, S0, xs)
    o = jnp.moveaxis(o, 0, 2).reshape(B, H, n * C, dv)[:, :, :T]
    return o.transpose(0, 2, 1, 3), S


def deltanet(qkv_raw, a_raw, b_raw, z_raw, conv_buf, S0, conv_w, A_log, dt_bias, dn_norm):
    B, T, _ = qkv_raw.shape
    xpad = jnp.concatenate([conv_buf.astype(qkv_raw.dtype), qkv_raw], axis=1)
    conv = xpad[:, 0:T] * conv_w[0]
    for j in range(1, DN_CONV):
        conv = conv + xpad[:, j:j + T] * conv_w[j]
    new_buf = xpad[:, T:]
    q, k, v = jnp.split(jax.nn.silu(conv), 3, axis=-1)
    shp = (B, T, DN_HEADS, DN_HEAD_DIM)
    q = l2norm(q.reshape(shp)) * (DN_HEAD_DIM ** -0.5)
    k = l2norm(k.reshape(shp))
    v = v.reshape(shp).astype(jnp.float32)
    beta = jax.nn.sigmoid(b_raw.astype(jnp.float32))
    g = -jnp.exp(A_log.astype(jnp.float32)) * jax.nn.softplus(a_raw.astype(jnp.float32) + dt_bias.astype(jnp.float32))
    o, S = gated_delta_rule(q, k, v, g, beta, S0.astype(jnp.float32))
    o = rmsnorm(o.astype(qkv_raw.dtype), dn_norm) * jax.nn.silu(z_raw.reshape(shp))
    return o.reshape(B, T, DN_WIDTH), new_buf, S.astype(S0.dtype)


def hier_moe(h, w_rg, b_rg, w_re, b_re, w_gate, w_up, w_down):
    M, D = h.shape
    p_grp = jax.nn.softmax((h @ w_rg).astype(jnp.float32) + b_rg.astype(jnp.float32), axis=-1)
    grp = jnp.argmax(p_grp, axis=-1)
    p_top = jnp.take_along_axis(p_grp, grp[:, None], axis=-1)
    logit_e = ((h @ w_re).astype(jnp.float32) + b_re.astype(jnp.float32)).reshape(M, N_GROUPS, EXPERTS_PER_GROUP)
    logit_in = jnp.take_along_axis(logit_e, grp[:, None, None], axis=1)[:, 0]
    top_v, top_i = lax.top_k(logit_in, TOP_K)
    weights = (p_top * jax.nn.softmax(top_v, axis=-1)).astype(h.dtype)
    expert = grp[:, None] * EXPERTS_PER_GROUP + top_i
    n_assign = M * TOP_K
    flat_e = expert.reshape(n_assign)
    order = jnp.argsort(flat_e)
    s_e = flat_e[order]
    s_tok = order // TOP_K
    s_w = weights.reshape(n_assign)[order]
    counts = jnp.zeros((N_EXPERTS,), jnp.int32).at[flat_e].add(1)
    padded = (counts + MOE_BLOCK - 1) // MOE_BLOCK * MOE_BLOCK
    pad_end = jnp.cumsum(padded)
    pad_start = pad_end - padded
    start = jnp.cumsum(counts) - counts
    dest = pad_start[s_e] + jnp.arange(n_assign) - start[s_e]
    n_blocks = (n_assign + N_EXPERTS * (MOE_BLOCK - 1) + MOE_BLOCK - 1) // MOE_BLOCK
    rows = jnp.zeros((n_blocks * MOE_BLOCK, D), h.dtype).at[dest].set(h[s_tok])
    blk_expert = jnp.minimum(jnp.searchsorted(pad_end, jnp.arange(n_blocks) * MOE_BLOCK, side='right'), N_EXPERTS - 1)

    def expert_block(args):
        xb, e = args
        return (jax.nn.silu(xb @ w_gate[e]) * (xb @ w_up[e])) @ w_down[e]

    out = lax.map(expert_block, (rows.reshape(n_blocks, MOE_BLOCK, D), blk_expert)).reshape(-1, D)
    return jnp.zeros((M, D), h.dtype).at[s_tok].add(out[dest] * s_w[:, None])


def merge_and_ffn(x, o_a, o_b, gate_a, gate_b, w_proj_a, w_proj_b, w_out, norm_ffn, w_rg, b_rg, w_re, b_re, w_gate, w_up, w_down):
    mixed = jax.nn.sigmoid(gate_a) * (o_a @ w_proj_a) + jax.nn.sigmoid(gate_b) * (o_b @ w_proj_b)
    x = x + mixed @ w_out
    B, T, D = x.shape
    y = hier_moe(rmsnorm(x, norm_ffn).reshape(B * T, D), w_rg, b_rg, w_re, b_re, w_gate, w_up, w_down)
    return x + y.reshape(B, T, D)


def setup_inputs(seed: int = 0) -> dict:
    key = jax.random.key(seed)
    ks = jax.random.split(key, 32)
    f32 = jnp.float32
    n_pages = PAST_LEN // PAGE_SIZE
    n_used = DEC_BATCH * n_pages
    n_pool = n_used + max(1, n_used // 4)
    win_rows = min(WINDOW, PAST_LEN)

    def nrm(k, shape, scale=1.0):
        return jax.random.normal(k, shape, f32) * scale

    page_table = jax.random.permutation(ks[3], n_pool)[:n_used].reshape(DEC_BATCH, n_pages).astype(jnp.int32)
    dt = jnp.exp(jax.random.uniform(ks[15], (DEPTH, DN_HEADS), f32, math.log(1e-3), math.log(1e-1)))
    return {
        'x_prompt': nrm(ks[0], (BATCH, SEQ, D_MODEL)),
        'x_sample': nrm(ks[1], (DEC_BATCH, DEC_SEQ, D_MODEL)),
        'cache_nsa': nrm(ks[2], (DEPTH, n_pool, PAGE_SIZE, N_NSA_ROWS, NSA_KV_HEADS, HEAD_DIM)),
        'page_table': page_table,
        'state_win': nrm(ks[4], (DEPTH, DEC_BATCH, win_rows, 2, NSA_KV_HEADS, HEAD_DIM)),
        'state_dn_conv': nrm(ks[5], (DEPTH, DEC_BATCH, DN_CONV - 1, 3 * DN_WIDTH)),
        'state_dn_S': nrm(ks[6], (DEPTH, DEC_BATCH, DN_HEADS, DN_HEAD_DIM, DN_HEAD_DIM), 0.1),
        'norm_mix': 1.0 + nrm(ks[7], (DEPTH, D_MODEL), 0.05),
        'w_in': nrm(ks[8], (DEPTH, D_MODEL, IN_COLS), D_MODEL ** -0.5),
        'q_norm': 1.0 + nrm(ks[9], (DEPTH, HEAD_DIM), 0.05),
        'k_norm': 1.0 + nrm(ks[10], (DEPTH, 3, HEAD_DIM), 0.05),
        'cmp_pos': nrm(ks[11], (DEPTH, 2, CMP_BLOCK, HEAD_DIM), 0.5),
        'w_cmp': nrm(ks[12], (DEPTH, 2, CMP_BLOCK, HEAD_DIM, HEAD_DIM), (CMP_BLOCK * HEAD_DIM) ** -0.5),
        'dn_conv_w': nrm(ks[13], (DEPTH, DN_CONV, 3 * DN_WIDTH), DN_CONV ** -0.5),
        'dn_A_log': jnp.log(jax.random.uniform(ks[14], (DEPTH, DN_HEADS), f32, 1.0, 16.0)),
        'dn_dt_bias': dt + jnp.log(-jnp.expm1(-dt)),
        'dn_norm': 1.0 + nrm(ks[16], (DEPTH, DN_HEAD_DIM), 0.05),
        'w_proj_a': nrm(ks[17], (DEPTH, NSA_Q_COLS, D_MODEL), NSA_Q_COLS ** -0.5),
        'w_proj_b': nrm(ks[18], (DEPTH, DN_WIDTH, D_MODEL), DN_WIDTH ** -0.5),
        'w_out': nrm(ks[19], (DEPTH, D_MODEL, D_MODEL), D_MODEL ** -0.5),
        'norm_ffn': 1.0 + nrm(ks[20], (DEPTH, D_MODEL), 0.05),
        'w_router_g': nrm(ks[21], (DEPTH, D_MODEL, N_GROUPS), D_MODEL ** -0.5),
        'b_router_g': nrm(ks[22], (DEPTH, N_GROUPS), 0.01),
        'w_router_e': nrm(ks[23], (DEPTH, D_MODEL, N_EXPERTS), D_MODEL ** -0.5),
        'b_router_e': nrm(ks[24], (DEPTH, N_EXPERTS), 0.01),
        'w_gate': nrm(ks[25], (DEPTH, N_EXPERTS, D_MODEL, D_EXPERT), D_MODEL ** -0.5),
        'w_up': nrm(ks[26], (DEPTH, N_EXPERTS, D_MODEL, D_EXPERT), D_MODEL ** -0.5),
        'w_down': nrm(ks[27], (DEPTH, N_EXPERTS, D_EXPERT, D_MODEL), D_EXPERT ** -0.5),
    }


def reference(x_prompt, x_sample, cache_nsa, page_table, state_win, state_dn_conv, state_dn_S,
              norm_mix, w_in, q_norm, k_norm, cmp_pos, w_cmp, dn_conv_w, dn_A_log, dn_dt_bias, dn_norm,
              w_proj_a, w_proj_b, w_out, norm_ffn, w_router_g, b_router_g, w_router_e, b_router_e,
              w_gate, w_up, w_down):
    B, T, _ = x_prompt.shape
    Bd, Tn, _ = x_sample.shape
    past_len = page_table.shape[1] * cache_nsa.shape[2]
    win_rows = state_win.shape[2]
    pos_p = jnp.arange(T)
    pos_d = past_len + jnp.arange(Tn)
    xp, xd = x_prompt, x_sample
    kv_p, kv_d, win_p, win_d, conv_p, conv_d, S_p, S_d = [], [], [], [], [], [], [], []
    for l in range(DEPTH):
        ffn_w = (norm_ffn[l], w_router_g[l], b_router_g[l], w_router_e[l], b_router_e[l], w_gate[l], w_up[l], w_down[l])
        dn_w = (dn_conv_w[l], dn_A_log[l], dn_dt_bias[l], dn_norm[l])
        (q_r, kc_r, vc_r, ks_r, vs_r, kw_r, vw_r, ng_r, dn_qkv, dn_a, dn_b, dn_z, mg_a, mg_b) = split_projection(rmsnorm(xp, norm_mix[l]), w_in[l])
        q, gts, rows, win = nsa_rows(q_r, kc_r, vc_r, ks_r, vs_r, kw_r, vw_r, ng_r, pos_p, q_norm[l], k_norm[l])
        o_a = nsa_prompt(q, gts, rows, win, cmp_pos[l], w_cmp[l], k_norm[l])
        o_b, cbuf, S = deltanet(dn_qkv, dn_a, dn_b, dn_z,
                                jnp.zeros((B, DN_CONV - 1, 3 * DN_WIDTH), xp.dtype),
                                jnp.zeros((B, DN_HEADS, DN_HEAD_DIM, DN_HEAD_DIM), state_dn_S.dtype), *dn_w)
        kv_p.append(rows)
        win_p.append(jnp.pad(win, ((0, 0), (max(0, win_rows - T), 0), (0, 0), (0, 0), (0, 0)))[:, -win_rows:])
        conv_p.append(cbuf)
        S_p.append(S)
        xp = merge_and_ffn(xp, o_a, o_b, mg_a, mg_b, w_proj_a[l], w_proj_b[l], w_out[l], *ffn_w)
        (q_r, kc_r, vc_r, ks_r, vs_r, kw_r, vw_r, ng_r, dn_qkv, dn_a, dn_b, dn_z, mg_a, mg_b) = split_projection(rmsnorm(xd, norm_mix[l]), w_in[l])
        q, gts, rows, win = nsa_rows(q_r, kc_r, vc_r, ks_r, vs_r, kw_r, vw_r, ng_r, pos_d, q_norm[l], k_norm[l])
        o_a, new_win = nsa_sample(q, gts, rows, win, cache_nsa[l], page_table, state_win[l], cmp_pos[l], w_cmp[l], k_norm[l])
        o_b, cbuf, S = deltanet(dn_qkv, dn_a, dn_b, dn_z, state_dn_conv[l], state_dn_S[l], *dn_w)
        kv_d.append(rows)
        win_d.append(new_win)
        conv_d.append(cbuf)
        S_d.append(S)
        xd = merge_and_ffn(xd, o_a, o_b, mg_a, mg_b, w_proj_a[l], w_proj_b[l], w_out[l], *ffn_w)
    return (xp, xd, jnp.stack(kv_p), jnp.stack(kv_d), jnp.stack(win_p), jnp.stack(win_d),
            jnp.stack(conv_p), jnp.stack(conv_d), jnp.stack(S_p), jnp.stack(S_d))
```

```python
import functools
import math

import numpy as np
import jax
import jax.numpy as jnp
from jax import lax
from jax.experimental import pallas as pl
from jax.experimental.pallas import tpu as pltpu

F32 = jnp.float32
BF16 = jnp.bfloat16

NSA_HEADS = 8
NSA_KV_HEADS = 2
NSA_GROUP = NSA_HEADS // NSA_KV_HEADS
HEAD_DIM = 64
ROPE_DIM = HEAD_DIM // 4
ROPE_HALF = ROPE_DIM // 2
ROPE_THETA = 500000.0
CMP_BLOCK = 32
CMP_STRIDE = 16
SEL_BLOCK = 64
SEL_SHIFT = 6
SEL_TOPN = 16
WINDOW = 512
FORCE_BONUS = 1000.0
DN_HEADS = 4
DN_HEAD_DIM = 128
DN_CONV = 4
DN_CHUNK = 64
N_GROUPS = 4
EXPERTS_PER_GROUP = 8
TOP_K = 2
EPS = 1e-6

LANES = 128
SUBLANES = 8
VMEM_LIMIT = 56 * 1024 * 1024
NEG = -1e30
Q_BLOCK = 128
SEL_CHUNK = 256
HEAD_PERM = (0, 4, 1, 5, 2, 6, 3, 7)
SM_A, SM_B, SM_GATE = 0, DN_HEADS, 2 * DN_HEADS
EXPERT_LANE0 = 32


def _cparams(sem):
    return pltpu.CompilerParams(dimension_semantics=sem, vmem_limit_bytes=VMEM_LIMIT)


def _dot(a, b):
    return jnp.dot(a, b, preferred_element_type=F32)


def _dot_nt(a, b):
    return lax.dot_general(a, b, (((1,), (1,)), ((), ())), preferred_element_type=F32)


def _dot_tn(a, b):
    return lax.dot_general(a, b, (((0,), (0,)), ((), ())), preferred_element_type=F32)


def _split2(x):
    hi = x.astype(BF16)
    lo = (x - hi.astype(F32)).astype(BF16)
    return hi, lo


def _split3(x):
    hi = x.astype(BF16)
    r = x - hi.astype(F32)
    mid = r.astype(BF16)
    lo = (r - mid.astype(F32)).astype(BF16)
    return hi, mid, lo


def _dot3(a, b):
    ah, al = _split2(a)
    bh, bl = _split2(b)
    return _dot(ah, bh) + (_dot(ah, bl) + _dot(al, bh))


def _sigmoid(x):
    return 1.0 / (1.0 + jnp.exp(-x))


def _seg_sumsq(x, bd):
    hi, lo = _split2(x * x)
    return _dot(hi, bd) + _dot(lo, bd)


def _rope128(x, c, s1, s2):
    return x * c + pltpu.roll(x, LANES - ROPE_HALF, 1) * s1 + pltpu.roll(x, ROPE_HALF, 1) * s2


def _head_norm_rope(v, gain, bd, c, s1, s2):
    ss = _seg_sumsq(v, bd) * (1.0 / HEAD_DIM)
    return _rope128(v * lax.rsqrt(ss + EPS) * gain, c, s1, s2)


def _masked_softmax(s, valid):
    s = jnp.where(valid, s, NEG)
    m = jnp.max(s, axis=-1, keepdims=True)
    e = jnp.where(valid, jnp.exp(s - m), 0.0)
    l = jnp.sum(e, axis=-1, keepdims=True)
    return e * (1.0 / jnp.maximum(l, 1e-30))


def _topk_mask(score, k):
    lane = lax.broadcasted_iota(jnp.int32, score.shape, 1).astype(F32)
    big = float(score.shape[-1])

    def body(_, carry):
        sc, sel = carry
        m = jnp.max(sc, axis=-1, keepdims=True)
        idx = jnp.min(jnp.where(sc == m, lane, big), axis=-1, keepdims=True)
        hit = lane == idx
        sel = jnp.where(hit, jnp.maximum(sel, jnp.where(m > -jnp.inf, 1.0, 0.0)), sel)
        sc = jnp.where(hit, -jnp.inf, sc)
        return sc, sel

    _, sel = lax.fori_loop(0, k, body, (score, jnp.zeros(score.shape, F32)))
    return sel


def _proj_layout(dnw, d):
    sizes = dict(q=NSA_HEADS * HEAD_DIM, kvc=2 * LANES, ks=LANES, vs=LANES, kw=LANES, vw=LANES,
                 small=LANES, dnqkv=3 * dnw, dnz=dnw, mg=2 * d)
    offs, o = {}, 0
    for name, n in sizes.items():
        offs[name] = (o, o + n)
        o += n
    return offs, o


def _proj_kernel(x_ref, gin_ref, w_ref, c_ref, s1_ref, s2_ref, qg_ref, kg_ref, bd_ref,
                 q_out, rows_out, win_out, kvb_out, small_out, dnqkv_out, dnz_out, mg_out, *, offs):
    x = x_ref[...]
    ms = jnp.mean(x * x, axis=-1, keepdims=True)
    hb = ((x * lax.rsqrt(ms + EPS)) * gin_ref[...]).astype(BF16)

    def mm(name, lo=0, hi=None):
        a, b = offs[name]
        hi = b - a if hi is None else hi
        return _dot(hb, w_ref[:, a + lo:a + hi])

    c, s1, s2 = c_ref[...], s1_ref[...], s2_ref[...]
    bd = bd_ref[...]
    for j in range(NSA_HEADS * HEAD_DIM // LANES):
        qj = _head_norm_rope(mm("q", LANES * j, LANES * (j + 1)), qg_ref[...], bd, c, s1, s2)
        q_out[:, LANES * j:LANES * (j + 1)] = (qj * (HEAD_DIM ** -0.5)).astype(BF16)
    rows_out[:, 0:2 * LANES] = mm("kvc")
    ks = _head_norm_rope(mm("ks"), kg_ref[0:1, :], bd, c, s1, s2)
    vs = mm("vs")
    rows_out[:, 2 * LANES:3 * LANES] = ks
    rows_out[:, 3 * LANES:4 * LANES] = vs
    kw = _head_norm_rope(mm("kw"), kg_ref[1:2, :], bd, c, s1, s2)
    vw = mm("vw")
    win_out[:, 0:LANES] = kw
    win_out[:, LANES:2 * LANES] = vw
    kvb_out[:, 0:LANES] = ks.astype(BF16)
    kvb_out[:, LANES:2 * LANES] = vs.astype(BF16)
    kvb_out[:, 2 * LANES:3 * LANES] = kw.astype(BF16)
    kvb_out[:, 3 * LANES:4 * LANES] = vw.astype(BF16)
    small_out[...] = mm("small")
    dnqkv_out[...] = mm("dnqkv")
    dnz_out[...] = mm("dnz")
    mg_out[...] = _sigmoid(mm("mg"))


def _proj_stage(x2d, pw, tabs, tm, n_tab_blocks):
    m, d = x2d.shape
    offs, ncols = pw["offs"], pw["ncols"]
    dnw = offs["dnz"][1] - offs["dnz"][0]
    row = lambda w: pl.BlockSpec((tm, w), lambda i: (i, 0))
    full = lambda a: pl.BlockSpec(a.shape, lambda i: (0,) * a.ndim)
    tab = pl.BlockSpec((tm, LANES), lambda i: (i % n_tab_blocks, 0))
    out_shape = (
        jax.ShapeDtypeStruct((m, 4 * LANES), BF16),
        jax.ShapeDtypeStruct((m, 4 * LANES), F32),
        jax.ShapeDtypeStruct((m, 2 * LANES), F32),
        jax.ShapeDtypeStruct((m, 4 * LANES), BF16),
        jax.ShapeDtypeStruct((m, LANES), F32),
        jax.ShapeDtypeStruct((m, 3 * dnw), F32),
        jax.ShapeDtypeStruct((m, dnw), F32),
        jax.ShapeDtypeStruct((m, 2 * d), F32),
    )
    return pl.pallas_call(
        functools.partial(_proj_kernel, offs=offs),
        grid=(m // tm,),
        in_specs=[row(d), full(pw["gin"]), full(pw["w"]), tab, tab, tab,
                  full(pw["qg"]), full(pw["kg"]), full(pw["bd"])],
        out_specs=tuple(row(s.shape[1]) for s in out_shape),
        out_shape=out_shape,
        compiler_params=_cparams(("parallel",)),
    )(x2d, pw["gin"], pw["w"], tabs[0], tabs[1], tabs[2], pw["qg"], pw["kg"], pw["bd"])


def _compress_body(xk_ref, xv_ref, w_ref, p_ref, kg_ref, bd_ref, c_ref, s1_ref, s2_ref, out_ref):
    nb = xk_ref.shape[0] // CMP_STRIDE
    acc = [jnp.zeros((nb, LANES), F32) for _ in range(4)]
    for l in range(CMP_STRIDE):
        xs = (xk_ref[pl.ds(l, nb, stride=CMP_STRIDE), :], xv_ref[pl.ds(l, nb, stride=CMP_STRIDE), :])
        for j in range(4):
            acc[j] = acc[j] + _dot((xs[j // 2] + p_ref[j, l:l + 1, :]).astype(BF16), w_ref[j, l])
    kraw = acc[0] + pltpu.roll(acc[1], nb - 1, 0)
    vraw = acc[2] + pltpu.roll(acc[3], nb - 1, 0)
    kc = _head_norm_rope(kraw, kg_ref[...], bd_ref[...], c_ref[...], s1_ref[...], s2_ref[...])
    out_ref[:, 0:LANES] = kc.astype(BF16)
    out_ref[:, LANES:2 * LANES] = vraw.astype(BF16)


def _compress_prompt_kernel(xk_ref, xv_ref, *rest):
    _compress_body(xk_ref, xv_ref, *rest)


def _compress_prompt(rows3, cw, tabs):
    b, t, _ = rows3.shape
    nb = t // CMP_STRIDE
    full = lambda a: pl.BlockSpec(a.shape, lambda i: (0,) * a.ndim)
    consts = (cw["w"], cw["p"], cw["kg"], cw["bd"]) + tuple(tabs)
    return pl.pallas_call(
        _compress_prompt_kernel,
        grid=(b,),
        in_specs=[pl.BlockSpec((None, t, LANES), lambda i: (i, 0, 0)),
                  pl.BlockSpec((None, t, LANES), lambda i: (i, 0, 1))] + [full(a) for a in consts],
        out_specs=pl.BlockSpec((None, nb, 2 * LANES), lambda i: (i, 0, 0)),
        out_shape=jax.ShapeDtypeStruct((b, nb, 2 * LANES), BF16),
        compiler_params=_cparams(("parallel",)),
    )(rows3, rows3, *consts)


def _page_copies(pt_ref, cache_ref, buf_ref, sem, b, n_pages, page, lane0):
    def copy(p, j):
        return pltpu.make_async_copy(
            cache_ref.at[pt_ref[b, p], :, pl.ds(lane0 + j * LANES, LANES)],
            buf_ref.at[j, pl.ds(pl.multiple_of(p * page, page), page), :], sem)

    def start(p, c):
        copy(p, 0).start()
        copy(p, 1).start()
        return c

    def wait(p, c):
        copy(p, 0).wait()
        copy(p, 1).wait()
        return c

    return (lambda: lax.fori_loop(0, n_pages, start, 0)), (lambda: lax.fori_loop(0, n_pages, wait, 0))


def _compress_sample_kernel(pt_ref, cache_ref, *rest, n_pages, page):
    *consts, out_ref, buf_ref, sem = rest
    start, wait = _page_copies(pt_ref, cache_ref, buf_ref, sem, pl.program_id(0), n_pages, page, 0)
    start()
    wait()
    _compress_body(buf_ref.at[0], buf_ref.at[1], *consts, out_ref)


def _compress_sample(cache3, page_table, cw, tabs):
    bd_, n_pages = page_table.shape
    page = cache3.shape[1]
    past = n_pages * page
    nb = past // CMP_STRIDE
    full = lambda a: pl.BlockSpec(a.shape, lambda i, pt: (0,) * a.ndim)
    consts = (cw["w"], cw["p"], cw["kg"], cw["bd"]) + tuple(tabs)
    grid_spec = pltpu.PrefetchScalarGridSpec(
        num_scalar_prefetch=1,
        grid=(bd_,),
        in_specs=[pl.BlockSpec(memory_space=pl.ANY)] + [full(a) for a in consts],
        out_specs=pl.BlockSpec((None, nb, 2 * LANES), lambda i, pt: (i, 0, 0)),
        scratch_shapes=[pltpu.VMEM((2, past, LANES), F32), pltpu.SemaphoreType.DMA(())],
    )
    return pl.pallas_call(
        functools.partial(_compress_sample_kernel, n_pages=n_pages, page=page),
        grid_spec=grid_spec,
        out_shape=jax.ShapeDtypeStruct((bd_, nb, 2 * LANES), BF16),
        compiler_params=_cparams(("arbitrary",)),
    )(page_table, cache3, *consts)


def _stack_heads(q, lo_half):
    zero = jnp.zeros((), q.dtype)
    parts = []
    for g in range(NSA_KV_HEADS):
        for r in range(NSA_GROUP):
            col = q[:, LANES * r:LANES * (r + 1)]
            parts.append(jnp.where(lo_half if g == 0 else jnp.logical_not(lo_half), col, zero))
    return jnp.concatenate(parts, axis=0)


def _sel_scores(imp, pos_q, blk):
    cur = pos_q >> SEL_SHIFT
    readable = blk * SEL_BLOCK <= pos_q
    forced = jnp.logical_or(blk == 0, jnp.logical_or(blk == cur, blk == cur - 1))
    return jnp.where(readable, imp + jnp.where(forced, FORCE_BONUS, 0.0), -jnp.inf)


def _attn_prompt_kernel(q_ref, small_ref, kvb_ref, kcv_ref, ovl_ref, o_ref, *, n_cmp):
    qb = Q_BLOCK
    m8 = NSA_HEADS * qb
    i = pl.program_id(1)
    q0 = i * qb
    lane = lax.broadcasted_iota(jnp.int32, (qb, LANES), 1)
    lo_half = lane < HEAD_DIM
    q8 = _stack_heads(q_ref[...], lo_half)

    def qpos(n):
        return q0 + (lax.broadcasted_iota(jnp.int32, (m8, n), 0) & (qb - 1))

    def kidx(n):
        return lax.broadcasted_iota(jnp.int32, (m8, n), 1)

    nb = kcv_ref.shape[0]
    s = _dot_nt(q8, kcv_ref[:, 0:LANES])
    cidx = kidx(nb)
    valid = jnp.logical_and(cidx < n_cmp, cidx * CMP_STRIDE + (CMP_BLOCK - 1) <= qpos(nb))
    p = _masked_softmax(s, valid).astype(BF16)
    o_c = _dot(p, kcv_ref[:, LANES:2 * LANES])
    imp8 = _dot(p, ovl_ref[...])

    ns = ovl_ref.shape[1]
    imps = []
    for g in range(NSA_KV_HEADS):
        acc = imp8[(g * NSA_GROUP) * qb:(g * NSA_GROUP + 1) * qb]
        for r in range(1, NSA_GROUP):
            acc = acc + imp8[(g * NSA_GROUP + r) * qb:(g * NSA_GROUP + r + 1) * qb]
        imps.append(acc)
    imp = jnp.concatenate(imps, axis=0)
    pos2 = q0 + (lax.broadcasted_iota(jnp.int32, (2 * qb, ns), 0) & (qb - 1))
    blk2 = lax.broadcasted_iota(jnp.int32, (2 * qb, ns), 1)
    sel = _topk_mask(_sel_scores(imp, pos2, blk2), SEL_TOPN)
    selb = jnp.where(sel > 0.0, 0.0, NEG).astype(BF16)
    sb8 = jnp.concatenate([selb[0:qb]] * NSA_GROUP + [selb[qb:2 * qb]] * NSA_GROUP, axis=0)
    qaug = jnp.concatenate([q8, sb8], axis=1)

    ck = SEL_CHUNK
    qpos_k = qpos(ck)
    kk = lax.broadcasted_iota(jnp.int32, (ck, ns), 0)
    ss = lax.broadcasted_iota(jnp.int32, (ck, ns), 1)

    def chunk(c, carry):
        m, l, acc = carry
        k0 = pl.multiple_of(c * ck, ck)
        kmat = kvb_ref[pl.ds(k0, ck), 0:LANES]
        vmat = kvb_ref[pl.ds(k0, ck), LANES:2 * LANES]
        e = jnp.where(((k0 + kk) >> SEL_SHIFT) == ss, 1.0, 0.0).astype(BF16)
        sc = _dot_nt(qaug, jnp.concatenate([kmat, e], axis=1))
        sc = jnp.where(k0 + kidx(ck) <= qpos_k, sc, NEG)
        m_new = jnp.maximum(m, jnp.max(sc, axis=-1, keepdims=True))
        alpha = jnp.exp(m - m_new)
        pe = jnp.exp(sc - m_new)
        l = alpha * l + jnp.sum(pe, axis=-1, keepdims=True)
        acc = alpha * acc + _dot(pe.astype(BF16), vmat)
        return m_new, l, acc

    n_chunks = (q0 + qb + ck - 1) // ck
    init = (jnp.full((m8, 1), NEG, F32), jnp.zeros((m8, 1), F32), jnp.zeros((m8, LANES), F32))
    _, l_s, acc_s = lax.fori_loop(0, n_chunks, chunk, init)
    o_s = acc_s * (1.0 / l_s)

    wk = WINDOW + qb
    ws = pl.multiple_of(jnp.maximum(q0 - WINDOW, 0), qb)
    s = _dot_nt(q8, kvb_ref[pl.ds(ws, wk), 2 * LANES:3 * LANES])
    dpos = qpos(wk) - (ws + kidx(wk))
    p = _masked_softmax(s, jnp.logical_and(dpos >= 0, dpos < WINDOW)).astype(BF16)
    o_w = _dot(p, kvb_ref[pl.ds(ws, wk), 3 * LANES:4 * LANES])

    sig = _sigmoid(small_ref[...])
    for r in range(NSA_GROUP):
        outs = []
        for g in range(NSA_KV_HEADS):
            h = g * NSA_GROUP + r
            rows = slice(h * qb, (h + 1) * qb)
            gl = SM_GATE + 3 * h
            outs.append(sig[:, gl:gl + 1] * o_c[rows] + sig[:, gl + 1:gl + 2] * o_s[rows]
                        + sig[:, gl + 2:gl + 3] * o_w[rows])
        o_ref[:, LANES * r:LANES * (r + 1)] = jnp.where(lo_half, outs[0], outs[1]).astype(BF16)


def _attn_prompt(q2d, small2d, kvb3, kcv3, ovl, n_cmp):
    b, t, _ = kvb3.shape
    nq = t // Q_BLOCK
    nb = kcv3.shape[1]
    return pl.pallas_call(
        functools.partial(_attn_prompt_kernel, n_cmp=n_cmp),
        grid=(b, nq),
        in_specs=[pl.BlockSpec((Q_BLOCK, 4 * LANES), lambda bi, i: (bi * nq + i, 0)),
                  pl.BlockSpec((Q_BLOCK, LANES), lambda bi, i: (bi * nq + i, 0)),
                  pl.BlockSpec((None, t, 4 * LANES), lambda bi, i: (bi, 0, 0)),
                  pl.BlockSpec((None, nb, 2 * LANES), lambda bi, i: (bi, 0, 0)),
                  pl.BlockSpec(ovl.shape, lambda bi, i: (0, 0))],
        out_specs=pl.BlockSpec((Q_BLOCK, 4 * LANES), lambda bi, i: (bi * nq + i, 0)),
        out_shape=jax.ShapeDtypeStruct((b * t, 4 * LANES), BF16),
        compiler_params=_cparams(("parallel", "arbitrary")),
    )(q2d, small2d, kvb3, kcv3, ovl)


def _attn_sample_kernel(pt_ref, cache_ref, q_ref, small_ref, rows_ref, win_ref, kcv_ref, swin_ref, ovl_ref,
                        o_ref, buf_ref, sem, *, n_pages, page, n_cmp, n_sel, key_chunk):
    past = n_pages * page
    start, wait = _page_copies(pt_ref, cache_ref, buf_ref, sem, pl.program_id(0), n_pages, page, 2 * LANES)
    start()
    lane1 = lax.broadcasted_iota(jnp.int32, (1, LANES), 1)
    lo1 = lane1 < HEAD_DIM
    q8f = _stack_heads(q_ref[...].astype(F32), lo1)
    q8 = q8f.astype(BF16)
    nh = NSA_HEADS

    nb = kcv_ref.shape[0]
    s = _dot_nt(q8, kcv_ref[:, 0:LANES])
    p = _masked_softmax(s, lax.broadcasted_iota(jnp.int32, (nh, nb), 1) < n_cmp).astype(BF16)
    o_c = _dot(p, kcv_ref[:, LANES:2 * LANES])
    imp8 = _dot(p, ovl_ref[...])
    nsp = ovl_ref.shape[1]
    blk = lax.broadcasted_iota(jnp.int32, (1, nsp), 1)
    selbs = []
    for g in range(NSA_KV_HEADS):
        imp = jnp.sum(imp8[g * NSA_GROUP:(g + 1) * NSA_GROUP], axis=0, keepdims=True)
        score = jnp.where(blk < n_sel, _sel_scores(imp, past, blk), -jnp.inf)
        selb = jnp.where(_topk_mask(score, SEL_TOPN) > 0.0, 0.0, NEG)
        selbs += [selb] * NSA_GROUP
    selb8 = jnp.concatenate(selbs, axis=0)
    selb8_b = selb8.astype(BF16)

    wait()
    ks_new = rows_ref[:, 2 * LANES:3 * LANES].astype(BF16).astype(F32)
    vs_new = rows_ref[:, 3 * LANES:4 * LANES].astype(BF16).astype(F32)
    s_new = jnp.sum(q8f * ks_new, axis=-1, keepdims=True) + selb8[:, n_sel - 1:n_sel]
    ck = key_chunk
    n_ck = past // ck
    ss = lax.broadcasted_iota(jnp.int32, (nsp, ck), 0)
    kk = lax.broadcasted_iota(jnp.int32, (nsp, ck), 1)
    scores = []
    for c in range(n_ck):
        kmat = buf_ref[0, c * ck:(c + 1) * ck, :].astype(BF16)
        e = jnp.where(((c * ck + kk) >> SEL_SHIFT) == ss, 1.0, 0.0).astype(BF16)
        scores.append(_dot_nt(q8, kmat) + _dot(selb8_b, e))
    m = s_new
    for sc in scores:
        m = jnp.maximum(m, jnp.max(sc, axis=-1, keepdims=True))
    p_new = jnp.exp(s_new - m)
    l = p_new
    acc = p_new * vs_new
    for c, sc in enumerate(scores):
        pe = jnp.exp(sc - m)
        l = l + jnp.sum(pe, axis=-1, keepdims=True)
        acc = acc + _dot(pe.astype(BF16), buf_ref[1, c * ck:(c + 1) * ck, :].astype(BF16))
    o_s = acc * (1.0 / l)

    wr = swin_ref.shape[0]
    s = _dot_nt(q8, swin_ref[:, 0:LANES].astype(BF16))
    j = lax.broadcasted_iota(jnp.int32, (nh, wr), 1)
    valid = wr - j < WINDOW
    kw_new = win_ref[:, 0:LANES].astype(BF16).astype(F32)
    vw_new = win_ref[:, LANES:2 * LANES].astype(BF16).astype(F32)
    s_new = jnp.sum(q8f * kw_new, axis=-1, keepdims=True)
    s = jnp.where(valid, s, NEG)
    m = jnp.maximum(s_new, jnp.max(s, axis=-1, keepdims=True))
    pe = jnp.where(valid, jnp.exp(s - m), 0.0)
    p_new = jnp.exp(s_new - m)
    l = p_new + jnp.sum(pe, axis=-1, keepdims=True)
    o_w = (p_new * vw_new + _dot(pe.astype(BF16), swin_ref[:, LANES:2 * LANES].astype(BF16))) * (1.0 / l)

    sig = _sigmoid(small_ref[...])
    for r in range(NSA_GROUP):
        outs = []
        for g in range(NSA_KV_HEADS):
            h = g * NSA_GROUP + r
            gl = SM_GATE + 3 * h
            outs.append(sig[:, gl:gl + 1] * o_c[h:h + 1] + sig[:, gl + 1:gl + 2] * o_s[h:h + 1]
                        + sig[:, gl + 2:gl + 3] * o_w[h:h + 1])
        o_ref[:, LANES * r:LANES * (r + 1)] = jnp.where(lo1, outs[0], outs[1]).astype(BF16)


def _attn_sample(cache3, page_table, q2d, small2d, rows2d, win2d, kcv3, swin3, ovl, n_cmp, n_sel):
    bd_, n_pages = page_table.shape
    page = cache3.shape[1]
    past = n_pages * page
    key_chunk = math.gcd(past, 2048)
    per_seq = lambda a: pl.BlockSpec((None,) + a.shape[1:], lambda i, pt: (i,) + (0,) * (a.ndim - 1))
    q3, small3, rows3, win3 = (a[:, None, :] for a in (q2d, small2d, rows2d, win2d))
    grid_spec = pltpu.PrefetchScalarGridSpec(
        num_scalar_prefetch=1,
        grid=(bd_,),
        in_specs=[pl.BlockSpec(memory_space=pl.ANY), per_seq(q3), per_seq(small3), per_seq(rows3), per_seq(win3),
                  per_seq(kcv3), per_seq(swin3), pl.BlockSpec(ovl.shape, lambda i, pt: (0, 0))],
        out_specs=pl.BlockSpec((None, 1, 4 * LANES), lambda i, pt: (i, 0, 0)),
        scratch_shapes=[pltpu.VMEM((2, past, LANES), F32), pltpu.SemaphoreType.DMA(())],
    )
    out = pl.pallas_call(
        functools.partial(_attn_sample_kernel, n_pages=n_pages, page=page, n_cmp=n_cmp, n_sel=n_sel,
                          key_chunk=key_chunk),
        grid_spec=grid_spec,
        out_shape=jax.ShapeDtypeStruct((bd_, 1, 4 * LANES), BF16),
        compiler_params=_cparams(("arbitrary",)),
    )(page_table, cache3, q3, small3, rows3, win3, kcv3, swin3, ovl)
    return out[:, 0, :]


def _dn_kernel(x_ref, small_ref, z_ref, cw_ref, c0_ref, s0_ref, alog_ref, dtb_ref, gn_ref,
               o_ref, sout_ref, s_scr, xs_scr, *, chunk, tb, t_valid):
    j = pl.program_id(1)
    dnw = z_ref.shape[1]
    hd = DN_HEAD_DIM
    hist = SUBLANES

    @pl.when(j == 0)
    def _():
        s_scr[...] = s0_ref[...]
        xs_scr[0:hist, :] = c0_ref[...]

    xs_scr[hist:hist + tb, :] = x_ref[...]
    conv = xs_scr[pl.ds(hist - (DN_CONV - 1), tb), :] * cw_ref[0:1, :]
    for jj in range(1, DN_CONV):
        conv = conv + xs_scr[pl.ds(hist - (DN_CONV - 1) + jj, tb), :] * cw_ref[jj:jj + 1, :]
    xs_scr[0:hist, :] = xs_scr[tb:tb + hist, :]
    act = conv * _sigmoid(conv)

    small = small_ref[...]
    tpos = j * tb + lax.broadcasted_iota(jnp.int32, (tb, LANES), 0)
    live = tpos < t_valid
    xg = small + dtb_ref[...]
    softplus = jnp.maximum(xg, 0.0) + jnp.log(1.0 + jnp.exp(-jnp.abs(xg)))
    g_all = jnp.where(live, -jnp.exp(alog_ref[...]) * softplus, 0.0)
    beta_all = jnp.where(live, _sigmoid(small), 0.0)

    ri = lax.broadcasted_iota(jnp.int32, (chunk, chunk), 0)
    ci = lax.broadcasted_iota(jnp.int32, (chunk, chunk), 1)
    incl = ri >= ci
    strict = ri > ci
    ltri = jnp.where(incl, 1.0, 0.0).astype(BF16)
    eye = jnp.where(ri == ci, 1.0, 0.0)
    lane_c = lax.broadcasted_iota(jnp.int32, (chunk, LANES), 1)
    n_sq = max(1, int(math.ceil(math.log2(chunk))))

    for c in range(tb // chunk):
        rs = slice(c * chunk, (c + 1) * chunk)
        g3 = _split3(g_all[rs])
        gc_all = _dot(ltri, g3[0]) + (_dot(ltri, g3[1]) + _dot(ltri, g3[2]))
        gc3 = _split3(gc_all)
        for h in range(DN_HEADS):
            q = act[rs, h * hd:(h + 1) * hd]
            k = act[rs, dnw + h * hd:dnw + (h + 1) * hd]
            v = act[rs, 2 * dnw + h * hd:2 * dnw + (h + 1) * hd]
            q = q * lax.rsqrt(jnp.sum(q * q, axis=-1, keepdims=True) + EPS) * (hd ** -0.5)
            k = k * lax.rsqrt(jnp.sum(k * k, axis=-1, keepdims=True) + EPS)
            beta = beta_all[rs, SM_B + h:SM_B + h + 1]
            gc = gc_all[:, SM_A + h:SM_A + h + 1]
            pick = jnp.where(lane_c == SM_A + h, 1.0, 0.0).astype(BF16)
            gc_row = _dot_nt(pick, gc3[0]) + (_dot_nt(pick, gc3[1]) + _dot_nt(pick, gc3[2]))
            dmask = jnp.where(incl, jnp.exp(jnp.where(incl, gc - gc_row, 0.0)), 0.0)
            kb = k * beta
            kbf = k.astype(BF16)
            a_strict = jnp.where(strict, _dot_nt(kb.astype(BF16), kbf) * dmask, 0.0)
            npow = -a_strict
            tinv = eye + npow
            for _ in range(n_sq - 1):
                npow = _dot3(npow, npow)
                tinv = tinv + _dot3(tinv, npow)
            tb16 = tinv.astype(BF16)
            egc = jnp.exp(gc)
            u = _dot(tb16, (v * beta).astype(BF16))
            w = _dot(tb16, (kb * egc).astype(BF16))
            qk = _dot_nt(q.astype(BF16), kbf) * dmask
            g_last = gc[chunk - 1:chunk, :]
            kd = k * jnp.exp(g_last - gc)
            s_old = s_scr[h]
            s_b = s_old.astype(BF16)
            v_new = u - _dot(w.astype(BF16), s_b)
            o = _dot((q * egc).astype(BF16), s_b) + _dot(qk.astype(BF16), v_new.astype(BF16))
            s_scr[h] = s_old * jnp.exp(g_last) + _dot_tn(kd.astype(BF16), v_new.astype(BF16))
            on = o * lax.rsqrt(jnp.mean(o * o, axis=-1, keepdims=True) + EPS) * gn_ref[...]
            z = z_ref[rs, h * hd:(h + 1) * hd]
            o_ref[rs, h * hd:(h + 1) * hd] = (on * (z * _sigmoid(z))).astype(BF16)

    @pl.when(j == pl.num_programs(1) - 1)
    def _():
        sout_ref[...] = s_scr[...]


def _dn_stage(x3, small3, z3, conv0, s0, dw, chunk, tb, t_valid):
    b, tpad, w3 = x3.shape
    dnw = w3 // 3
    nblk = tpad // tb
    full = lambda a: pl.BlockSpec(a.shape, lambda bi, j: (0,) * a.ndim)
    tok = lambda w: pl.BlockSpec((None, tb, w), lambda bi, j: (bi, j, 0))
    o, s_out = pl.pallas_call(
        functools.partial(_dn_kernel, chunk=chunk, tb=tb, t_valid=t_valid),
        grid=(b, nblk),
        in_specs=[tok(w3), tok(LANES), tok(dnw), full(dw["cw"]),
                  pl.BlockSpec((None, SUBLANES, w3), lambda bi, j: (bi, 0, 0)),
                  pl.BlockSpec((None, DN_HEADS, DN_HEAD_DIM, DN_HEAD_DIM), lambda bi, j: (bi, 0, 0, 0)),
                  full(dw["alog"]), full(dw["dtb"]), full(dw["gn"])],
        out_specs=(tok(dnw),
                   pl.BlockSpec((None, DN_HEADS, DN_HEAD_DIM, DN_HEAD_DIM), lambda bi, j: (bi, 0, 0, 0))),
        out_shape=(jax.ShapeDtypeStruct((b, tpad, dnw), BF16),
                   jax.ShapeDtypeStruct((b, DN_HEADS, DN_HEAD_DIM, DN_HEAD_DIM), F32)),
        scratch_shapes=[pltpu.VMEM((DN_HEADS, DN_HEAD_DIM, DN_HEAD_DIM), F32),
                        pltpu.VMEM((tb + SUBLANES, w3), F32)],
        compiler_params=_cparams(("parallel", "arbitrary")),
    )(x3, small3, z3, dw["cw"], conv0, s0, dw["alog"], dw["dtb"], dw["gn"])
    return o, s_out


def _merge_kernel(x_ref, oa_ref, ob_ref, mg_ref, wpa_ref, wpb_ref, wout_ref, gf_ref, wr_ref, br_ref,
                  x1_ref, h_ref, route_ref):
    d = x_ref.shape[1]
    mixed = mg_ref[:, 0:d] * _dot(oa_ref[...], wpa_ref[...]) + mg_ref[:, d:2 * d] * _dot(ob_ref[...], wpb_ref[...])
    x1 = x_ref[...] + _dot(mixed.astype(BF16), wout_ref[...])
    x1_ref[...] = x1
    h = (x1 * lax.rsqrt(jnp.mean(x1 * x1, axis=-1, keepdims=True) + EPS)) * gf_ref[...]
    h_ref[...] = h
    logits = _dot(h.astype(BF16), wr_ref[...]) + br_ref[...]
    lane = lax.broadcasted_iota(jnp.int32, logits.shape, 1)
    lanef = lane.astype(F32)
    big = float(LANES)
    is_g = lane < N_GROUPS
    lg = jnp.where(is_g, logits, NEG)
    eg = jnp.where(is_g, jnp.exp(lg - jnp.max(lg, axis=-1, keepdims=True)), 0.0)
    pg = eg / jnp.sum(eg, axis=-1, keepdims=True)
    p_top = jnp.max(pg, axis=-1, keepdims=True)
    grp = jnp.min(jnp.where(jnp.logical_and(is_g, pg == p_top), lanef, big), axis=-1, keepdims=True)
    e_lo = EXPERT_LANE0 + grp * EXPERTS_PER_GROUP
    in_grp = jnp.logical_and(lanef >= e_lo, lanef < e_lo + EXPERTS_PER_GROUP)
    le = jnp.where(in_grp, logits, -jnp.inf)
    v0 = jnp.max(le, axis=-1, keepdims=True)
    i0 = jnp.min(jnp.where(le == v0, lanef, big), axis=-1, keepdims=True)
    le1 = jnp.where(lanef == i0, -jnp.inf, le)
    v1 = jnp.max(le1, axis=-1, keepdims=True)
    i1 = jnp.min(jnp.where(le1 == v1, lanef, big), axis=-1, keepdims=True)
    e1 = jnp.exp(v1 - v0)
    den = 1.0 + e1
    w0 = p_top * (1.0 / den)
    w1 = p_top * (e1 / den)
    route = jnp.where(lane == 0, w0, jnp.where(lane == 1, w1, jnp.where(
        lane == 2, i0 - EXPERT_LANE0, jnp.where(lane == 3, i1 - EXPERT_LANE0, 0.0))))
    route_ref[...] = route


def _merge_stage(x2d, oa, ob, mg, mw, tm):
    m, d = x2d.shape
    row = lambda w: pl.BlockSpec((tm, w), lambda i: (i, 0))
    full = lambda a: pl.BlockSpec(a.shape, lambda i: (0,) * a.ndim)
    consts = (mw["wpa"], mw["wpb"], mw["wout"], mw["gf"], mw["wr"], mw["br"])
    return pl.pallas_call(
        _merge_kernel,
        grid=(m // tm,),
        in_specs=[row(d), row(oa.shape[1]), row(ob.shape[1]), row(2 * d)] + [full(a) for a in consts],
        out_specs=(row(d), row(d), row(LANES)),
        out_shape=(jax.ShapeDtypeStruct((m, d), F32), jax.ShapeDtypeStruct((m, d), F32),
                   jax.ShapeDtypeStruct((m, LANES), F32)),
        compiler_params=_cparams(("parallel",)),
    )(x2d, oa, ob, mg, *consts)


def _row_copy(src_ref, dst_ref, sem, src_row, dst_row):
    return pltpu.make_async_copy(src_ref.at[pl.ds(src_row, 1), :], dst_ref.at[pl.ds(dst_row, 1), :], sem)


def _expert_kernel(be_ref, tok_ref, h_ref, roww_ref, wg_ref, wu_ref, wd_ref, y_ref, buf_ref, sem, *, blk):
    i = pl.program_id(0)

    def start(r, c):
        _row_copy(h_ref, buf_ref, sem, tok_ref[i * blk + r], r).start()
        return c

    def wait(r, c):
        _row_copy(h_ref, buf_ref, sem, 0, r).wait()
        return c

    lax.fori_loop(0, blk, start, 0)
    lax.fori_loop(0, blk, wait, 0)
    xb = buf_ref[...].astype(BF16)
    gate = _dot(xb, wg_ref[...])
    up = _dot(xb, wu_ref[...])
    mid = (gate * _sigmoid(gate)) * up
    y_ref[...] = _dot(mid.astype(BF16), wd_ref[...]) * roww_ref[...]


def _expert_stage(h2d, blk_expert, row_tok, row_w, ew, blk):
    m, d = h2d.shape
    n_blocks = blk_expert.shape[0]
    de = ew["wg"].shape[2]
    wspec = lambda s: pl.BlockSpec((None,) + s, lambda i, be, tok: (be[i], 0, 0))
    grid_spec = pltpu.PrefetchScalarGridSpec(
        num_scalar_prefetch=2,
        grid=(n_blocks,),
        in_specs=[pl.BlockSpec(memory_space=pl.ANY),
                  pl.BlockSpec((blk, 1), lambda i, be, tok: (i, 0)),
                  wspec((d, de)), wspec((d, de)), wspec((de, d))],
        out_specs=pl.BlockSpec((blk, d), lambda i, be, tok: (i, 0)),
        scratch_shapes=[pltpu.VMEM((blk, d), F32), pltpu.SemaphoreType.DMA(())],
    )
    return pl.pallas_call(
        functools.partial(_expert_kernel, blk=blk),
        grid_spec=grid_spec,
        out_shape=jax.ShapeDtypeStruct((n_blocks * blk, d), F32),
        compiler_params=_cparams(("arbitrary",)),
    )(blk_expert, row_tok, h2d, row_w[:, None], ew["wg"], ew["wu"], ew["wd"])


def _combine_kernel(pos_ref, x1_ref, y_ref, out_ref, buf_ref, sem, *, tm):
    i = pl.program_id(0)

    def start(r, c):
        for k in range(TOP_K):
            _row_copy(y_ref, buf_ref.at[k], sem, pos_ref[(i * tm + r) * TOP_K + k], r).start()
        return c

    def wait(r, c):
        for k in range(TOP_K):
            _row_copy(y_ref, buf_ref.at[k], sem, 0, r).wait()
        return c

    lax.fori_loop(0, tm, start, 0)
    lax.fori_loop(0, tm, wait, 0)
    out_ref[...] = x1_ref[...] + (buf_ref[0] + buf_ref[1])


def _combine_stage(x1, y_sorted, pos, tm):
    m, d = x1.shape
    grid_spec = pltpu.PrefetchScalarGridSpec(
        num_scalar_prefetch=1,
        grid=(m // tm,),
        in_specs=[pl.BlockSpec((tm, d), lambda i, p: (i, 0)), pl.BlockSpec(memory_space=pl.ANY)],
        out_specs=pl.BlockSpec((tm, d), lambda i, p: (i, 0)),
        scratch_shapes=[pltpu.VMEM((TOP_K, tm, d), F32), pltpu.SemaphoreType.DMA(())],
    )
    return pl.pallas_call(
        functools.partial(_combine_kernel, tm=tm),
        grid_spec=grid_spec,
        out_shape=jax.ShapeDtypeStruct((m, d), F32),
        compiler_params=_cparams(("arbitrary",)),
    )(pos, x1, y_sorted)


def _moe_stage(x1, h2d, route, ew, blk, tm):
    m, d = h2d.shape
    n_exp = ew["wg"].shape[0]
    n_assign = m * TOP_K
    weights = route[:, 0:TOP_K]
    expert = route[:, TOP_K:2 * TOP_K].astype(jnp.int32)
    flat_e = expert.reshape(n_assign)
    order = jnp.argsort(flat_e)
    s_e = flat_e[order]
    counts = jnp.zeros((n_exp,), jnp.int32).at[flat_e].add(1)
    padded = (counts + blk - 1) // blk * blk
    pad_end = jnp.cumsum(padded)
    pad_start = pad_end - padded
    start = jnp.cumsum(counts) - counts
    dest = (pad_start[s_e] + jnp.arange(n_assign, dtype=jnp.int32) - start[s_e]).astype(jnp.int32)
    n_blocks = (n_assign + n_exp * (blk - 1) + blk - 1) // blk
    blk_expert = jnp.minimum(jnp.searchsorted(pad_end, jnp.arange(n_blocks, dtype=jnp.int32) * blk, side="right"),
                             n_exp - 1).astype(jnp.int32)
    n_rows = n_blocks * blk
    row_tok = jnp.zeros((n_rows,), jnp.int32).at[dest].set((order // TOP_K).astype(jnp.int32))
    row_w = jnp.zeros((n_rows,), F32).at[dest].set(weights.reshape(n_assign)[order])
    pos = jnp.zeros((n_assign,), jnp.int32).at[order].set(dest)
    y_sorted = _expert_stage(h2d, blk_expert, row_tok, row_w, ew, blk)
    return _combine_stage(x1, y_sorted, pos, tm)


def _rope_tables(pos):
    inv_freq = ROPE_THETA ** (-jnp.arange(ROPE_HALF, dtype=F32) / ROPE_HALF)
    ang = pos.astype(F32)[:, None] * inv_freq[None, :]
    cos, sin = jnp.cos(ang), jnp.sin(ang)
    n = pos.shape[0]
    zeros = lambda k: jnp.zeros((n, k), F32)
    c = jnp.concatenate([cos, cos, jnp.ones((n, HEAD_DIM - ROPE_DIM), F32)], axis=1)
    s1 = jnp.concatenate([-sin, zeros(HEAD_DIM - ROPE_HALF)], axis=1)
    s2 = jnp.concatenate([zeros(ROPE_HALF), sin, zeros(HEAD_DIM - ROPE_DIM)], axis=1)
    return tuple(jnp.tile(a, (1, LANES // HEAD_DIM)) for a in (c, s1, s2))


def _block_diag_ones():
    i = np.arange(LANES)
    return jnp.asarray((i[:, None] // HEAD_DIM == i[None, :] // HEAD_DIM).astype(np.float32), BF16)


def _pack_proj(w_in, norm_mix, q_norm, k_norm, dnw):
    d = w_in.shape[0]
    qc = NSA_HEADS * HEAD_DIM
    kvc = NSA_KV_HEADS * HEAD_DIM
    sizes = (qc,) + (kvc,) * 6 + (NSA_HEADS * 3, 3 * dnw, DN_HEADS, DN_HEADS, dnw, d, d)
    o = np.concatenate([[0], np.cumsum(sizes)])
    seg = lambda i: w_in[:, int(o[i]):int(o[i + 1])]
    wq = seg(0).reshape(d, NSA_HEADS, HEAD_DIM)[:, np.asarray(HEAD_PERM)].reshape(d, qc)
    small = jnp.concatenate([seg(9), seg(10), seg(7), jnp.zeros((d, LANES - 2 * DN_HEADS - 3 * NSA_HEADS), F32)], axis=1)
    w = jnp.concatenate([wq, seg(1), seg(2), seg(3), seg(4), seg(5), seg(6), small, seg(8), seg(11), seg(12), seg(13)],
                        axis=1).astype(BF16)
    offs, ncols = _proj_layout(dnw, d)
    assert ncols == w.shape[1]
    rep = LANES // HEAD_DIM
    return dict(w=w, offs=offs, ncols=ncols, gin=norm_mix[None, :], qg=jnp.tile(q_norm, rep)[None, :],
                kg=jnp.tile(k_norm[0:2], (1, rep)), bd=_block_diag_ones())


def _pack_compress(w_cmp, cmp_pos, k_norm):
    def bdiag(w):
        z = jnp.zeros_like(w)
        return jnp.concatenate([jnp.concatenate([w, z], axis=2), jnp.concatenate([z, w], axis=2)], axis=1)

    s = CMP_STRIDE
    halves = [(0, slice(0, s)), (0, slice(s, 2 * s)), (1, slice(0, s)), (1, slice(s, 2 * s))]
    return dict(w=jnp.stack([bdiag(w_cmp[i, sl]) for i, sl in halves]).astype(BF16),
                p=jnp.stack([jnp.tile(cmp_pos[i, sl], (1, LANES // HEAD_DIM)) for i, sl in halves]),
                kg=jnp.tile(k_norm[2], LANES // HEAD_DIM)[None, :], bd=_block_diag_ones())


def _overlap(nb, ns_pad, n_cmp, n_sel):
    cs = np.arange(nb)[:, None] * CMP_STRIDE
    ss = np.arange(ns_pad)[None, :] * SEL_BLOCK
    ov = (cs <= ss + SEL_BLOCK - 1) & (cs + CMP_BLOCK - 1 >= ss)
    ov &= (np.arange(nb)[:, None] < n_cmp) & (np.arange(ns_pad)[None, :] < n_sel)
    return jnp.asarray(ov.astype(np.float32), BF16)


def _pad_lanes(v, n=LANES):
    return jnp.zeros((1, n), F32).at[0, :v.shape[0]].set(v)


def _round_up(x, n):
    return (x + n - 1) // n * n


def kernel(x_prompt, x_sample, cache_nsa, page_table, state_win, state_dn_conv, state_dn_S, norm_mix, w_in, q_norm, k_norm, cmp_pos, w_cmp, dn_conv_w, dn_A_log, dn_dt_bias, dn_norm, w_proj_a, w_proj_b, w_out, norm_ffn, w_router_g, b_router_g, w_router_e, b_router_e, w_gate, w_up, w_down):
    b, t, d = x_prompt.shape
    bd_, tn, _ = x_sample.shape
    depth = w_in.shape[0]
    assert depth == 1 and tn == 1, "one layer, one new token per sample sequence"
    n_pool, page = cache_nsa.shape[1], cache_nsa.shape[2]
    n_pages = page_table.shape[1]
    past = n_pages * page
    win_rows = state_win.shape[2]
    dnw = dn_conv_w.shape[2] // 3
    assert t % Q_BLOCK == 0 and t >= WINDOW + Q_BLOCK and t % SEL_CHUNK == 0 and past % CMP_STRIDE == 0
    l = 0

    pw = _pack_proj(w_in[l], norm_mix[l], q_norm[l], k_norm[l], dnw)
    cw = _pack_compress(w_cmp[l], cmp_pos[l], k_norm[l])
    dw = dict(cw=dn_conv_w[l], alog=_pad_lanes(dn_A_log[l]), dtb=_pad_lanes(dn_dt_bias[l]), gn=dn_norm[l][None, :])
    perm = np.asarray(HEAD_PERM)
    wr = jnp.zeros((d, LANES), F32).at[:, 0:N_GROUPS].set(w_router_g[l])
    wr = wr.at[:, EXPERT_LANE0:EXPERT_LANE0 + w_router_e.shape[2]].set(w_router_e[l]).astype(BF16)
    br = jnp.zeros((1, LANES), F32).at[0, 0:N_GROUPS].set(b_router_g[l])
    br = br.at[0, EXPERT_LANE0:EXPERT_LANE0 + b_router_e.shape[1]].set(b_router_e[l])
    mw = dict(wpa=w_proj_a[l].reshape(NSA_HEADS, HEAD_DIM, d)[perm].reshape(NSA_HEADS * HEAD_DIM, d).astype(BF16),
              wpb=w_proj_b[l].astype(BF16), wout=w_out[l].astype(BF16), gf=norm_ffn[l][None, :], wr=wr, br=br)
    ew = dict(wg=w_gate[l].astype(BF16), wu=w_up[l].astype(BF16), wd=w_down[l].astype(BF16))

    tm = 256
    tabs_p = _rope_tables(jnp.arange(t))
    q_p, rows_p, win_p, kvb_p, small_p, dnqkv_p, dnz_p, mg_p = _proj_stage(
        x_prompt.reshape(b * t, d), pw, tabs_p, tm, t // tm)
    nb_p = t // CMP_STRIDE
    n_cmp_p = (t - CMP_BLOCK) // CMP_STRIDE + 1
    n_sel_p = -(-t // SEL_BLOCK)
    tabs_cp = _rope_tables(jnp.arange(nb_p) * CMP_STRIDE + CMP_BLOCK - 1)
    kcv_p = _compress_prompt(rows_p.reshape(b, t, 4 * LANES), cw, tabs_cp)
    ovl_p = _overlap(nb_p, _round_up(n_sel_p, LANES), n_cmp_p, n_sel_p)
    oa_p = _attn_prompt(q_p, small_p, kvb_p.reshape(b, t, 4 * LANES), kcv_p, ovl_p, n_cmp_p)
    tb = 256
    ob_p, s_p = _dn_stage(dnqkv_p.reshape(b, t, 3 * dnw), small_p.reshape(b, t, LANES), dnz_p.reshape(b, t, dnw),
                          jnp.zeros((b, SUBLANES, 3 * dnw), F32),
                          jnp.zeros((b, DN_HEADS, DN_HEAD_DIM, DN_HEAD_DIM), F32), dw, DN_CHUNK, tb, t)
    x1_p, h_p, route_p = _merge_stage(x_prompt.reshape(b * t, d), oa_p, ob_p.reshape(b * t, dnw), mg_p, mw, tm)
    y_p = _moe_stage(x1_p, h_p, route_p, ew, 256, tm).reshape(b, t, d)
    kv_p = rows_p.reshape(1, b, t, 4, NSA_KV_HEADS, HEAD_DIM)
    win_all = win_p.reshape(b, t, 2, NSA_KV_HEADS, HEAD_DIM)
    win_out_p = jnp.pad(win_all, ((0, 0), (max(0, win_rows - t), 0), (0, 0), (0, 0), (0, 0)))[:, -win_rows:][None]
    conv_p = dnqkv_p.reshape(b, t, 3 * dnw)[:, t - (DN_CONV - 1):][None]

    tabs_s = _rope_tables(jnp.full((bd_,), past, jnp.int32))
    q_s, rows_s, win_s, _, small_s, dnqkv_s, dnz_s, mg_s = _proj_stage(x_sample.reshape(bd_, d), pw, tabs_s, bd_, 1)
    cache3 = cache_nsa[l].reshape(n_pool, page, 4 * LANES)
    nb_s = past // CMP_STRIDE
    n_cmp_s = (past + tn - CMP_BLOCK) // CMP_STRIDE + 1
    n_sel_s = -(-(past + tn) // SEL_BLOCK)
    assert n_cmp_s == nb_s - 1 and n_cmp_p == nb_p - 1 and (1 << SEL_SHIFT) == SEL_BLOCK
    tabs_cs = _rope_tables(jnp.arange(nb_s) * CMP_STRIDE + CMP_BLOCK - 1)
    kcv_s = _compress_sample(cache3, page_table, cw, tabs_cs)
    ovl_s = _overlap(nb_s, _round_up(n_sel_s, LANES), n_cmp_s, n_sel_s)
    swin3 = state_win[l].reshape(bd_, win_rows, 2 * LANES)
    oa_s = _attn_sample(cache3, page_table, q_s, small_s, rows_s, win_s, kcv_s, swin3, ovl_s, n_cmp_s, n_sel_s)
    pad_t = lambda a: jnp.pad(a[:, None, :], ((0, 0), (0, SUBLANES - tn), (0, 0)))
    conv0_s = jnp.pad(state_dn_conv[l], ((0, 0), (SUBLANES - (DN_CONV - 1), 0), (0, 0)))
    ob_s, s_s = _dn_stage(pad_t(dnqkv_s), pad_t(small_s), pad_t(dnz_s), conv0_s, state_dn_S[l], dw,
                          SUBLANES, SUBLANES, tn)
    x1_s, h_s, route_s = _merge_stage(x_sample.reshape(bd_, d), oa_s, ob_s[:, 0, :], mg_s, mw, bd_)
    y_s = _moe_stage(x1_s, h_s, route_s, ew, 64, bd_).reshape(bd_, tn, d)
    kv_s = rows_s.reshape(1, bd_, tn, 4, NSA_KV_HEADS, HEAD_DIM)
    win_new = win_s.reshape(bd_, tn, 2, NSA_KV_HEADS, HEAD_DIM)
    win_out_s = jnp.concatenate([state_win[l], win_new], axis=1)[:, -win_rows:][None]
    conv_s = jnp.concatenate([state_dn_conv[l], dnqkv_s[:, None, :]], axis=1)[:, -(DN_CONV - 1):][None]

    return (y_p, y_s, kv_p, kv_s, win_out_p, win_out_s, conv_p, conv_s, s_p[None], s_s[None])
```

```python
import functools
import math

import numpy as np
import jax
import jax.numpy as jnp
from jax import lax
from jax.experimental import pallas as pl
from jax.experimental.pallas import tpu as pltpu

F32 = jnp.float32
BF16 = jnp.bfloat16

NSA_HEADS = 8
NSA_KV_HEADS = 2
NSA_GROUP = NSA_HEADS // NSA_KV_HEADS
HEAD_DIM = 64
ROPE_DIM = HEAD_DIM // 4
ROPE_HALF = ROPE_DIM // 2
ROPE_THETA = 500000.0
CMP_BLOCK = 32
CMP_STRIDE = 16
SEL_BLOCK = 64
SEL_SHIFT = 6
SEL_TOPN = 16
WINDOW = 512
FORCE_BONUS = 1000.0
DN_HEADS = 4
DN_HEAD_DIM = 128
DN_CONV = 4
DN_CHUNK = 64
N_GROUPS = 4
EXPERTS_PER_GROUP = 8
TOP_K = 2
EPS = 1e-6

LANES = 128
SUBLANES = 8
VMEM_LIMIT = 56 * 1024 * 1024
NEG = -1e30
Q_BLOCK = 128
SEL_CHUNK = 1024
DN_TB = 256
HEAD_PERM = (0, 4, 1, 5, 2, 6, 3, 7)
SM_A, SM_B, SM_GATE = 0, DN_HEADS, 2 * DN_HEADS
EXPERT_LANE0 = 32


def _cparams(sem):
    return pltpu.CompilerParams(dimension_semantics=sem, vmem_limit_bytes=VMEM_LIMIT)


def _dot(a, b):
    return jnp.dot(a, b, preferred_element_type=F32)


def _dot_nt(a, b):
    return lax.dot_general(a, b, (((1,), (1,)), ((), ())), preferred_element_type=F32)


def _dot_tn(a, b):
    return lax.dot_general(a, b, (((0,), (0,)), ((), ())), preferred_element_type=F32)


def _split2(x):
    hi = x.astype(BF16)
    lo = (x - hi.astype(F32)).astype(BF16)
    return hi, lo


def _split3(x):
    hi = x.astype(BF16)
    r = x - hi.astype(F32)
    mid = r.astype(BF16)
    lo = (r - mid.astype(F32)).astype(BF16)
    return hi, mid, lo


def _dot3s(a, b):
    return _dot(a[0], b[0]) + (_dot(a[0], b[1]) + _dot(a[1], b[0]))


def _sigmoid(x):
    return 1.0 / (1.0 + jnp.exp(-x))


def _seg_sumsq(x, bd):
    hi, lo = _split2(x * x)
    return _dot(hi, bd) + _dot(lo, bd)


def _rope128(x, c, s1, s2):
    return x * c + pltpu.roll(x, LANES - ROPE_HALF, 1) * s1 + pltpu.roll(x, ROPE_HALF, 1) * s2


def _head_norm_rope(v, gain, bd, c, s1, s2):
    ss = _seg_sumsq(v, bd) * (1.0 / HEAD_DIM)
    return _rope128(v * lax.rsqrt(ss + EPS) * gain, c, s1, s2)


def _masked_softmax(s, valid):
    s = jnp.where(valid, s, NEG)
    m = jnp.max(s, axis=-1, keepdims=True)
    e = jnp.where(valid, jnp.exp(s - m), 0.0)
    l = jnp.sum(e, axis=-1, keepdims=True)
    return e * (1.0 / jnp.maximum(l, 1e-30))


def _topk_mask(score, k):
    lane = lax.broadcasted_iota(jnp.int32, score.shape, 1).astype(F32)
    big = float(score.shape[-1])

    def body(_, carry):
        sc, sel = carry
        m = jnp.max(sc, axis=-1, keepdims=True)
        idx = jnp.min(jnp.where(sc == m, lane, big), axis=-1, keepdims=True)
        hit = lane == idx
        sel = jnp.where(hit, jnp.maximum(sel, jnp.where(m > -jnp.inf, 1.0, 0.0)), sel)
        sc = jnp.where(hit, -jnp.inf, sc)
        return sc, sel

    _, sel = lax.fori_loop(0, k, body, (score, jnp.zeros(score.shape, F32)))
    return sel


def _proj_layout(dnw, d):
    sizes = dict(q=NSA_HEADS * HEAD_DIM, kvc=2 * LANES, ks=LANES, vs=LANES, kw=LANES, vw=LANES,
                 small=LANES, dnqkv=3 * dnw, dnz=dnw, mg=2 * d)
    offs, o = {}, 0
    for name, n in sizes.items():
        offs[name] = (o, o + n)
        o += n
    return offs, o


def _proj_kernel(x_ref, gin_ref, w_ref, c_ref, s1_ref, s2_ref, qg_ref, kg_ref, bd_ref,
                 q_out, rows_out, win_out, kvb_out, small_out, dnqkv_out, dnz_out, mg_out, *, offs):
    x = x_ref[...]
    ms = jnp.mean(x * x, axis=-1, keepdims=True)
    hb = ((x * lax.rsqrt(ms + EPS)) * gin_ref[...]).astype(BF16)

    def mm(name, lo=0, hi=None):
        a, b = offs[name]
        hi = b - a if hi is None else hi
        return _dot(hb, w_ref[:, a + lo:a + hi])

    c, s1, s2 = c_ref[...], s1_ref[...], s2_ref[...]
    bd = bd_ref[...]
    for j in range(NSA_HEADS * HEAD_DIM // LANES):
        qj = _head_norm_rope(mm("q", LANES * j, LANES * (j + 1)), qg_ref[...], bd, c, s1, s2)
        q_out[:, LANES * j:LANES * (j + 1)] = (qj * (HEAD_DIM ** -0.5)).astype(BF16)
    rows_out[:, 0:2 * LANES] = mm("kvc")
    ks = _head_norm_rope(mm("ks"), kg_ref[0:1, :], bd, c, s1, s2)
    vs = mm("vs")
    rows_out[:, 2 * LANES:3 * LANES] = ks
    rows_out[:, 3 * LANES:4 * LANES] = vs
    kw = _head_norm_rope(mm("kw"), kg_ref[1:2, :], bd, c, s1, s2)
    vw = mm("vw")
    win_out[:, 0:LANES] = kw
    win_out[:, LANES:2 * LANES] = vw
    kvb_out[:, 0:LANES] = ks.astype(BF16)
    kvb_out[:, LANES:2 * LANES] = vs.astype(BF16)
    kvb_out[:, 2 * LANES:3 * LANES] = kw.astype(BF16)
    kvb_out[:, 3 * LANES:4 * LANES] = vw.astype(BF16)
    small_out[...] = mm("small")
    dnqkv_out[...] = mm("dnqkv")
    dnz_out[...] = mm("dnz")
    mg_out[...] = _sigmoid(mm("mg"))


def _proj_stage(x2d, pw, tabs, tm, n_tab_blocks):
    m, d = x2d.shape
    offs, ncols = pw["offs"], pw["ncols"]
    dnw = offs["dnz"][1] - offs["dnz"][0]
    row = lambda w: pl.BlockSpec((tm, w), lambda i: (i, 0))
    full = lambda a: pl.BlockSpec(a.shape, lambda i: (0,) * a.ndim)
    tab = pl.BlockSpec((tm, LANES), lambda i: (i % n_tab_blocks, 0))
    out_shape = (
        jax.ShapeDtypeStruct((m, 4 * LANES), BF16),
        jax.ShapeDtypeStruct((m, 4 * LANES), F32),
        jax.ShapeDtypeStruct((m, 2 * LANES), F32),
        jax.ShapeDtypeStruct((m, 4 * LANES), BF16),
        jax.ShapeDtypeStruct((m, LANES), F32),
        jax.ShapeDtypeStruct((m, 3 * dnw), F32),
        jax.ShapeDtypeStruct((m, dnw), F32),
        jax.ShapeDtypeStruct((m, 2 * d), F32),
    )
    return pl.pallas_call(
        functools.partial(_proj_kernel, offs=offs),
        grid=(m // tm,),
        in_specs=[row(d), full(pw["gin"]), full(pw["w"]), tab, tab, tab,
                  full(pw["qg"]), full(pw["kg"]), full(pw["bd"])],
        out_specs=tuple(row(s.shape[1]) for s in out_shape),
        out_shape=out_shape,
        compiler_params=_cparams(("parallel",)),
    )(x2d, pw["gin"], pw["w"], tabs[0], tabs[1], tabs[2], pw["qg"], pw["kg"], pw["bd"])


def _compress_body(xk_ref, xv_ref, w_ref, p_ref, kg_ref, bd_ref, c_ref, s1_ref, s2_ref, out_ref):
    nb = xk_ref.shape[0] // CMP_STRIDE
    acc = [jnp.zeros((nb, LANES), F32) for _ in range(4)]
    for l in range(CMP_STRIDE):
        xs = (xk_ref[pl.ds(l, nb, stride=CMP_STRIDE), :], xv_ref[pl.ds(l, nb, stride=CMP_STRIDE), :])
        for j in range(4):
            acc[j] = acc[j] + _dot((xs[j // 2] + p_ref[j, l:l + 1, :]).astype(BF16), w_ref[j, l])
    kraw = acc[0] + pltpu.roll(acc[1], nb - 1, 0)
    vraw = acc[2] + pltpu.roll(acc[3], nb - 1, 0)
    kc = _head_norm_rope(kraw, kg_ref[...], bd_ref[...], c_ref[...], s1_ref[...], s2_ref[...])
    out_ref[:, 0:LANES] = kc.astype(BF16)
    out_ref[:, LANES:2 * LANES] = vraw.astype(BF16)


def _compress_prompt_kernel(xk_ref, xv_ref, *rest):
    _compress_body(xk_ref, xv_ref, *rest)


def _compress_prompt(rows3, cw, tabs):
    b, t, _ = rows3.shape
    nb = t // CMP_STRIDE
    full = lambda a: pl.BlockSpec(a.shape, lambda i: (0,) * a.ndim)
    consts = (cw["w"], cw["p"], cw["kg"], cw["bd"]) + tuple(tabs)
    return pl.pallas_call(
        _compress_prompt_kernel,
        grid=(b,),
        in_specs=[pl.BlockSpec((None, t, LANES), lambda i: (i, 0, 0)),
                  pl.BlockSpec((None, t, LANES), lambda i: (i, 0, 1))] + [full(a) for a in consts],
        out_specs=pl.BlockSpec((None, nb, 2 * LANES), lambda i: (i, 0, 0)),
        out_shape=jax.ShapeDtypeStruct((b, nb, 2 * LANES), BF16),
        compiler_params=_cparams(("parallel",)),
    )(rows3, rows3, *consts)


def _wait_all(buf_view, sem):
    pltpu.make_async_copy(buf_view, buf_view, sem).wait()


def _start_pages(pt_ref, cache_ref, buf_ref, sem, b, slot, n_pages, page, lane0):
    def start(p, c):
        for j in range(2):
            pltpu.make_async_copy(
                cache_ref.at[pt_ref[b, p], :, pl.ds(lane0 + j * LANES, LANES)],
                buf_ref.at[slot, j, pl.ds(pl.multiple_of(p * page, page), page), :], sem.at[slot]).start()
        return c

    lax.fori_loop(0, n_pages, start, 0)


def _prefetch_pages(pt_ref, cache_ref, buf_ref, sem, n_pages, page, lane0):
    b = pl.program_id(0)
    slot = b % 2

    @pl.when(b == 0)
    def _():
        _start_pages(pt_ref, cache_ref, buf_ref, sem, 0, 0, n_pages, page, lane0)

    @pl.when(b + 1 < pl.num_programs(0))
    def _():
        _start_pages(pt_ref, cache_ref, buf_ref, sem, b + 1, 1 - slot, n_pages, page, lane0)

    return slot


def _compress_sample_kernel(pt_ref, cache_ref, *rest, n_pages, page):
    *consts, out_ref, buf_ref, sem = rest
    slot = _prefetch_pages(pt_ref, cache_ref, buf_ref, sem, n_pages, page, 0)
    _wait_all(buf_ref.at[slot], sem.at[slot])
    _compress_body(buf_ref.at[slot, 0], buf_ref.at[slot, 1], *consts, out_ref)


def _compress_sample(cache3, page_table, cw, tabs):
    bd_, n_pages = page_table.shape
    page = cache3.shape[1]
    past = n_pages * page
    nb = past // CMP_STRIDE
    full = lambda a: pl.BlockSpec(a.shape, lambda i, pt: (0,) * a.ndim)
    consts = (cw["w"], cw["p"], cw["kg"], cw["bd"]) + tuple(tabs)
    grid_spec = pltpu.PrefetchScalarGridSpec(
        num_scalar_prefetch=1,
        grid=(bd_,),
        in_specs=[pl.BlockSpec(memory_space=pl.ANY)] + [full(a) for a in consts],
        out_specs=pl.BlockSpec((None, nb, 2 * LANES), lambda i, pt: (i, 0, 0)),
        scratch_shapes=[pltpu.VMEM((2, 2, past, LANES), F32), pltpu.SemaphoreType.DMA((2,))],
    )
    return pl.pallas_call(
        functools.partial(_compress_sample_kernel, n_pages=n_pages, page=page),
        grid_spec=grid_spec,
        out_shape=jax.ShapeDtypeStruct((bd_, nb, 2 * LANES), BF16),
        compiler_params=_cparams(("arbitrary",)),
    )(page_table, cache3, *consts)


def _stack_heads(q, lo_half):
    zero = jnp.zeros((), q.dtype)
    parts = []
    for g in range(NSA_KV_HEADS):
        for r in range(NSA_GROUP):
            col = q[:, LANES * r:LANES * (r + 1)]
            parts.append(jnp.where(lo_half if g == 0 else jnp.logical_not(lo_half), col, zero))
    return jnp.concatenate(parts, axis=0)


def _sel_scores(imp, pos_q, blk):
    cur = pos_q >> SEL_SHIFT
    readable = blk * SEL_BLOCK <= pos_q
    forced = jnp.logical_or(blk == 0, jnp.logical_or(blk == cur, blk == cur - 1))
    return jnp.where(readable, imp + jnp.where(forced, FORCE_BONUS, 0.0), -jnp.inf)


def _attn_prompt_kernel(q_ref, small_ref, kvb_ref, kcv_ref, ovl_ref, o_ref, *, n_cmp):
    qb = Q_BLOCK
    m8 = NSA_HEADS * qb
    i = pl.program_id(1)
    q0 = i * qb
    lane = lax.broadcasted_iota(jnp.int32, (qb, LANES), 1)
    lo_half = lane < HEAD_DIM
    q8 = _stack_heads(q_ref[...], lo_half)

    def qpos(n):
        return q0 + (lax.broadcasted_iota(jnp.int32, (m8, n), 0) & (qb - 1))

    def kidx(n):
        return lax.broadcasted_iota(jnp.int32, (m8, n), 1)

    nb = kcv_ref.shape[0]
    s = _dot_nt(q8, kcv_ref[:, 0:LANES])
    cidx = kidx(nb)
    valid = jnp.logical_and(cidx < n_cmp, cidx * CMP_STRIDE + (CMP_BLOCK - 1) <= qpos(nb))
    p = _masked_softmax(s, valid).astype(BF16)
    o_c = _dot(p, kcv_ref[:, LANES:2 * LANES])
    imp8 = _dot(p, ovl_ref[...])

    ns = ovl_ref.shape[1]
    imps = []
    for g in range(NSA_KV_HEADS):
        acc = imp8[(g * NSA_GROUP) * qb:(g * NSA_GROUP + 1) * qb]
        for r in range(1, NSA_GROUP):
            acc = acc + imp8[(g * NSA_GROUP + r) * qb:(g * NSA_GROUP + r + 1) * qb]
        imps.append(acc)
    imp = jnp.concatenate(imps, axis=0)
    pos2 = q0 + (lax.broadcasted_iota(jnp.int32, (2 * qb, ns), 0) & (qb - 1))
    blk2 = lax.broadcasted_iota(jnp.int32, (2 * qb, ns), 1)
    sel = _topk_mask(_sel_scores(imp, pos2, blk2), SEL_TOPN)
    selb = jnp.where(sel > 0.0, 0.0, NEG).astype(BF16)
    sb8 = jnp.concatenate([selb[0:qb]] * NSA_GROUP + [selb[qb:2 * qb]] * NSA_GROUP, axis=0)
    qaug = jnp.concatenate([q8, sb8], axis=1)

    ck = SEL_CHUNK
    qpos_k = qpos(ck)
    kk = lax.broadcasted_iota(jnp.int32, (ck, ns), 0)
    ss = lax.broadcasted_iota(jnp.int32, (ck, ns), 1)

    def chunk(c, carry, causal):
        m, l, acc = carry
        k0 = pl.multiple_of(c * ck, ck)
        kmat = kvb_ref[pl.ds(k0, ck), 0:LANES]
        vmat = kvb_ref[pl.ds(k0, ck), LANES:2 * LANES]
        e = jnp.where(((k0 + kk) >> SEL_SHIFT) == ss, 1.0, 0.0).astype(BF16)
        sc = _dot_nt(qaug, jnp.concatenate([kmat, e], axis=1))
        if causal:
            sc = jnp.where(k0 + kidx(ck) <= qpos_k, sc, NEG)
        m_new = jnp.maximum(m, jnp.max(sc, axis=-1, keepdims=True))
        alpha = jnp.exp(m - m_new)
        pe = jnp.exp(sc - m_new)
        l = alpha * l + jnp.sum(pe, axis=-1, keepdims=True)
        acc = alpha * acc + _dot(pe.astype(BF16), vmat)
        return m_new, l, acc

    n_chunks = (q0 + qb + ck - 1) // ck
    init = (jnp.full((m8, 1), NEG, F32), jnp.zeros((m8, 1), F32), jnp.zeros((m8, LANES), F32))
    carry = lax.fori_loop(0, n_chunks - 1, functools.partial(chunk, causal=False), init)
    _, l_s, acc_s = chunk(n_chunks - 1, carry, True)
    o_s = acc_s * (1.0 / l_s)

    wk = WINDOW + qb
    ws = pl.multiple_of(jnp.maximum(q0 - WINDOW, 0), qb)
    s = _dot_nt(q8, kvb_ref[pl.ds(ws, wk), 2 * LANES:3 * LANES])
    dpos = qpos(wk) - (ws + kidx(wk))
    p = _masked_softmax(s, jnp.logical_and(dpos >= 0, dpos < WINDOW)).astype(BF16)
    o_w = _dot(p, kvb_ref[pl.ds(ws, wk), 3 * LANES:4 * LANES])

    sig = _sigmoid(small_ref[...])
    for r in range(NSA_GROUP):
        outs = []
        for g in range(NSA_KV_HEADS):
            h = g * NSA_GROUP + r
            rows = slice(h * qb, (h + 1) * qb)
            gl = SM_GATE + 3 * h
            outs.append(sig[:, gl:gl + 1] * o_c[rows] + sig[:, gl + 1:gl + 2] * o_s[rows]
                        + sig[:, gl + 2:gl + 3] * o_w[rows])
        o_ref[:, LANES * r:LANES * (r + 1)] = jnp.where(lo_half, outs[0], outs[1]).astype(BF16)


def _attn_prompt(q2d, small2d, kvb3, kcv3, ovl, n_cmp):
    b, t, _ = kvb3.shape
    nq = t // Q_BLOCK
    nb = kcv3.shape[1]
    return pl.pallas_call(
        functools.partial(_attn_prompt_kernel, n_cmp=n_cmp),
        grid=(b, nq),
        in_specs=[pl.BlockSpec((Q_BLOCK, 4 * LANES), lambda bi, i: (bi * nq + i, 0)),
                  pl.BlockSpec((Q_BLOCK, LANES), lambda bi, i: (bi * nq + i, 0)),
                  pl.BlockSpec((None, t, 4 * LANES), lambda bi, i: (bi, 0, 0)),
                  pl.BlockSpec((None, nb, 2 * LANES), lambda bi, i: (bi, 0, 0)),
                  pl.BlockSpec(ovl.shape, lambda bi, i: (0, 0))],
        out_specs=pl.BlockSpec((Q_BLOCK, 4 * LANES), lambda bi, i: (bi * nq + i, 0)),
        out_shape=jax.ShapeDtypeStruct((b * t, 4 * LANES), BF16),
        compiler_params=_cparams(("parallel", "arbitrary")),
    )(q2d, small2d, kvb3, kcv3, ovl)


def _attn_sample_kernel(pt_ref, cache_ref, q_ref, small_ref, rows_ref, win_ref, kcv_ref, swin_ref, ovl_ref,
                        o_ref, buf_ref, sem, *, n_pages, page, n_cmp, n_sel, key_chunk):
    past = n_pages * page
    slot = _prefetch_pages(pt_ref, cache_ref, buf_ref, sem, n_pages, page, 2 * LANES)
    lane1 = lax.broadcasted_iota(jnp.int32, (1, LANES), 1)
    lo1 = lane1 < HEAD_DIM
    q8f = _stack_heads(q_ref[...].astype(F32), lo1)
    q8 = q8f.astype(BF16)
    nh = NSA_HEADS

    nb = kcv_ref.shape[0]
    s = _dot_nt(q8, kcv_ref[:, 0:LANES])
    p = _masked_softmax(s, lax.broadcasted_iota(jnp.int32, (nh, nb), 1) < n_cmp).astype(BF16)
    o_c = _dot(p, kcv_ref[:, LANES:2 * LANES])
    imp8 = _dot(p, ovl_ref[...])
    nsp = ovl_ref.shape[1]
    blk = lax.broadcasted_iota(jnp.int32, (1, nsp), 1)
    selbs = []
    for g in range(NSA_KV_HEADS):
        imp = jnp.sum(imp8[g * NSA_GROUP:(g + 1) * NSA_GROUP], axis=0, keepdims=True)
        score = jnp.where(blk < n_sel, _sel_scores(imp, past, blk), -jnp.inf)
        selb = jnp.where(_topk_mask(score, SEL_TOPN) > 0.0, 0.0, NEG)
        selbs += [selb] * NSA_GROUP
    selb8 = jnp.concatenate(selbs, axis=0)
    selb8_b = selb8.astype(BF16)

    _wait_all(buf_ref.at[slot], sem.at[slot])
    ks_new = rows_ref[:, 2 * LANES:3 * LANES].astype(BF16).astype(F32)
    vs_new = rows_ref[:, 3 * LANES:4 * LANES].astype(BF16).astype(F32)
    s_new = jnp.sum(q8f * ks_new, axis=-1, keepdims=True) + selb8[:, n_sel - 1:n_sel]
    ck = key_chunk
    n_ck = past // ck
    ss = lax.broadcasted_iota(jnp.int32, (nsp, ck), 0)
    kk = lax.broadcasted_iota(jnp.int32, (nsp, ck), 1)
    scores = []
    for c in range(n_ck):
        kmat = buf_ref[slot, 0, c * ck:(c + 1) * ck, :].astype(BF16)
        e = jnp.where(((c * ck + kk) >> SEL_SHIFT) == ss, 1.0, 0.0).astype(BF16)
        scores.append(_dot_nt(q8, kmat) + _dot(selb8_b, e))
    m = s_new
    for sc in scores:
        m = jnp.maximum(m, jnp.max(sc, axis=-1, keepdims=True))
    p_new = jnp.exp(s_new - m)
    l = p_new
    acc = p_new * vs_new
    for c, sc in enumerate(scores):
        pe = jnp.exp(sc - m)
        l = l + jnp.sum(pe, axis=-1, keepdims=True)
        acc = acc + _dot(pe.astype(BF16), buf_ref[slot, 1, c * ck:(c + 1) * ck, :].astype(BF16))
    o_s = acc * (1.0 / l)

    wr = swin_ref.shape[0]
    s = _dot_nt(q8, swin_ref[:, 0:LANES].astype(BF16))
    j = lax.broadcasted_iota(jnp.int32, (nh, wr), 1)
    valid = wr - j < WINDOW
    kw_new = win_ref[:, 0:LANES].astype(BF16).astype(F32)
    vw_new = win_ref[:, LANES:2 * LANES].astype(BF16).astype(F32)
    s_new = jnp.sum(q8f * kw_new, axis=-1, keepdims=True)
    s = jnp.where(valid, s, NEG)
    m = jnp.maximum(s_new, jnp.max(s, axis=-1, keepdims=True))
    pe = jnp.where(valid, jnp.exp(s - m), 0.0)
    p_new = jnp.exp(s_new - m)
    l = p_new + jnp.sum(pe, axis=-1, keepdims=True)
    o_w = (p_new * vw_new + _dot(pe.astype(BF16), swin_ref[:, LANES:2 * LANES].astype(BF16))) * (1.0 / l)

    sig = _sigmoid(small_ref[...])
    for r in range(NSA_GROUP):
        outs = []
        for g in range(NSA_KV_HEADS):
            h = g * NSA_GROUP + r
            gl = SM_GATE + 3 * h
            outs.append(sig[:, gl:gl + 1] * o_c[h:h + 1] + sig[:, gl + 1:gl + 2] * o_s[h:h + 1]
                        + sig[:, gl + 2:gl + 3] * o_w[h:h + 1])
        o_ref[:, LANES * r:LANES * (r + 1)] = jnp.where(lo1, outs[0], outs[1]).astype(BF16)


def _attn_sample(cache3, page_table, q2d, small2d, rows2d, win2d, kcv3, swin3, ovl, n_cmp, n_sel):
    bd_, n_pages = page_table.shape
    page = cache3.shape[1]
    past = n_pages * page
    key_chunk = math.gcd(past, 2048)
    per_seq = lambda a: pl.BlockSpec((None,) + a.shape[1:], lambda i, pt: (i,) + (0,) * (a.ndim - 1))
    q3, small3, rows3, win3 = (a[:, None, :] for a in (q2d, small2d, rows2d, win2d))
    grid_spec = pltpu.PrefetchScalarGridSpec(
        num_scalar_prefetch=1,
        grid=(bd_,),
        in_specs=[pl.BlockSpec(memory_space=pl.ANY), per_seq(q3), per_seq(small3), per_seq(rows3), per_seq(win3),
                  per_seq(kcv3), per_seq(swin3), pl.BlockSpec(ovl.shape, lambda i, pt: (0, 0))],
        out_specs=pl.BlockSpec((None, 1, 4 * LANES), lambda i, pt: (i, 0, 0)),
        scratch_shapes=[pltpu.VMEM((2, 2, past, LANES), F32), pltpu.SemaphoreType.DMA((2,))],
    )
    out = pl.pallas_call(
        functools.partial(_attn_sample_kernel, n_pages=n_pages, page=page, n_cmp=n_cmp, n_sel=n_sel,
                          key_chunk=key_chunk),
        grid_spec=grid_spec,
        out_shape=jax.ShapeDtypeStruct((bd_, 1, 4 * LANES), BF16),
        compiler_params=_cparams(("arbitrary",)),
    )(page_table, cache3, q3, small3, rows3, win3, kcv3, swin3, ovl)
    return out[:, 0, :]


def _dn_kernel(x_ref, small_ref, z_ref, cw_ref, c0_ref, s0_ref, alog_ref, dtb_ref, gn_ref,
               o_ref, sout_ref, s_scr, xs_scr, *, chunk, tb, t_valid):
    j = pl.program_id(1)
    dnw = z_ref.shape[1]
    hd = DN_HEAD_DIM
    hist = SUBLANES

    @pl.when(j == 0)
    def _():
        s_scr[...] = s0_ref[...]
        xs_scr[0:hist, :] = c0_ref[...]

    xs_scr[hist:hist + tb, :] = x_ref[...]
    conv = xs_scr[pl.ds(hist - (DN_CONV - 1), tb), :] * cw_ref[0:1, :]
    for jj in range(1, DN_CONV):
        conv = conv + xs_scr[pl.ds(hist - (DN_CONV - 1) + jj, tb), :] * cw_ref[jj:jj + 1, :]
    xs_scr[0:hist, :] = xs_scr[tb:tb + hist, :]
    act = conv * _sigmoid(conv)

    small = small_ref[...]
    tpos = j * tb + lax.broadcasted_iota(jnp.int32, (tb, LANES), 0)
    live = tpos < t_valid
    xg = small + dtb_ref[...]
    softplus = jnp.maximum(xg, 0.0) + jnp.log(1.0 + jnp.exp(-jnp.abs(xg)))
    g_all = jnp.where(live, -jnp.exp(alog_ref[...]) * softplus, 0.0)
    beta_all = jnp.where(live, _sigmoid(small), 0.0)

    ri = lax.broadcasted_iota(jnp.int32, (chunk, chunk), 0)
    ci = lax.broadcasted_iota(jnp.int32, (chunk, chunk), 1)
    incl = ri >= ci
    strict = ri > ci
    ltri = jnp.where(incl, 1.0, 0.0).astype(BF16)
    eye = jnp.where(ri == ci, 1.0, 0.0)
    lane_c = lax.broadcasted_iota(jnp.int32, (chunk, LANES), 1)
    n_sq = max(1, int(math.ceil(math.log2(chunk))))

    n_chunks = tb // chunk
    items = [(c, h) for c in range(n_chunks) for h in range(DN_HEADS)]

    gc_alls, gc3s = [], []
    for c in range(n_chunks):
        g3 = _split3(g_all[c * chunk:(c + 1) * chunk])
        gc_all = _dot(ltri, g3[0]) + (_dot(ltri, g3[1]) + _dot(ltri, g3[2]))
        gc_alls.append(gc_all)
        gc3s.append(_split3(gc_all))

    prep = []
    for c, h in items:
        rs = slice(c * chunk, (c + 1) * chunk)
        q = act[rs, h * hd:(h + 1) * hd]
        k = act[rs, dnw + h * hd:dnw + (h + 1) * hd]
        v = act[rs, 2 * dnw + h * hd:2 * dnw + (h + 1) * hd]
        q = q * lax.rsqrt(jnp.sum(q * q, axis=-1, keepdims=True) + EPS) * (hd ** -0.5)
        k = k * lax.rsqrt(jnp.sum(k * k, axis=-1, keepdims=True) + EPS)
        beta = beta_all[rs, SM_B + h:SM_B + h + 1]
        gc = gc_alls[c][:, SM_A + h:SM_A + h + 1]
        pick = jnp.where(lane_c == SM_A + h, 1.0, 0.0).astype(BF16)
        g3 = gc3s[c]
        gc_row = _dot_nt(pick, g3[0]) + (_dot_nt(pick, g3[1]) + _dot_nt(pick, g3[2]))
        dmask = jnp.where(incl, jnp.exp(jnp.where(incl, gc - gc_row, 0.0)), 0.0)
        kb = k * beta
        kbf = k.astype(BF16)
        a_strict = jnp.where(strict, _dot_nt(kb.astype(BF16), kbf) * dmask, 0.0)
        egc = jnp.exp(gc)
        g_last = gc[chunk - 1:chunk, :]
        prep.append(dict(
            vb=(v * beta).astype(BF16), kbg=(kb * egc).astype(BF16), qg=(q * egc).astype(BF16),
            qk=(_dot_nt(q.astype(BF16), kbf) * dmask).astype(BF16),
            kd=(k * jnp.exp(g_last - gc)).astype(BF16), decay=jnp.exp(g_last), n=-a_strict))

    tinv = [eye + p["n"] for p in prep]
    nsp = [_split2(p["n"]) for p in prep]
    for _ in range(n_sq - 1):
        nsp = [_split2(_dot3s(n, n)) for n in nsp]
        tinv = [t + _dot3s(_split2(t), n) for t, n in zip(tinv, nsp)]
    us, ws = [], []
    for p, t in zip(prep, tinv):
        tb16 = t.astype(BF16)
        us.append(_dot(tb16, p["vb"]))
        ws.append(_dot(tb16, p["kbg"]).astype(BF16))

    state = [s_scr[h] for h in range(DN_HEADS)]
    for i, (c, h) in enumerate(items):
        p = prep[i]
        rs = slice(c * chunk, (c + 1) * chunk)
        s_b = state[h].astype(BF16)
        v_new = (us[i] - _dot(ws[i], s_b)).astype(BF16)
        o = _dot(p["qg"], s_b) + _dot(p["qk"], v_new)
        state[h] = state[h] * p["decay"] + _dot_tn(p["kd"], v_new)
        on = o * lax.rsqrt(jnp.mean(o * o, axis=-1, keepdims=True) + EPS) * gn_ref[...]
        z = z_ref[rs, h * hd:(h + 1) * hd]
        o_ref[rs, h * hd:(h + 1) * hd] = (on * (z * _sigmoid(z))).astype(BF16)
    for h in range(DN_HEADS):
        s_scr[h] = state[h]

    @pl.when(j == pl.num_programs(1) - 1)
    def _():
        sout_ref[...] = s_scr[...]


def _dn_stage(x3, small3, z3, conv0, s0, dw, chunk, tb, t_valid):
    b, tpad, w3 = x3.shape
    dnw = w3 // 3
    nblk = tpad // tb
    full = lambda a: pl.BlockSpec(a.shape, lambda bi, j: (0,) * a.ndim)
    tok = lambda w: pl.BlockSpec((None, tb, w), lambda bi, j: (bi, j, 0))
    o, s_out = pl.pallas_call(
        functools.partial(_dn_kernel, chunk=chunk, tb=tb, t_valid=t_valid),
        grid=(b, nblk),
        in_specs=[tok(w3), tok(LANES), tok(dnw), full(dw["cw"]),
                  pl.BlockSpec((None, SUBLANES, w3), lambda bi, j: (bi, 0, 0)),
                  pl.BlockSpec((None, DN_HEADS, DN_HEAD_DIM, DN_HEAD_DIM), lambda bi, j: (bi, 0, 0, 0)),
                  full(dw["alog"]), full(dw["dtb"]), full(dw["gn"])],
        out_specs=(tok(dnw),
                   pl.BlockSpec((None, DN_HEADS, DN_HEAD_DIM, DN_HEAD_DIM), lambda bi, j: (bi, 0, 0, 0))),
        out_shape=(jax.ShapeDtypeStruct((b, tpad, dnw), BF16),
                   jax.ShapeDtypeStruct((b, DN_HEADS, DN_HEAD_DIM, DN_HEAD_DIM), F32)),
        scratch_shapes=[pltpu.VMEM((DN_HEADS, DN_HEAD_DIM, DN_HEAD_DIM), F32),
                        pltpu.VMEM((tb + SUBLANES, w3), F32)],
        compiler_params=_cparams(("parallel", "arbitrary")),
    )(x3, small3, z3, dw["cw"], conv0, s0, dw["alog"], dw["dtb"], dw["gn"])
    return o, s_out


def _merge_kernel(x_ref, oa_ref, ob_ref, mg_ref, wpa_ref, wpb_ref, wout_ref, gf_ref, wr_ref, br_ref,
                  x1_ref, h_ref, route_ref):
    d = x_ref.shape[1]
    mixed = mg_ref[:, 0:d] * _dot(oa_ref[...], wpa_ref[...]) + mg_ref[:, d:2 * d] * _dot(ob_ref[...], wpb_ref[...])
    x1 = x_ref[...] + _dot(mixed.astype(BF16), wout_ref[...])
    x1_ref[...] = x1
    h = (x1 * lax.rsqrt(jnp.mean(x1 * x1, axis=-1, keepdims=True) + EPS)) * gf_ref[...]
    h_ref[...] = h
    logits = _dot(h.astype(BF16), wr_ref[...]) + br_ref[...]
    lane = lax.broadcasted_iota(jnp.int32, logits.shape, 1)
    lanef = lane.astype(F32)
    big = float(LANES)
    is_g = lane < N_GROUPS
    lg = jnp.where(is_g, logits, NEG)
    eg = jnp.where(is_g, jnp.exp(lg - jnp.max(lg, axis=-1, keepdims=True)), 0.0)
    pg = eg / jnp.sum(eg, axis=-1, keepdims=True)
    p_top = jnp.max(pg, axis=-1, keepdims=True)
    grp = jnp.min(jnp.where(jnp.logical_and(is_g, pg == p_top), lanef, big), axis=-1, keepdims=True)
    e_lo = EXPERT_LANE0 + grp * EXPERTS_PER_GROUP
    in_grp = jnp.logical_and(lanef >= e_lo, lanef < e_lo + EXPERTS_PER_GROUP)
    le = jnp.where(in_grp, logits, -jnp.inf)
    v0 = jnp.max(le, axis=-1, keepdims=True)
    i0 = jnp.min(jnp.where(le == v0, lanef, big), axis=-1, keepdims=True)
    le1 = jnp.where(lanef == i0, -jnp.inf, le)
    v1 = jnp.max(le1, axis=-1, keepdims=True)
    i1 = jnp.min(jnp.where(le1 == v1, lanef, big), axis=-1, keepdims=True)
    e1 = jnp.exp(v1 - v0)
    den = 1.0 + e1
    w0 = p_top * (1.0 / den)
    w1 = p_top * (e1 / den)
    route = jnp.where(lane == 0, w0, jnp.where(lane == 1, w1, jnp.where(
        lane == 2, i0 - EXPERT_LANE0, jnp.where(lane == 3, i1 - EXPERT_LANE0, 0.0))))
    route_ref[...] = route


def _merge_stage(x2d, oa, ob, mg, mw, tm):
    m, d = x2d.shape
    row = lambda w: pl.BlockSpec((tm, w), lambda i: (i, 0))
    full = lambda a: pl.BlockSpec(a.shape, lambda i: (0,) * a.ndim)
    consts = (mw["wpa"], mw["wpb"], mw["wout"], mw["gf"], mw["wr"], mw["br"])
    return pl.pallas_call(
        _merge_kernel,
        grid=(m // tm,),
        in_specs=[row(d), row(oa.shape[1]), row(ob.shape[1]), row(2 * d)] + [full(a) for a in consts],
        out_specs=(row(d), row(d), row(LANES)),
        out_shape=(jax.ShapeDtypeStruct((m, d), F32), jax.ShapeDtypeStruct((m, d), F32),
                   jax.ShapeDtypeStruct((m, LANES), F32)),
        compiler_params=_cparams(("parallel",)),
    )(x2d, oa, ob, mg, *consts)


def _row_copy(src_ref, dst_ref, sem, src_row, dst_row):
    return pltpu.make_async_copy(src_ref.at[pl.ds(src_row, 1), :], dst_ref.at[pl.ds(dst_row, 1), :], sem)


def _expert_kernel(be_ref, tok_ref, nblk_ref, h_ref, roww_ref, wg_ref, wu_ref, wd_ref, y_ref, buf_ref, sem, *, blk):
    i = pl.program_id(0)
    n_used = nblk_ref[0]
    slot = i % 2

    def gather(b, s):
        def start(r, c):
            _row_copy(h_ref, buf_ref.at[s], sem.at[s], tok_ref[b * blk + r], r).start()
            return c
        lax.fori_loop(0, blk, start, 0, unroll=8)

    @pl.when(i == 0)
    def _():
        gather(0, 0)

    @pl.when(i + 1 < n_used)
    def _():
        gather(i + 1, 1 - slot)

    @pl.when(i < n_used)
    def _():
        _wait_all(buf_ref.at[slot], sem.at[slot])
        xb = buf_ref[slot].astype(BF16)
        gate = _dot(xb, wg_ref[...])
        up = _dot(xb, wu_ref[...])
        mid = (gate * _sigmoid(gate)) * up
        y_ref[...] = _dot(mid.astype(BF16), wd_ref[...]) * roww_ref[...]

    @pl.when(i >= n_used)
    def _():
        y_ref[...] = jnp.zeros(y_ref.shape, F32)


def _expert_stage(h2d, blk_expert, row_tok, n_used, row_w, ew, blk):
    m, d = h2d.shape
    n_blocks = blk_expert.shape[0]
    de = ew["wg"].shape[2]
    wspec = lambda s: pl.BlockSpec((None,) + s, lambda i, be, tok, nb: (be[i], 0, 0))
    grid_spec = pltpu.PrefetchScalarGridSpec(
        num_scalar_prefetch=3,
        grid=(n_blocks,),
        in_specs=[pl.BlockSpec(memory_space=pl.ANY),
                  pl.BlockSpec((blk, 1), lambda i, be, tok, nb: (i, 0)),
                  wspec((d, de)), wspec((d, de)), wspec((de, d))],
        out_specs=pl.BlockSpec((blk, d), lambda i, be, tok, nb: (i, 0)),
        scratch_shapes=[pltpu.VMEM((2, blk, d), F32), pltpu.SemaphoreType.DMA((2,))],
    )
    return pl.pallas_call(
        functools.partial(_expert_kernel, blk=blk),
        grid_spec=grid_spec,
        out_shape=jax.ShapeDtypeStruct((n_blocks * blk, d), F32),
        compiler_params=_cparams(("arbitrary",)),
    )(blk_expert, row_tok, n_used, h2d, row_w[:, None], ew["wg"], ew["wu"], ew["wd"])


def _combine_kernel(pos_ref, x1_ref, y_ref, out_ref, buf_ref, sem, *, tm):
    i = pl.program_id(0)
    slot = i % 2

    def gather(t, s):
        def start(r, c):
            for k in range(TOP_K):
                _row_copy(y_ref, buf_ref.at[s, k], sem.at[s], pos_ref[(t * tm + r) * TOP_K + k], r).start()
            return c
        lax.fori_loop(0, tm, start, 0, unroll=4)

    @pl.when(i == 0)
    def _():
        gather(0, 0)

    @pl.when(i + 1 < pl.num_programs(0))
    def _():
        gather(i + 1, 1 - slot)

    _wait_all(buf_ref.at[slot], sem.at[slot])
    out_ref[...] = x1_ref[...] + (buf_ref[slot, 0] + buf_ref[slot, 1])


def _combine_stage(x1, y_sorted, pos, tm):
    m, d = x1.shape
    grid_spec = pltpu.PrefetchScalarGridSpec(
        num_scalar_prefetch=1,
        grid=(m // tm,),
        in_specs=[pl.BlockSpec((tm, d), lambda i, p: (i, 0)), pl.BlockSpec(memory_space=pl.ANY)],
        out_specs=pl.BlockSpec((tm, d), lambda i, p: (i, 0)),
        scratch_shapes=[pltpu.VMEM((2, TOP_K, tm, d), F32), pltpu.SemaphoreType.DMA((2,))],
    )
    return pl.pallas_call(
        functools.partial(_combine_kernel, tm=tm),
        grid_spec=grid_spec,
        out_shape=jax.ShapeDtypeStruct((m, d), F32),
        compiler_params=_cparams(("arbitrary",)),
    )(pos, x1, y_sorted)


def _moe_stage(x1, h2d, route, ew, blk, tm):
    m, d = h2d.shape
    n_exp = ew["wg"].shape[0]
    n_assign = m * TOP_K
    weights = route[:, 0:TOP_K]
    expert = route[:, TOP_K:2 * TOP_K].astype(jnp.int32)
    i32 = jnp.int32
    flat_e = expert.reshape(n_assign)
    order = jnp.argsort(flat_e).astype(i32)
    rank = jnp.argsort(order).astype(i32)
    counts = jnp.sum(flat_e[:, None] == jnp.arange(n_exp, dtype=i32)[None, :], axis=0, dtype=i32)
    padded = (counts + blk - 1) // blk * blk
    pad_end = jnp.cumsum(padded)
    pad_start = pad_end - padded
    start = jnp.cumsum(counts) - counts
    n_blocks = (n_assign + n_exp * (blk - 1) + blk - 1) // blk
    blk_first = jnp.arange(n_blocks, dtype=i32) * blk
    blk_expert = jnp.minimum(jnp.sum(pad_end[None, :] <= blk_first[:, None], axis=1, dtype=i32), n_exp - 1)
    n_rows = n_blocks * blk
    row = jnp.arange(n_rows, dtype=i32)
    row_e = jnp.repeat(blk_expert, blk)
    off = row - pad_start[row_e]
    live = jnp.logical_and(off >= 0, off < counts[row_e])
    src = order[jnp.clip(start[row_e] + off, 0, n_assign - 1)]
    row_tok = jnp.where(live, src // TOP_K, 0).astype(i32)
    row_w = jnp.where(live, weights.reshape(n_assign)[src], 0.0)
    pos = (pad_start[flat_e] + rank - start[flat_e]).astype(i32)
    n_used = (pad_end[n_exp - 1:n_exp] // blk).astype(i32)
    y_sorted = _expert_stage(h2d, blk_expert, row_tok, n_used, row_w, ew, blk)
    return _combine_stage(x1, y_sorted, pos, tm)


def _rope_tables(pos):
    inv_freq = ROPE_THETA ** (-jnp.arange(ROPE_HALF, dtype=F32) / ROPE_HALF)
    ang = pos.astype(F32)[:, None] * inv_freq[None, :]
    cos, sin = jnp.cos(ang), jnp.sin(ang)
    n = pos.shape[0]
    zeros = lambda k: jnp.zeros((n, k), F32)
    c = jnp.concatenate([cos, cos, jnp.ones((n, HEAD_DIM - ROPE_DIM), F32)], axis=1)
    s1 = jnp.concatenate([-sin, zeros(HEAD_DIM - ROPE_HALF)], axis=1)
    s2 = jnp.concatenate([zeros(ROPE_HALF), sin, zeros(HEAD_DIM - ROPE_DIM)], axis=1)
    return tuple(jnp.tile(a, (1, LANES // HEAD_DIM)) for a in (c, s1, s2))


def _block_diag_ones():
    i = np.arange(LANES)
    return jnp.asarray((i[:, None] // HEAD_DIM == i[None, :] // HEAD_DIM).astype(np.float32), BF16)


def _pack_proj(w_in, norm_mix, q_norm, k_norm, dnw):
    d = w_in.shape[0]
    qc = NSA_HEADS * HEAD_DIM
    kvc = NSA_KV_HEADS * HEAD_DIM
    sizes = (qc,) + (kvc,) * 6 + (NSA_HEADS * 3, 3 * dnw, DN_HEADS, DN_HEADS, dnw, d, d)
    o = np.concatenate([[0], np.cumsum(sizes)])
    seg = lambda i: w_in[:, int(o[i]):int(o[i + 1])]
    wq = seg(0).reshape(d, NSA_HEADS, HEAD_DIM)[:, np.asarray(HEAD_PERM)].reshape(d, qc)
    small = jnp.concatenate([seg(9), seg(10), seg(7), jnp.zeros((d, LANES - 2 * DN_HEADS - 3 * NSA_HEADS), F32)], axis=1)
    w = jnp.concatenate([wq, seg(1), seg(2), seg(3), seg(4), seg(5), seg(6), small, seg(8), seg(11), seg(12), seg(13)],
                        axis=1).astype(BF16)
    offs, ncols = _proj_layout(dnw, d)
    assert ncols == w.shape[1]
    rep = LANES // HEAD_DIM
    return dict(w=w, offs=offs, ncols=ncols, gin=norm_mix[None, :], qg=jnp.tile(q_norm, rep)[None, :],
                kg=jnp.tile(k_norm[0:2], (1, rep)), bd=_block_diag_ones())


def _pack_compress(w_cmp, cmp_pos, k_norm):
    def bdiag(w):
        z = jnp.zeros_like(w)
        return jnp.concatenate([jnp.concatenate([w, z], axis=2), jnp.concatenate([z, w], axis=2)], axis=1)

    s = CMP_STRIDE
    halves = [(0, slice(0, s)), (0, slice(s, 2 * s)), (1, slice(0, s)), (1, slice(s, 2 * s))]
    return dict(w=jnp.stack([bdiag(w_cmp[i, sl]) for i, sl in halves]).astype(BF16),
                p=jnp.stack([jnp.tile(cmp_pos[i, sl], (1, LANES // HEAD_DIM)) for i, sl in halves]),
                kg=jnp.tile(k_norm[2], LANES // HEAD_DIM)[None, :], bd=_block_diag_ones())


def _overlap(nb, ns_pad, n_cmp, n_sel):
    cs = np.arange(nb)[:, None] * CMP_STRIDE
    ss = np.arange(ns_pad)[None, :] * SEL_BLOCK
    ov = (cs <= ss + SEL_BLOCK - 1) & (cs + CMP_BLOCK - 1 >= ss)
    ov &= (np.arange(nb)[:, None] < n_cmp) & (np.arange(ns_pad)[None, :] < n_sel)
    return jnp.asarray(ov.astype(np.float32), BF16)


def _pad_lanes(v, n=LANES):
    return jnp.zeros((1, n), F32).at[0, :v.shape[0]].set(v)


def _round_up(x, n):
    return (x + n - 1) // n * n


def kernel(x_prompt, x_sample, cache_nsa, page_table, state_win, state_dn_conv, state_dn_S, norm_mix, w_in, q_norm, k_norm, cmp_pos, w_cmp, dn_conv_w, dn_A_log, dn_dt_bias, dn_norm, w_proj_a, w_proj_b, w_out, norm_ffn, w_router_g, b_router_g, w_router_e, b_router_e, w_gate, w_up, w_down):
    b, t, d = x_prompt.shape
    bd_, tn, _ = x_sample.shape
    depth = w_in.shape[0]
    assert depth == 1 and tn == 1, "one layer, one new token per sample sequence"
    n_pool, page = cache_nsa.shape[1], cache_nsa.shape[2]
    n_pages = page_table.shape[1]
    past = n_pages * page
    win_rows = state_win.shape[2]
    dnw = dn_conv_w.shape[2] // 3
    assert t % Q_BLOCK == 0 and t >= WINDOW + Q_BLOCK and t % SEL_CHUNK == 0 and past % CMP_STRIDE == 0
    l = 0

    pw = _pack_proj(w_in[l], norm_mix[l], q_norm[l], k_norm[l], dnw)
    cw = _pack_compress(w_cmp[l], cmp_pos[l], k_norm[l])
    dw = dict(cw=dn_conv_w[l], alog=_pad_lanes(dn_A_log[l]), dtb=_pad_lanes(dn_dt_bias[l]), gn=dn_norm[l][None, :])
    perm = np.asarray(HEAD_PERM)
    wr = jnp.zeros((d, LANES), F32).at[:, 0:N_GROUPS].set(w_router_g[l])
    wr = wr.at[:, EXPERT_LANE0:EXPERT_LANE0 + w_router_e.shape[2]].set(w_router_e[l]).astype(BF16)
    br = jnp.zeros((1, LANES), F32).at[0, 0:N_GROUPS].set(b_router_g[l])
    br = br.at[0, EXPERT_LANE0:EXPERT_LANE0 + b_router_e.shape[1]].set(b_router_e[l])
    mw = dict(wpa=w_proj_a[l].reshape(NSA_HEADS, HEAD_DIM, d)[perm].reshape(NSA_HEADS * HEAD_DIM, d).astype(BF16),
              wpb=w_proj_b[l].astype(BF16), wout=w_out[l].astype(BF16), gf=norm_ffn[l][None, :], wr=wr, br=br)
    ew = dict(wg=w_gate[l].astype(BF16), wu=w_up[l].astype(BF16), wd=w_down[l].astype(BF16))

    tm = 256
    tabs_p = _rope_tables(jnp.arange(t))
    q_p, rows_p, win_p, kvb_p, small_p, dnqkv_p, dnz_p, mg_p = _proj_stage(
        x_prompt.reshape(b * t, d), pw, tabs_p, tm, t // tm)
    nb_p = t // CMP_STRIDE
    n_cmp_p = (t - CMP_BLOCK) // CMP_STRIDE + 1
    n_sel_p = -(-t // SEL_BLOCK)
    tabs_cp = _rope_tables(jnp.arange(nb_p) * CMP_STRIDE + CMP_BLOCK - 1)
    kcv_p = _compress_prompt(rows_p.reshape(b, t, 4 * LANES), cw, tabs_cp)
    ovl_p = _overlap(nb_p, _round_up(n_sel_p, LANES), n_cmp_p, n_sel_p)
    oa_p = _attn_prompt(q_p, small_p, kvb_p.reshape(b, t, 4 * LANES), kcv_p, ovl_p, n_cmp_p)
    tb = DN_TB
    ob_p, s_p = _dn_stage(dnqkv_p.reshape(b, t, 3 * dnw), small_p.reshape(b, t, LANES), dnz_p.reshape(b, t, dnw),
                          jnp.zeros((b, SUBLANES, 3 * dnw), F32),
                          jnp.zeros((b, DN_HEADS, DN_HEAD_DIM, DN_HEAD_DIM), F32), dw, DN_CHUNK, tb, t)
    x1_p, h_p, route_p = _merge_stage(x_prompt.reshape(b * t, d), oa_p, ob_p.reshape(b * t, dnw), mg_p, mw, tm)
    y_p = _moe_stage(x1_p, h_p, route_p, ew, 256, tm).reshape(b, t, d)
    kv_p = rows_p.reshape(1, b, t, 4, NSA_KV_HEADS, HEAD_DIM)
    win_all = win_p.reshape(b, t, 2, NSA_KV_HEADS, HEAD_DIM)
    win_out_p = jnp.pad(win_all, ((0, 0), (max(0, win_rows - t), 0), (0, 0), (0, 0), (0, 0)))[:, -win_rows:][None]
    conv_p = dnqkv_p.reshape(b, t, 3 * dnw)[:, t - (DN_CONV - 1):][None]

    tabs_s = _rope_tables(jnp.full((bd_,), past, jnp.int32))
    q_s, rows_s, win_s, _, small_s, dnqkv_s, dnz_s, mg_s = _proj_stage(x_sample.reshape(bd_, d), pw, tabs_s, bd_, 1)
    cache3 = cache_nsa[l].reshape(n_pool, page, 4 * LANES)
    nb_s = past // CMP_STRIDE
    n_cmp_s = (past + tn - CMP_BLOCK) // CMP_STRIDE + 1
    n_sel_s = -(-(past + tn) // SEL_BLOCK)
    assert n_cmp_s == nb_s - 1 and n_cmp_p == nb_p - 1 and (1 << SEL_SHIFT) == SEL_BLOCK
    tabs_cs = _rope_tables(jnp.arange(nb_s) * CMP_STRIDE + CMP_BLOCK - 1)
    kcv_s = _compress_sample(cache3, page_table, cw, tabs_cs)
    ovl_s = _overlap(nb_s, _round_up(n_sel_s, LANES), n_cmp_s, n_sel_s)
    swin3 = state_win[l].reshape(bd_, win_rows, 2 * LANES)
    oa_s = _attn_sample(cache3, page_table, q_s, small_s, rows_s, win_s, kcv_s, swin3, ovl_s, n_cmp_s, n_sel_s)
    pad_t = lambda a: jnp.pad(a[:, None, :], ((0, 0), (0, SUBLANES - tn), (0, 0)))
    conv0_s = jnp.pad(state_dn_conv[l], ((0, 0), (SUBLANES - (DN_CONV - 1), 0), (0, 0)))
    ob_s, s_s = _dn_stage(pad_t(dnqkv_s), pad_t(small_s), pad_t(dnz_s), conv0_s, state_dn_S[l], dw,
                          SUBLANES, SUBLANES, tn)
    x1_s, h_s, route_s = _merge_stage(x_sample.reshape(bd_, d), oa_s, ob_s[:, 0, :], mg_s, mw, bd_)
    y_s = _moe_stage(x1_s, h_s, route_s, ew, 64, bd_).reshape(bd_, tn, d)
    kv_s = rows_s.reshape(1, bd_, tn, 4, NSA_KV_HEADS, HEAD_DIM)
    win_new = win_s.reshape(bd_, tn, 2, NSA_KV_HEADS, HEAD_DIM)
    win_out_s = jnp.concatenate([state_win[l], win_new], axis=1)[:, -win_rows:][None]
    conv_s = jnp.concatenate([state_dn_conv[l], dnqkv_s[:, None, :]], axis=1)[:, -(DN_CONV - 1):][None]

    return (y_p, y_s, kv_p, kv_s, win_out_p, win_out_s, conv_p, conv_s, s_p[None], s_s[None])
```

```python
import functools
import math

import numpy as np
import jax
import jax.numpy as jnp
from jax import lax
from jax.experimental import pallas as pl
from jax.experimental.pallas import tpu as pltpu

F32 = jnp.float32
BF16 = jnp.bfloat16

NSA_HEADS = 8
NSA_KV_HEADS = 2
NSA_GROUP = NSA_HEADS // NSA_KV_HEADS
HEAD_DIM = 64
ROPE_DIM = HEAD_DIM // 4
ROPE_HALF = ROPE_DIM // 2
ROPE_THETA = 500000.0
CMP_BLOCK = 32
CMP_STRIDE = 16
SEL_BLOCK = 64
SEL_SHIFT = 6
SEL_TOPN = 16
WINDOW = 512
FORCE_BONUS = 1000.0
DN_HEADS = 4
DN_HEAD_DIM = 128
DN_CONV = 4
DN_CHUNK = 64
N_GROUPS = 4
EXPERTS_PER_GROUP = 8
TOP_K = 2
EPS = 1e-6

LANES = 128
SUBLANES = 8
VMEM_LIMIT = 56 * 1024 * 1024
NEG = -1e30
Q_BLOCK = 128
SEL_CHUNK = 1024
DN_TB = 256
SHIFT_SLACK = 1.0 + 2.0 ** -5
SHIFT_MIN_SUM = 1e-20
HEAD_PERM = (0, 4, 1, 5, 2, 6, 3, 7)
SM_A, SM_B, SM_GATE = 0, DN_HEADS, 2 * DN_HEADS
EXPERT_LANE0 = 32


def _cparams(sem):
    return pltpu.CompilerParams(dimension_semantics=sem, vmem_limit_bytes=VMEM_LIMIT)


def _dot(a, b):
    return jnp.dot(a, b, preferred_element_type=F32)


def _dot_nt(a, b):
    return lax.dot_general(a, b, (((1,), (1,)), ((), ())), preferred_element_type=F32)


def _dot_tn(a, b):
    return lax.dot_general(a, b, (((0,), (0,)), ((), ())), preferred_element_type=F32)


def _split2(x):
    hi = x.astype(BF16)
    lo = (x - hi.astype(F32)).astype(BF16)
    return hi, lo


def _split3(x):
    hi = x.astype(BF16)
    r = x - hi.astype(F32)
    mid = r.astype(BF16)
    lo = (r - mid.astype(F32)).astype(BF16)
    return hi, mid, lo


def _dot3s(a, b):
    return _dot(a[0], b[0]) + (_dot(a[0], b[1]) + _dot(a[1], b[0]))


def _sigmoid(x):
    return 1.0 / (1.0 + jnp.exp(-x))


def _seg_sumsq(x, bd):
    hi, lo = _split2(x * x)
    return _dot(hi, bd) + _dot(lo, bd)


def _rope128(x, c, s1, s2):
    return x * c + pltpu.roll(x, LANES - ROPE_HALF, 1) * s1 + pltpu.roll(x, ROPE_HALF, 1) * s2


def _head_norm_rope(v, gain, bd, c, s1, s2):
    ss = _seg_sumsq(v, bd) * (1.0 / HEAD_DIM)
    return _rope128(v * lax.rsqrt(ss + EPS) * gain, c, s1, s2)


def _masked_softmax(s, valid):
    s = jnp.where(valid, s, NEG)
    m = jnp.max(s, axis=-1, keepdims=True)
    e = jnp.where(valid, jnp.exp(s - m), 0.0)
    l = jnp.sum(e, axis=-1, keepdims=True)
    return e * (1.0 / jnp.maximum(l, 1e-30))


def _topk_mask(score, k):
    lane = lax.broadcasted_iota(jnp.int32, score.shape, 1).astype(F32)
    big = float(score.shape[-1])

    def body(_, carry):
        sc, sel = carry
        m = jnp.max(sc, axis=-1, keepdims=True)
        idx = jnp.min(jnp.where(sc == m, lane, big), axis=-1, keepdims=True)
        hit = lane == idx
        sel = jnp.where(hit, jnp.maximum(sel, jnp.where(m > -jnp.inf, 1.0, 0.0)), sel)
        sc = jnp.where(hit, -jnp.inf, sc)
        return sc, sel

    _, sel = lax.fori_loop(0, k, body, (score, jnp.zeros(score.shape, F32)))
    return sel


def _topk_mask_rows(score, k):
    idx = lax.broadcasted_iota(jnp.int32, score.shape, 0).astype(F32)
    big = float(score.shape[0])

    def body(_, carry):
        sc, sel = carry
        m = jnp.max(sc, axis=0, keepdims=True)
        first = jnp.min(jnp.where(sc == m, idx, big), axis=0, keepdims=True)
        hit = idx == first
        sel = jnp.where(hit, jnp.maximum(sel, jnp.where(m > -jnp.inf, 1.0, 0.0)), sel)
        sc = jnp.where(hit, -jnp.inf, sc)
        return sc, sel

    _, sel = lax.fori_loop(0, k, body, (score, jnp.zeros(score.shape, F32)))
    return sel


def _proj_layout(dnw, d):
    sizes = dict(q=NSA_HEADS * HEAD_DIM, kvc=2 * LANES, ks=LANES, vs=LANES, kw=LANES, vw=LANES,
                 small=LANES, dnqkv=3 * dnw, dnz=dnw, mg=2 * d)
    offs, o = {}, 0
    for name, n in sizes.items():
        offs[name] = (o, o + n)
        o += n
    return offs, o


def _proj_kernel(x_ref, gin_ref, w_ref, c_ref, s1_ref, s2_ref, qg_ref, kg_ref, bd_ref,
                 q_out, rows_out, win_out, kvb_out, small_out, dnqkv_out, dnz_out, mg_out, *, offs):
    x = x_ref[...]
    ms = jnp.mean(x * x, axis=-1, keepdims=True)
    hb = ((x * lax.rsqrt(ms + EPS)) * gin_ref[...]).astype(BF16)

    def mm(name, lo=0, hi=None):
        a, b = offs[name]
        hi = b - a if hi is None else hi
        return _dot(hb, w_ref[:, a + lo:a + hi])

    c, s1, s2 = c_ref[...], s1_ref[...], s2_ref[...]
    bd = bd_ref[...]
    for j in range(NSA_HEADS * HEAD_DIM // LANES):
        qj = _head_norm_rope(mm("q", LANES * j, LANES * (j + 1)), qg_ref[...], bd, c, s1, s2)
        q_out[:, LANES * j:LANES * (j + 1)] = (qj * (HEAD_DIM ** -0.5)).astype(BF16)
    rows_out[:, 0:2 * LANES] = mm("kvc")
    ks = _head_norm_rope(mm("ks"), kg_ref[0:1, :], bd, c, s1, s2)
    vs = mm("vs")
    rows_out[:, 2 * LANES:3 * LANES] = ks
    rows_out[:, 3 * LANES:4 * LANES] = vs
    kw = _head_norm_rope(mm("kw"), kg_ref[1:2, :], bd, c, s1, s2)
    vw = mm("vw")
    win_out[:, 0:LANES] = kw
    win_out[:, LANES:2 * LANES] = vw
    kvb_out[:, 0:LANES] = ks.astype(BF16)
    kvb_out[:, LANES:2 * LANES] = vs.astype(BF16)
    kvb_out[:, 2 * LANES:3 * LANES] = kw.astype(BF16)
    kvb_out[:, 3 * LANES:4 * LANES] = vw.astype(BF16)
    small_out[...] = mm("small")
    dnqkv_out[...] = mm("dnqkv")
    dnz_out[...] = mm("dnz")
    mg_out[...] = _sigmoid(mm("mg"))


def _proj_stage(x2d, pw, tabs, tm, n_tab_blocks):
    m, d = x2d.shape
    offs, ncols = pw["offs"], pw["ncols"]
    dnw = offs["dnz"][1] - offs["dnz"][0]
    row = lambda w: pl.BlockSpec((tm, w), lambda i: (i, 0))
    full = lambda a: pl.BlockSpec(a.shape, lambda i: (0,) * a.ndim)
    tab = pl.BlockSpec((tm, LANES), lambda i: (i % n_tab_blocks, 0))
    out_shape = (
        jax.ShapeDtypeStruct((m, 4 * LANES), BF16),
        jax.ShapeDtypeStruct((m, 4 * LANES), F32),
        jax.ShapeDtypeStruct((m, 2 * LANES), F32),
        jax.ShapeDtypeStruct((m, 4 * LANES), BF16),
        jax.ShapeDtypeStruct((m, LANES), F32),
        jax.ShapeDtypeStruct((m, 3 * dnw), F32),
        jax.ShapeDtypeStruct((m, dnw), F32),
        jax.ShapeDtypeStruct((m, 2 * d), F32),
    )
    return pl.pallas_call(
        functools.partial(_proj_kernel, offs=offs),
        grid=(m // tm,),
        in_specs=[row(d), full(pw["gin"]), full(pw["w"]), tab, tab, tab,
                  full(pw["qg"]), full(pw["kg"]), full(pw["bd"])],
        out_specs=tuple(row(s.shape[1]) for s in out_shape),
        out_shape=out_shape,
        compiler_params=_cparams(("parallel",)),
    )(x2d, pw["gin"], pw["w"], tabs[0], tabs[1], tabs[2], pw["qg"], pw["kg"], pw["bd"])


def _compress_body(xk_ref, xv_ref, w_ref, p_ref, kg_ref, bd_ref, c_ref, s1_ref, s2_ref, out_ref):
    nb = xk_ref.shape[0] // CMP_STRIDE
    acc = [jnp.zeros((nb, LANES), F32) for _ in range(4)]
    for l in range(CMP_STRIDE):
        xs = (xk_ref[pl.ds(l, nb, stride=CMP_STRIDE), :], xv_ref[pl.ds(l, nb, stride=CMP_STRIDE), :])
        for j in range(4):
            acc[j] = acc[j] + _dot((xs[j // 2] + p_ref[j, l:l + 1, :]).astype(BF16), w_ref[j, l])
    kraw = acc[0] + pltpu.roll(acc[1], nb - 1, 0)
    vraw = acc[2] + pltpu.roll(acc[3], nb - 1, 0)
    kc = _head_norm_rope(kraw, kg_ref[...], bd_ref[...], c_ref[...], s1_ref[...], s2_ref[...])
    out_ref[:, 0:LANES] = kc.astype(BF16)
    out_ref[:, LANES:2 * LANES] = vraw.astype(BF16)


def _compress_prompt_kernel(xk_ref, xv_ref, *rest):
    _compress_body(xk_ref, xv_ref, *rest)


def _compress_prompt(rows3, cw, tabs):
    b, t, _ = rows3.shape
    nb = t // CMP_STRIDE
    full = lambda a: pl.BlockSpec(a.shape, lambda i: (0,) * a.ndim)
    consts = (cw["w"], cw["p"], cw["kg"], cw["bd"]) + tuple(tabs)
    return pl.pallas_call(
        _compress_prompt_kernel,
        grid=(b,),
        in_specs=[pl.BlockSpec((None, t, LANES), lambda i: (i, 0, 0)),
                  pl.BlockSpec((None, t, LANES), lambda i: (i, 0, 1))] + [full(a) for a in consts],
        out_specs=pl.BlockSpec((None, nb, 2 * LANES), lambda i: (i, 0, 0)),
        out_shape=jax.ShapeDtypeStruct((b, nb, 2 * LANES), BF16),
        compiler_params=_cparams(("parallel",)),
    )(rows3, rows3, *consts)


def _wait_all(buf_view, sem):
    pltpu.make_async_copy(buf_view, buf_view, sem).wait()


def _start_pages(pt_ref, cache_ref, buf_ref, sem, b, slot, n_pages, page, lane0):
    def start(p, c):
        for j in range(2):
            pltpu.make_async_copy(
                cache_ref.at[pt_ref[b, p], :, pl.ds(lane0 + j * LANES, LANES)],
                buf_ref.at[slot, j, pl.ds(pl.multiple_of(p * page, page), page), :], sem.at[slot]).start()
        return c

    lax.fori_loop(0, n_pages, start, 0)


def _prefetch_pages(pt_ref, cache_ref, buf_ref, sem, n_pages, page, lane0):
    b = pl.program_id(0)
    slot = b % 2

    @pl.when(b == 0)
    def _():
        _start_pages(pt_ref, cache_ref, buf_ref, sem, 0, 0, n_pages, page, lane0)

    @pl.when(b + 1 < pl.num_programs(0))
    def _():
        _start_pages(pt_ref, cache_ref, buf_ref, sem, b + 1, 1 - slot, n_pages, page, lane0)

    return slot


def _compress_sample_kernel(pt_ref, cache_ref, *rest, n_pages, page):
    *consts, out_ref, buf_ref, sem = rest
    slot = _prefetch_pages(pt_ref, cache_ref, buf_ref, sem, n_pages, page, 0)
    _wait_all(buf_ref.at[slot], sem.at[slot])
    _compress_body(buf_ref.at[slot, 0], buf_ref.at[slot, 1], *consts, out_ref)


def _compress_sample(cache3, page_table, cw, tabs):
    bd_, n_pages = page_table.shape
    page = cache3.shape[1]
    past = n_pages * page
    nb = past // CMP_STRIDE
    full = lambda a: pl.BlockSpec(a.shape, lambda i, pt: (0,) * a.ndim)
    consts = (cw["w"], cw["p"], cw["kg"], cw["bd"]) + tuple(tabs)
    grid_spec = pltpu.PrefetchScalarGridSpec(
        num_scalar_prefetch=1,
        grid=(bd_,),
        in_specs=[pl.BlockSpec(memory_space=pl.ANY)] + [full(a) for a in consts],
        out_specs=pl.BlockSpec((None, nb, 2 * LANES), lambda i, pt: (i, 0, 0)),
        scratch_shapes=[pltpu.VMEM((2, 2, past, LANES), F32), pltpu.SemaphoreType.DMA((2,))],
    )
    return pl.pallas_call(
        functools.partial(_compress_sample_kernel, n_pages=n_pages, page=page),
        grid_spec=grid_spec,
        out_shape=jax.ShapeDtypeStruct((bd_, nb, 2 * LANES), BF16),
        compiler_params=_cparams(("arbitrary",)),
    )(page_table, cache3, *consts)


def _stack_heads(q, lo_half):
    zero = jnp.zeros((), q.dtype)
    parts = []
    for g in range(NSA_KV_HEADS):
        for r in range(NSA_GROUP):
            col = q[:, LANES * r:LANES * (r + 1)]
            parts.append(jnp.where(lo_half if g == 0 else jnp.logical_not(lo_half), col, zero))
    return jnp.concatenate(parts, axis=0)


def _sel_scores(imp, pos_q, blk):
    cur = pos_q >> SEL_SHIFT
    readable = blk * SEL_BLOCK <= pos_q
    forced = jnp.logical_or(blk == 0, jnp.logical_or(blk == cur, blk == cur - 1))
    return jnp.where(readable, imp + jnp.where(forced, FORCE_BONUS, 0.0), -jnp.inf)


def _attn_prompt_kernel(q_ref, small_ref, kvb_ref, kcv_ref, ovl_ref, bd_ref, o_ref, kmx_ref, os_ref, *, n_cmp):
    qb = Q_BLOCK
    m8 = NSA_HEADS * qb
    i = pl.program_id(1)
    q0 = i * qb

    @pl.when(i == 0)
    def _():
        def body(c, mx):
            kf = kvb_ref[pl.ds(pl.multiple_of(c * SEL_CHUNK, SEL_CHUNK), SEL_CHUNK), 0:LANES].astype(F32)
            return jnp.maximum(mx, jnp.max(_seg_sumsq(kf, bd_ref[...]), axis=0, keepdims=True))
        kmx_ref[...] = lax.fori_loop(0, kvb_ref.shape[0] // SEL_CHUNK, body, jnp.zeros((1, LANES), F32))

    lane = lax.broadcasted_iota(jnp.int32, (qb, LANES), 1)
    lo_half = lane < HEAD_DIM
    q8 = _stack_heads(q_ref[...], lo_half)

    def qpos(n):
        return q0 + (lax.broadcasted_iota(jnp.int32, (m8, n), 0) & (qb - 1))

    def kidx(n):
        return lax.broadcasted_iota(jnp.int32, (m8, n), 1)

    nb = kcv_ref.shape[0]
    s = _dot_nt(q8, kcv_ref[:, 0:LANES])
    cidx = kidx(nb)
    valid = jnp.logical_and(cidx < n_cmp, cidx * CMP_STRIDE + (CMP_BLOCK - 1) <= qpos(nb))
    p = _masked_softmax(s, valid).astype(BF16)
    o_c = _dot(p, kcv_ref[:, LANES:2 * LANES])
    imp8 = _dot(p, ovl_ref[...])

    ns = ovl_ref.shape[1]
    imps = []
    for g in range(NSA_KV_HEADS):
        acc = imp8[(g * NSA_GROUP) * qb:(g * NSA_GROUP + 1) * qb]
        for r in range(1, NSA_GROUP):
            acc = acc + imp8[(g * NSA_GROUP + r) * qb:(g * NSA_GROUP + r + 1) * qb]
        imps.append(acc)
    imp = jnp.concatenate(imps, axis=0)
    pos2 = q0 + (lax.broadcasted_iota(jnp.int32, (2 * qb, ns), 0) & (qb - 1))
    blk2 = lax.broadcasted_iota(jnp.int32, (2 * qb, ns), 1)
    sel = jnp.transpose(_topk_mask_rows(jnp.transpose(_sel_scores(imp, pos2, blk2)), SEL_TOPN))
    sel8 = jnp.concatenate([sel[0:qb]] * NSA_GROUP + [sel[qb:2 * qb]] * NSA_GROUP, axis=0) > 0.0

    ck = SEL_CHUNK
    qpos_k = qpos(ck)
    kk = lax.broadcasted_iota(jnp.int32, (ck, ns), 0)
    ss = lax.broadcasted_iota(jnp.int32, (ck, ns), 1)
    n_chunks = (q0 + qb + ck - 1) // ck

    def scores(qaug, c, causal):
        k0 = pl.multiple_of(c * ck, ck)
        e = jnp.where(((k0 + kk) >> SEL_SHIFT) == ss, 1.0, 0.0).astype(BF16)
        sc = _dot_nt(qaug, jnp.concatenate([kvb_ref[pl.ds(k0, ck), 0:LANES], e], axis=1))
        if causal:
            sc = jnp.where(k0 + kidx(ck) <= qpos_k, sc, NEG)
        return sc, kvb_ref[pl.ds(k0, ck), LANES:2 * LANES]

    q8f = q8.astype(F32)
    qn2 = jnp.sum(q8f * q8f, axis=-1, keepdims=True)
    kmx = kmx_ref[...]
    half = NSA_GROUP * qb
    kq2 = jnp.concatenate([qn2[0:half] * kmx[:, 0:1], qn2[half:] * kmx[:, HEAD_DIM:HEAD_DIM + 1]], axis=0)
    shift = jnp.sqrt(kq2) * SHIFT_SLACK + 1e-6
    qaug = jnp.concatenate([q8, jnp.where(sel8, -shift, NEG).astype(BF16)], axis=1)

    def chunk_shifted(c, carry, causal):
        l, acc = carry
        sc, vmat = scores(qaug, c, causal)
        pe = jnp.exp(sc)
        return l + jnp.sum(pe, axis=-1, keepdims=True), acc + _dot(pe.astype(BF16), vmat)

    carry = lax.fori_loop(0, n_chunks - 1, functools.partial(chunk_shifted, causal=False),
                          (jnp.zeros((m8, 1), F32), jnp.zeros((m8, LANES), F32)))
    l_s, acc_s = chunk_shifted(n_chunks - 1, carry, True)
    os_ref[...] = acc_s * (1.0 / jnp.maximum(l_s, 1e-37))

    @pl.when(jnp.min(l_s) < SHIFT_MIN_SUM)
    def _():
        qaug_x = jnp.concatenate([q8, jnp.where(sel8, 0.0, NEG).astype(BF16)], axis=1)

        def chunk_online(c, carry, causal):
            m, l, acc = carry
            sc, vmat = scores(qaug_x, c, causal)
            m_new = jnp.maximum(m, jnp.max(sc, axis=-1, keepdims=True))
            alpha = jnp.exp(m - m_new)
            pe = jnp.exp(sc - m_new)
            return (m_new, alpha * l + jnp.sum(pe, axis=-1, keepdims=True),
                    alpha * acc + _dot(pe.astype(BF16), vmat))

        init = (jnp.full((m8, 1), NEG, F32), jnp.zeros((m8, 1), F32), jnp.zeros((m8, LANES), F32))
        carry_x = lax.fori_loop(0, n_chunks - 1, functools.partial(chunk_online, causal=False), init)
        _, l_x, acc_x = chunk_online(n_chunks - 1, carry_x, True)
        os_ref[...] = acc_x * (1.0 / l_x)

    o_s = os_ref[...]

    wk = WINDOW + qb
    ws = pl.multiple_of(jnp.maximum(q0 - WINDOW, 0), qb)
    s = _dot_nt(q8, kvb_ref[pl.ds(ws, wk), 2 * LANES:3 * LANES])
    dpos = qpos(wk) - (ws + kidx(wk))
    p = _masked_softmax(s, jnp.logical_and(dpos >= 0, dpos < WINDOW)).astype(BF16)
    o_w = _dot(p, kvb_ref[pl.ds(ws, wk), 3 * LANES:4 * LANES])

    sig = _sigmoid(small_ref[...])
    for r in range(NSA_GROUP):
        outs = []
        for g in range(NSA_KV_HEADS):
            h = g * NSA_GROUP + r
            rows = slice(h * qb, (h + 1) * qb)
            gl = SM_GATE + 3 * h
            outs.append(sig[:, gl:gl + 1] * o_c[rows] + sig[:, gl + 1:gl + 2] * o_s[rows]
                        + sig[:, gl + 2:gl + 3] * o_w[rows])
        o_ref[:, LANES * r:LANES * (r + 1)] = jnp.where(lo_half, outs[0], outs[1]).astype(BF16)


def _attn_prompt(q2d, small2d, kvb3, kcv3, ovl, n_cmp):
    b, t, _ = kvb3.shape
    nq = t // Q_BLOCK
    nb = kcv3.shape[1]
    bd = _block_diag_ones()
    return pl.pallas_call(
        functools.partial(_attn_prompt_kernel, n_cmp=n_cmp),
        grid=(b, nq),
        in_specs=[pl.BlockSpec((Q_BLOCK, 4 * LANES), lambda bi, i: (bi * nq + i, 0)),
                  pl.BlockSpec((Q_BLOCK, LANES), lambda bi, i: (bi * nq + i, 0)),
                  pl.BlockSpec((None, t, 4 * LANES), lambda bi, i: (bi, 0, 0)),
                  pl.BlockSpec((None, nb, 2 * LANES), lambda bi, i: (bi, 0, 0)),
                  pl.BlockSpec(ovl.shape, lambda bi, i: (0, 0)),
                  pl.BlockSpec(bd.shape, lambda bi, i: (0, 0))],
        out_specs=pl.BlockSpec((Q_BLOCK, 4 * LANES), lambda bi, i: (bi * nq + i, 0)),
        out_shape=jax.ShapeDtypeStruct((b * t, 4 * LANES), BF16),
        scratch_shapes=[pltpu.VMEM((1, LANES), F32), pltpu.VMEM((NSA_HEADS * Q_BLOCK, LANES), F32)],
        compiler_params=_cparams(("parallel", "arbitrary")),
    )(q2d, small2d, kvb3, kcv3, ovl, bd)


def _attn_sample_kernel(pt_ref, cache_ref, q_ref, small_ref, rows_ref, win_ref, kcv_ref, swin_ref, ovl_ref,
                        o_ref, buf_ref, sem, *, n_pages, page, n_cmp, n_sel, key_chunk):
    past = n_pages * page
    slot = _prefetch_pages(pt_ref, cache_ref, buf_ref, sem, n_pages, page, 2 * LANES)
    lane1 = lax.broadcasted_iota(jnp.int32, (1, LANES), 1)
    lo1 = lane1 < HEAD_DIM
    q8f = _stack_heads(q_ref[...].astype(F32), lo1)
    q8 = q8f.astype(BF16)
    nh = NSA_HEADS

    nb = kcv_ref.shape[0]
    s = _dot_nt(q8, kcv_ref[:, 0:LANES])
    p = _masked_softmax(s, lax.broadcasted_iota(jnp.int32, (nh, nb), 1) < n_cmp).astype(BF16)
    o_c = _dot(p, kcv_ref[:, LANES:2 * LANES])
    imp8 = _dot(p, ovl_ref[...])
    nsp = ovl_ref.shape[1]
    blk = lax.broadcasted_iota(jnp.int32, (1, nsp), 1)
    selbs = []
    for g in range(NSA_KV_HEADS):
        imp = jnp.sum(imp8[g * NSA_GROUP:(g + 1) * NSA_GROUP], axis=0, keepdims=True)
        score = jnp.where(blk < n_sel, _sel_scores(imp, past, blk), -jnp.inf)
        selb = jnp.where(_topk_mask(score, SEL_TOPN) > 0.0, 0.0, NEG)
        selbs += [selb] * NSA_GROUP
    selb8 = jnp.concatenate(selbs, axis=0)
    selb8_b = selb8.astype(BF16)

    _wait_all(buf_ref.at[slot], sem.at[slot])
    ks_new = rows_ref[:, 2 * LANES:3 * LANES].astype(BF16).astype(F32)
    vs_new = rows_ref[:, 3 * LANES:4 * LANES].astype(BF16).astype(F32)
    s_new = jnp.sum(q8f * ks_new, axis=-1, keepdims=True) + selb8[:, n_sel - 1:n_sel]
    ck = key_chunk
    n_ck = past // ck
    ss = lax.broadcasted_iota(jnp.int32, (nsp, ck), 0)
    kk = lax.broadcasted_iota(jnp.int32, (nsp, ck), 1)
    scores = []
    for c in range(n_ck):
        kmat = buf_ref[slot, 0, c * ck:(c + 1) * ck, :].astype(BF16)
        e = jnp.where(((c * ck + kk) >> SEL_SHIFT) == ss, 1.0, 0.0).astype(BF16)
        scores.append(_dot_nt(q8, kmat) + _dot(selb8_b, e))
    m = s_new
    for sc in scores:
        m = jnp.maximum(m, jnp.max(sc, axis=-1, keepdims=True))
    p_new = jnp.exp(s_new - m)
    l = p_new
    acc = p_new * vs_new
    for c, sc in enumerate(scores):
        pe = jnp.exp(sc - m)
        l = l + jnp.sum(pe, axis=-1, keepdims=True)
        acc = acc + _dot(pe.astype(BF16), buf_ref[slot, 1, c * ck:(c + 1) * ck, :].astype(BF16))
    o_s = acc * (1.0 / l)

    wr = swin_ref.shape[0]
    s = _dot_nt(q8, swin_ref[:, 0:LANES].astype(BF16))
    j = lax.broadcasted_iota(jnp.int32, (nh, wr), 1)
    valid = wr - j < WINDOW
    kw_new = win_ref[:, 0:LANES].astype(BF16).astype(F32)
    vw_new = win_ref[:, LANES:2 * LANES].astype(BF16).astype(F32)
    s_new = jnp.sum(q8f * kw_new, axis=-1, keepdims=True)
    s = jnp.where(valid, s, NEG)
    m = jnp.maximum(s_new, jnp.max(s, axis=-1, keepdims=True))
    pe = jnp.where(valid, jnp.exp(s - m), 0.0)
    p_new = jnp.exp(s_new - m)
    l = p_new + jnp.sum(pe, axis=-1, keepdims=True)
    o_w = (p_new * vw_new + _dot(pe.astype(BF16), swin_ref[:, LANES:2 * LANES].astype(BF16))) * (1.0 / l)

    sig = _sigmoid(small_ref[...])
    for r in range(NSA_GROUP):
        outs = []
        for g in range(NSA_KV_HEADS):
            h = g * NSA_GROUP + r
            gl = SM_GATE + 3 * h
            outs.append(sig[:, gl:gl + 1] * o_c[h:h + 1] + sig[:, gl + 1:gl + 2] * o_s[h:h + 1]
                        + sig[:, gl + 2:gl + 3] * o_w[h:h + 1])
        o_ref[:, LANES * r:LANES * (r + 1)] = jnp.where(lo1, outs[0], outs[1]).astype(BF16)


def _attn_sample(cache3, page_table, q2d, small2d, rows2d, win2d, kcv3, swin3, ovl, n_cmp, n_sel):
    bd_, n_pages = page_table.shape
    page = cache3.shape[1]
    past = n_pages * page
    key_chunk = math.gcd(past, 2048)
    per_seq = lambda a: pl.BlockSpec((None,) + a.shape[1:], lambda i, pt: (i,) + (0,) * (a.ndim - 1))
    q3, small3, rows3, win3 = (a[:, None, :] for a in (q2d, small2d, rows2d, win2d))
    grid_spec = pltpu.PrefetchScalarGridSpec(
        num_scalar_prefetch=1,
        grid=(bd_,),
        in_specs=[pl.BlockSpec(memory_space=pl.ANY), per_seq(q3), per_seq(small3), per_seq(rows3), per_seq(win3),
                  per_seq(kcv3), per_seq(swin3), pl.BlockSpec(ovl.shape, lambda i, pt: (0, 0))],
        out_specs=pl.BlockSpec((None, 1, 4 * LANES), lambda i, pt: (i, 0, 0)),
        scratch_shapes=[pltpu.VMEM((2, 2, past, LANES), F32), pltpu.SemaphoreType.DMA((2,))],
    )
    out = pl.pallas_call(
        functools.partial(_attn_sample_kernel, n_pages=n_pages, page=page, n_cmp=n_cmp, n_sel=n_sel,
                          key_chunk=key_chunk),
        grid_spec=grid_spec,
        out_shape=jax.ShapeDtypeStruct((bd_, 1, 4 * LANES), BF16),
        compiler_params=_cparams(("arbitrary",)),
    )(page_table, cache3, q3, small3, rows3, win3, kcv3, swin3, ovl)
    return out[:, 0, :]


def _dn_kernel(x_ref, small_ref, z_ref, cw_ref, c0_ref, s0_ref, alog_ref, dtb_ref, gn_ref,
               o_ref, sout_ref, s_scr, xs_scr, *, chunk, tb, t_valid):
    j = pl.program_id(1)
    dnw = z_ref.shape[1]
    hd = DN_HEAD_DIM
    hist = SUBLANES

    @pl.when(j == 0)
    def _():
        s_scr[...] = s0_ref[...]
        xs_scr[0:hist, :] = c0_ref[...]

    xs_scr[hist:hist + tb, :] = x_ref[...]
    conv = xs_scr[pl.ds(hist - (DN_CONV - 1), tb), :] * cw_ref[0:1, :]
    for jj in range(1, DN_CONV):
        conv = conv + xs_scr[pl.ds(hist - (DN_CONV - 1) + jj, tb), :] * cw_ref[jj:jj + 1, :]
    xs_scr[0:hist, :] = xs_scr[tb:tb + hist, :]
    act = conv * _sigmoid(conv)

    small = small_ref[...]
    tpos = j * tb + lax.broadcasted_iota(jnp.int32, (tb, LANES), 0)
    live = tpos < t_valid
    xg = small + dtb_ref[...]
    softplus = jnp.maximum(xg, 0.0) + jnp.log(1.0 + jnp.exp(-jnp.abs(xg)))
    g_all = jnp.where(live, -jnp.exp(alog_ref[...]) * softplus, 0.0)
    beta_all = jnp.where(live, _sigmoid(small), 0.0)

    ri = lax.broadcasted_iota(jnp.int32, (chunk, chunk), 0)
    ci = lax.broadcasted_iota(jnp.int32, (chunk, chunk), 1)
    incl = ri >= ci
    strict = ri > ci
    ltri = jnp.where(incl, 1.0, 0.0).astype(BF16)
    eye = jnp.where(ri == ci, 1.0, 0.0)
    lane_c = lax.broadcasted_iota(jnp.int32, (chunk, LANES), 1)
    n_sq = max(1, int(math.ceil(math.log2(chunk))))

    n_chunks = tb // chunk
    items = [(c, h) for c in range(n_chunks) for h in range(DN_HEADS)]

    gc_alls, gc3s = [], []
    for c in range(n_chunks):
        g3 = _split3(g_all[c * chunk:(c + 1) * chunk])
        gc_all = _dot(ltri, g3[0]) + (_dot(ltri, g3[1]) + _dot(ltri, g3[2]))
        gc_alls.append(gc_all)
        gc3s.append(_split3(gc_all))

    prep = []
    for c, h in items:
        rs = slice(c * chunk, (c + 1) * chunk)
        q = act[rs, h * hd:(h + 1) * hd]
        k = act[rs, dnw + h * hd:dnw + (h + 1) * hd]
        v = act[rs, 2 * dnw + h * hd:2 * dnw + (h + 1) * hd]
        q = q * lax.rsqrt(jnp.sum(q * q, axis=-1, keepdims=True) + EPS) * (hd ** -0.5)
        k = k * lax.rsqrt(jnp.sum(k * k, axis=-1, keepdims=True) + EPS)
        beta = beta_all[rs, SM_B + h:SM_B + h + 1]
        gc = gc_alls[c][:, SM_A + h:SM_A + h + 1]
        pick = jnp.where(lane_c == SM_A + h, 1.0, 0.0).astype(BF16)
        g3 = gc3s[c]
        gc_row = _dot_nt(pick, g3[0]) + (_dot_nt(pick, g3[1]) + _dot_nt(pick, g3[2]))
        dmask = jnp.where(incl, jnp.exp(jnp.where(incl, gc - gc_row, 0.0)), 0.0)
        kb = k * beta
        kbf = k.astype(BF16)
        a_strict = jnp.where(strict, _dot_nt(kb.astype(BF16), kbf) * dmask, 0.0)
        egc = jnp.exp(gc)
        g_last = gc[chunk - 1:chunk, :]
        prep.append(dict(
            vb=(v * beta).astype(BF16), kbg=(kb * egc).astype(BF16), qg=(q * egc).astype(BF16),
            qk=(_dot_nt(q.astype(BF16), kbf) * dmask).astype(BF16),
            kd=(k * jnp.exp(g_last - gc)).astype(BF16), decay=jnp.exp(g_last), n=-a_strict))

    tinv = [eye + p["n"] for p in prep]
    nsp = [_split2(p["n"]) for p in prep]
    for _ in range(n_sq - 1):
        nsp = [_split2(_dot3s(n, n)) for n in nsp]
        tinv = [t + _dot3s(_split2(t), n) for t, n in zip(tinv, nsp)]
    us, ws = [], []
    for p, t in zip(prep, tinv):
        tb16 = t.astype(BF16)
        us.append(_dot(tb16, p["vb"]))
        ws.append(_dot(tb16, p["kbg"]).astype(BF16))

    state = [s_scr[h] for h in range(DN_HEADS)]
    for i, (c, h) in enumerate(items):
        p = prep[i]
        rs = slice(c * chunk, (c + 1) * chunk)
        s_b = state[h].astype(BF16)
        v_new = (us[i] - _dot(ws[i], s_b)).astype(BF16)
        o = _dot(p["qg"], s_b) + _dot(p["qk"], v_new)
        state[h] = state[h] * p["decay"] + _dot_tn(p["kd"], v_new)
        on = o * lax.rsqrt(jnp.mean(o * o, axis=-1, keepdims=True) + EPS) * gn_ref[...]
        z = z_ref[rs, h * hd:(h + 1) * hd]
        o_ref[rs, h * hd:(h + 1) * hd] = (on * (z * _sigmoid(z))).astype(BF16)
    for h in range(DN_HEADS):
        s_scr[h] = state[h]

    @pl.when(j == pl.num_programs(1) - 1)
    def _():
        sout_ref[...] = s_scr[...]


def _dn_stage(x3, small3, z3, conv0, s0, dw, chunk, tb, t_valid):
    b, tpad, w3 = x3.shape
    dnw = w3 // 3
    nblk = tpad // tb
    full = lambda a: pl.BlockSpec(a.shape, lambda bi, j: (0,) * a.ndim)
    tok = lambda w: pl.BlockSpec((None, tb, w), lambda bi, j: (bi, j, 0))
    o, s_out = pl.pallas_call(
        functools.partial(_dn_kernel, chunk=chunk, tb=tb, t_valid=t_valid),
        grid=(b, nblk),
        in_specs=[tok(w3), tok(LANES), tok(dnw), full(dw["cw"]),
                  pl.BlockSpec((None, SUBLANES, w3), lambda bi, j: (bi, 0, 0)),
                  pl.BlockSpec((None, DN_HEADS, DN_HEAD_DIM, DN_HEAD_DIM), lambda bi, j: (bi, 0, 0, 0)),
                  full(dw["alog"]), full(dw["dtb"]), full(dw["gn"])],
        out_specs=(tok(dnw),
                   pl.BlockSpec((None, DN_HEADS, DN_HEAD_DIM, DN_HEAD_DIM), lambda bi, j: (bi, 0, 0, 0))),
        out_shape=(jax.ShapeDtypeStruct((b, tpad, dnw), BF16),
                   jax.ShapeDtypeStruct((b, DN_HEADS, DN_HEAD_DIM, DN_HEAD_DIM), F32)),
        scratch_shapes=[pltpu.VMEM((DN_HEADS, DN_HEAD_DIM, DN_HEAD_DIM), F32),
                        pltpu.VMEM((tb + SUBLANES, w3), F32)],
        compiler_params=_cparams(("parallel", "arbitrary")),
    )(x3, small3, z3, dw["cw"], conv0, s0, dw["alog"], dw["dtb"], dw["gn"])
    return o, s_out


def _merge_kernel(x_ref, oa_ref, ob_ref, mg_ref, wpa_ref, wpb_ref, wout_ref, gf_ref, wr_ref, br_ref,
                  x1_ref, h_ref, route_ref):
    d = x_ref.shape[1]
    mixed = mg_ref[:, 0:d] * _dot(oa_ref[...], wpa_ref[...]) + mg_ref[:, d:2 * d] * _dot(ob_ref[...], wpb_ref[...])
    x1 = x_ref[...] + _dot(mixed.astype(BF16), wout_ref[...])
    x1_ref[...] = x1
    h = (x1 * lax.rsqrt(jnp.mean(x1 * x1, axis=-1, keepdims=True) + EPS)) * gf_ref[...]
    h_ref[...] = h
    logits = _dot(h.astype(BF16), wr_ref[...]) + br_ref[...]
    lane = lax.broadcasted_iota(jnp.int32, logits.shape, 1)
    lanef = lane.astype(F32)
    big = float(LANES)
    is_g = lane < N_GROUPS
    lg = jnp.where(is_g, logits, NEG)
    eg = jnp.where(is_g, jnp.exp(lg - jnp.max(lg, axis=-1, keepdims=True)), 0.0)
    pg = eg / jnp.sum(eg, axis=-1, keepdims=True)
    p_top = jnp.max(pg, axis=-1, keepdims=True)
    grp = jnp.min(jnp.where(jnp.logical_and(is_g, pg == p_top), lanef, big), axis=-1, keepdims=True)
    e_lo = EXPERT_LANE0 + grp * EXPERTS_PER_GROUP
    in_grp = jnp.logical_and(lanef >= e_lo, lanef < e_lo + EXPERTS_PER_GROUP)
    le = jnp.where(in_grp, logits, -jnp.inf)
    v0 = jnp.max(le, axis=-1, keepdims=True)
    i0 = jnp.min(jnp.where(le == v0, lanef, big), axis=-1, keepdims=True)
    le1 = jnp.where(lanef == i0, -jnp.inf, le)
    v1 = jnp.max(le1, axis=-1, keepdims=True)
    i1 = jnp.min(jnp.where(le1 == v1, lanef, big), axis=-1, keepdims=True)
    e1 = jnp.exp(v1 - v0)
    den = 1.0 + e1
    w0 = p_top * (1.0 / den)
    w1 = p_top * (e1 / den)
    route = jnp.where(lane == 0, w0, jnp.where(lane == 1, w1, jnp.where(
        lane == 2, i0 - EXPERT_LANE0, jnp.where(lane == 3, i1 - EXPERT_LANE0, 0.0))))
    route_ref[...] = route


def _merge_stage(x2d, oa, ob, mg, mw, tm):
    m, d = x2d.shape
    row = lambda w: pl.BlockSpec((tm, w), lambda i: (i, 0))
    full = lambda a: pl.BlockSpec(a.shape, lambda i: (0,) * a.ndim)
    consts = (mw["wpa"], mw["wpb"], mw["wout"], mw["gf"], mw["wr"], mw["br"])
    return pl.pallas_call(
        _merge_kernel,
        grid=(m // tm,),
        in_specs=[row(d), row(oa.shape[1]), row(ob.shape[1]), row(2 * d)] + [full(a) for a in consts],
        out_specs=(row(d), row(d), row(LANES)),
        out_shape=(jax.ShapeDtypeStruct((m, d), F32), jax.ShapeDtypeStruct((m, d), F32),
                   jax.ShapeDtypeStruct((m, LANES), F32)),
        compiler_params=_cparams(("parallel",)),
    )(x2d, oa, ob, mg, *consts)


def _row_copy(src_ref, dst_ref, sem, src_row, dst_row):
    return pltpu.make_async_copy(src_ref.at[pl.ds(src_row, 1), :], dst_ref.at[pl.ds(dst_row, 1), :], sem)


def _expert_kernel(be_ref, tok_ref, nblk_ref, h_ref, roww_ref, wg_ref, wu_ref, wd_ref, y_ref, buf_ref, sem, *, blk):
    i = pl.program_id(0)
    n_used = nblk_ref[0]
    slot = i % 2

    def gather(b, s):
        def start(r, c):
            _row_copy(h_ref, buf_ref.at[s], sem.at[s], tok_ref[b * blk + r], r).start()
            return c
        lax.fori_loop(0, blk, start, 0, unroll=8)

    @pl.when(i == 0)
    def _():
        gather(0, 0)

    @pl.when(i + 1 < n_used)
    def _():
        gather(i + 1, 1 - slot)

    @pl.when(i < n_used)
    def _():
        _wait_all(buf_ref.at[slot], sem.at[slot])
        xb = buf_ref[slot].astype(BF16)
        gate = _dot(xb, wg_ref[...])
        up = _dot(xb, wu_ref[...])
        mid = (gate * _sigmoid(gate)) * up
        y_ref[...] = _dot(mid.astype(BF16), wd_ref[...]) * roww_ref[...]

    @pl.when(i >= n_used)
    def _():
        y_ref[...] = jnp.zeros(y_ref.shape, F32)


def _expert_stage(h2d, blk_expert, row_tok, n_used, row_w, ew, blk):
    m, d = h2d.shape
    n_blocks = blk_expert.shape[0]
    de = ew["wg"].shape[2]
    wspec = lambda s: pl.BlockSpec((None,) + s, lambda i, be, tok, nb: (be[i], 0, 0))
    grid_spec = pltpu.PrefetchScalarGridSpec(
        num_scalar_prefetch=3,
        grid=(n_blocks,),
        in_specs=[pl.BlockSpec(memory_space=pl.ANY),
                  pl.BlockSpec((blk, 1), lambda i, be, tok, nb: (i, 0)),
                  wspec((d, de)), wspec((d, de)), wspec((de, d))],
        out_specs=pl.BlockSpec((blk, d), lambda i, be, tok, nb: (i, 0)),
        scratch_shapes=[pltpu.VMEM((2, blk, d), F32), pltpu.SemaphoreType.DMA((2,))],
    )
    return pl.pallas_call(
        functools.partial(_expert_kernel, blk=blk),
        grid_spec=grid_spec,
        out_shape=jax.ShapeDtypeStruct((n_blocks * blk, d), F32),
        compiler_params=_cparams(("arbitrary",)),
    )(blk_expert, row_tok, n_used, h2d, row_w[:, None], ew["wg"], ew["wu"], ew["wd"])


def _combine_kernel(pos_ref, x1_ref, y_ref, out_ref, buf_ref, sem, *, tm):
    i = pl.program_id(0)
    slot = i % 2

    def gather(t, s):
        def start(r, c):
            for k in range(TOP_K):
                _row_copy(y_ref, buf_ref.at[s, k], sem.at[s], pos_ref[(t * tm + r) * TOP_K + k], r).start()
            return c
        lax.fori_loop(0, tm, start, 0, unroll=4)

    @pl.when(i == 0)
    def _():
        gather(0, 0)

    @pl.when(i + 1 < pl.num_programs(0))
    def _():
        gather(i + 1, 1 - slot)

    _wait_all(buf_ref.at[slot], sem.at[slot])
    out_ref[...] = x1_ref[...] + (buf_ref[slot, 0] + buf_ref[slot, 1])


def _combine_stage(x1, y_sorted, pos, tm):
    m, d = x1.shape
    grid_spec = pltpu.PrefetchScalarGridSpec(
        num_scalar_prefetch=1,
        grid=(m // tm,),
        in_specs=[pl.BlockSpec((tm, d), lambda i, p: (i, 0)), pl.BlockSpec(memory_space=pl.ANY)],
        out_specs=pl.BlockSpec((tm, d), lambda i, p: (i, 0)),
        scratch_shapes=[pltpu.VMEM((2, TOP_K, tm, d), F32), pltpu.SemaphoreType.DMA((2,))],
    )
    return pl.pallas_call(
        functools.partial(_combine_kernel, tm=tm),
        grid_spec=grid_spec,
        out_shape=jax.ShapeDtypeStruct((m, d), F32),
        compiler_params=_cparams(("arbitrary",)),
    )(pos, x1, y_sorted)


def _moe_stage(x1, h2d, route, ew, blk, tm):
    m, d = h2d.shape
    n_exp = ew["wg"].shape[0]
    n_assign = m * TOP_K
    weights = route[:, 0:TOP_K]
    expert = route[:, TOP_K:2 * TOP_K].astype(jnp.int32)
    i32 = jnp.int32
    flat_e = expert.reshape(n_assign)
    order = jnp.argsort(flat_e).astype(i32)
    rank = jnp.argsort(order).astype(i32)
    counts = jnp.sum(flat_e[None, :] == jnp.arange(n_exp, dtype=i32)[:, None], axis=1, dtype=i32)
    padded = (counts + blk - 1) // blk * blk
    pad_end = jnp.cumsum(padded)
    pad_start = pad_end - padded
    start = jnp.cumsum(counts) - counts
    n_blocks = (n_assign + n_exp * (blk - 1) + blk - 1) // blk
    blk_first = jnp.arange(n_blocks, dtype=i32) * blk
    blk_expert = jnp.minimum(jnp.sum(pad_end[None, :] <= blk_first[:, None], axis=1, dtype=i32), n_exp - 1)
    n_rows = n_blocks * blk
    row = jnp.arange(n_rows, dtype=i32)
    row_e = jnp.repeat(blk_expert, blk)
    off = row - pad_start[row_e]
    live = jnp.logical_and(off >= 0, off < counts[row_e])
    src = order[jnp.clip(start[row_e] + off, 0, n_assign - 1)]
    row_tok = jnp.where(live, src // TOP_K, 0).astype(i32)
    row_w = jnp.where(live, weights.reshape(n_assign)[src], 0.0)
    pos = (pad_start[flat_e] + rank - start[flat_e]).astype(i32)
    n_used = (pad_end[n_exp - 1:n_exp] // blk).astype(i32)
    y_sorted = _expert_stage(h2d, blk_expert, row_tok, n_used, row_w, ew, blk)
    return _combine_stage(x1, y_sorted, pos, tm)


def _rope_tables(pos):
    inv_freq = ROPE_THETA ** (-jnp.arange(ROPE_HALF, dtype=F32) / ROPE_HALF)
    ang = pos.astype(F32)[:, None] * inv_freq[None, :]
    cos, sin = jnp.cos(ang), jnp.sin(ang)
    n = pos.shape[0]
    zeros = lambda k: jnp.zeros((n, k), F32)
    c = jnp.concatenate([cos, cos, jnp.ones((n, HEAD_DIM - ROPE_DIM), F32)], axis=1)
    s1 = jnp.concatenate([-sin, zeros(HEAD_DIM - ROPE_HALF)], axis=1)
    s2 = jnp.concatenate([zeros(ROPE_HALF), sin, zeros(HEAD_DIM - ROPE_DIM)], axis=1)
    return tuple(jnp.tile(a, (1, LANES // HEAD_DIM)) for a in (c, s1, s2))


def _block_diag_ones():
    i = np.arange(LANES)
    return jnp.asarray((i[:, None] // HEAD_DIM == i[None, :] // HEAD_DIM).astype(np.float32), BF16)


def _pack_proj(w_in, norm_mix, q_norm, k_norm, dnw):
    d = w_in.shape[0]
    qc = NSA_HEADS * HEAD_DIM
    kvc = NSA_KV_HEADS * HEAD_DIM
    sizes = (qc,) + (kvc,) * 6 + (NSA_HEADS * 3, 3 * dnw, DN_HEADS, DN_HEADS, dnw, d, d)
    o = np.concatenate([[0], np.cumsum(sizes)])
    seg = lambda i: w_in[:, int(o[i]):int(o[i + 1])]
    wq = seg(0).reshape(d, NSA_HEADS, HEAD_DIM)[:, np.asarray(HEAD_PERM)].reshape(d, qc)
    small = jnp.concatenate([seg(9), seg(10), seg(7), jnp.zeros((d, LANES - 2 * DN_HEADS - 3 * NSA_HEADS), F32)], axis=1)
    w = jnp.concatenate([wq, seg(1), seg(2), seg(3), seg(4), seg(5), seg(6), small, seg(8), seg(11), seg(12), seg(13)],
                        axis=1).astype(BF16)
    offs, ncols = _proj_layout(dnw, d)
    assert ncols == w.shape[1]
    rep = LANES // HEAD_DIM
    return dict(w=w, offs=offs, ncols=ncols, gin=norm_mix[None, :], qg=jnp.tile(q_norm, rep)[None, :],
                kg=jnp.tile(k_norm[0:2], (1, rep)), bd=_block_diag_ones())


def _pack_compress(w_cmp, cmp_pos, k_norm):
    def bdiag(w):
        z = jnp.zeros_like(w)
        return jnp.concatenate([jnp.concatenate([w, z], axis=2), jnp.concatenate([z, w], axis=2)], axis=1)

    s = CMP_STRIDE
    halves = [(0, slice(0, s)), (0, slice(s, 2 * s)), (1, slice(0, s)), (1, slice(s, 2 * s))]
    return dict(w=jnp.stack([bdiag(w_cmp[i, sl]) for i, sl in halves]).astype(BF16),
                p=jnp.stack([jnp.tile(cmp_pos[i, sl], (1, LANES // HEAD_DIM)) for i, sl in halves]),
                kg=jnp.tile(k_norm[2], LANES // HEAD_DIM)[None, :], bd=_block_diag_ones())


def _overlap(nb, ns_pad, n_cmp, n_sel):
    cs = np.arange(nb)[:, None] * CMP_STRIDE
    ss = np.arange(ns_pad)[None, :] * SEL_BLOCK
    ov = (cs <= ss + SEL_BLOCK - 1) & (cs + CMP_BLOCK - 1 >= ss)
    ov &= (np.arange(nb)[:, None] < n_cmp) & (np.arange(ns_pad)[None, :] < n_sel)
    return jnp.asarray(ov.astype(np.float32), BF16)


def _pad_lanes(v, n=LANES):
    return jnp.zeros((1, n), F32).at[0, :v.shape[0]].set(v)


def _round_up(x, n):
    return (x + n - 1) // n * n


def kernel(x_prompt, x_sample, cache_nsa, page_table, state_win, state_dn_conv, state_dn_S, norm_mix, w_in, q_norm, k_norm, cmp_pos, w_cmp, dn_conv_w, dn_A_log, dn_dt_bias, dn_norm, w_proj_a, w_proj_b, w_out, norm_ffn, w_router_g, b_router_g, w_router_e, b_router_e, w_gate, w_up, w_down):
    b, t, d = x_prompt.shape
    bd_, tn, _ = x_sample.shape
    depth = w_in.shape[0]
    assert depth == 1 and tn == 1, "one layer, one new token per sample sequence"
    n_pool, page = cache_nsa.shape[1], cache_nsa.shape[2]
    n_pages = page_table.shape[1]
    past = n_pages * page
    win_rows = state_win.shape[2]
    dnw = dn_conv_w.shape[2] // 3
    assert t % Q_BLOCK == 0 and t >= WINDOW + Q_BLOCK and t % SEL_CHUNK == 0 and past % CMP_STRIDE == 0
    l = 0

    pw = _pack_proj(w_in[l], norm_mix[l], q_norm[l], k_norm[l], dnw)
    cw = _pack_compress(w_cmp[l], cmp_pos[l], k_norm[l])
    dw = dict(cw=dn_conv_w[l], alog=_pad_lanes(dn_A_log[l]), dtb=_pad_lanes(dn_dt_bias[l]), gn=dn_norm[l][None, :])
    perm = np.asarray(HEAD_PERM)
    wr = jnp.zeros((d, LANES), F32).at[:, 0:N_GROUPS].set(w_router_g[l])
    wr = wr.at[:, EXPERT_LANE0:EXPERT_LANE0 + w_router_e.shape[2]].set(w_router_e[l]).astype(BF16)
    br = jnp.zeros((1, LANES), F32).at[0, 0:N_GROUPS].set(b_router_g[l])
    br = br.at[0, EXPERT_LANE0:EXPERT_LANE0 + b_router_e.shape[1]].set(b_router_e[l])
    mw = dict(wpa=w_proj_a[l].reshape(NSA_HEADS, HEAD_DIM, d)[perm].reshape(NSA_HEADS * HEAD_DIM, d).astype(BF16),
              wpb=w_proj_b[l].astype(BF16), wout=w_out[l].astype(BF16), gf=norm_ffn[l][None, :], wr=wr, br=br)
    ew = dict(wg=w_gate[l].astype(BF16), wu=w_up[l].astype(BF16), wd=w_down[l].astype(BF16))

    tm = 256
    tabs_p = _rope_tables(jnp.arange(t))
    q_p, rows_p, win_p, kvb_p, small_p, dnqkv_p, dnz_p, mg_p = _proj_stage(
        x_prompt.reshape(b * t, d), pw, tabs_p, tm, t // tm)
    nb_p = t // CMP_STRIDE
    n_cmp_p = (t - CMP_BLOCK) // CMP_STRIDE + 1
    n_sel_p = -(-t // SEL_BLOCK)
    tabs_cp = _rope_tables(jnp.arange(nb_p) * CMP_STRIDE + CMP_BLOCK - 1)
    kcv_p = _compress_prompt(rows_p.reshape(b, t, 4 * LANES), cw, tabs_cp)
    ovl_p = _overlap(nb_p, _round_up(n_sel_p, LANES), n_cmp_p, n_sel_p)
    oa_p = _attn_prompt(q_p, small_p, kvb_p.reshape(b, t, 4 * LANES), kcv_p, ovl_p, n_cmp_p)
    tb = DN_TB
    ob_p, s_p = _dn_stage(dnqkv_p.reshape(b, t, 3 * dnw), small_p.reshape(b, t, LANES), dnz_p.reshape(b, t, dnw),
                          jnp.zeros((b, SUBLANES, 3 * dnw), F32),
                          jnp.zeros((b, DN_HEADS, DN_HEAD_DIM, DN_HEAD_DIM), F32), dw, DN_CHUNK, tb, t)
    x1_p, h_p, route_p = _merge_stage(x_prompt.reshape(b * t, d), oa_p, ob_p.reshape(b * t, dnw), mg_p, mw, tm)
    y_p = _moe_stage(x1_p, h_p, route_p, ew, 256, tm).reshape(b, t, d)
    kv_p = rows_p.reshape(1, b, t, 4, NSA_KV_HEADS, HEAD_DIM)
    win_all = win_p.reshape(b, t, 2, NSA_KV_HEADS, HEAD_DIM)
    win_out_p = jnp.pad(win_all, ((0, 0), (max(0, win_rows - t), 0), (0, 0), (0, 0), (0, 0)))[:, -win_rows:][None]
    conv_p = dnqkv_p.reshape(b, t, 3 * dnw)[:, t - (DN_CONV - 1):][None]

    tabs_s = _rope_tables(jnp.full((bd_,), past, jnp.int32))
    q_s, rows_s, win_s, _, small_s, dnqkv_s, dnz_s, mg_s = _proj_stage(x_sample.reshape(bd_, d), pw, tabs_s, bd_, 1)
    cache3 = cache_nsa[l].reshape(n_pool, page, 4 * LANES)
    nb_s = past // CMP_STRIDE
    n_cmp_s = (past + tn - CMP_BLOCK) // CMP_STRIDE + 1
    n_sel_s = -(-(past + tn) // SEL_BLOCK)
    assert n_cmp_s == nb_s - 1 and n_cmp_p == nb_p - 1 and (1 << SEL_SHIFT) == SEL_BLOCK
    tabs_cs = _rope_tables(jnp.arange(nb_s) * CMP_STRIDE + CMP_BLOCK - 1)
    kcv_s = _compress_sample(cache3, page_table, cw, tabs_cs)
    ovl_s = _overlap(nb_s, _round_up(n_sel_s, LANES), n_cmp_s, n_sel_s)
    swin3 = state_win[l].reshape(bd_, win_rows, 2 * LANES)
    oa_s = _attn_sample(cache3, page_table, q_s, small_s, rows_s, win_s, kcv_s, swin3, ovl_s, n_cmp_s, n_sel_s)
    pad_t = lambda a: jnp.pad(a[:, None, :], ((0, 0), (0, SUBLANES - tn), (0, 0)))
    conv0_s = jnp.pad(state_dn_conv[l], ((0, 0), (SUBLANES - (DN_CONV - 1), 0), (0, 0)))
    ob_s, s_s = _dn_stage(pad_t(dnqkv_s), pad_t(small_s), pad_t(dnz_s), conv0_s, state_dn_S[l], dw,
                          SUBLANES, SUBLANES, tn)
    x1_s, h_s, route_s = _merge_stage(x_sample.reshape(bd_, d), oa_s, ob_s[:, 0, :], mg_s, mw, bd_)
    y_s = _moe_stage(x1_s, h_s, route_s, ew, 64, bd_).reshape(bd_, tn, d)
    kv_s = rows_s.reshape(1, bd_, tn, 4, NSA_KV_HEADS, HEAD_DIM)
    win_new = win_s.reshape(bd_, tn, 2, NSA_KV_HEADS, HEAD_DIM)
    win_out_s = jnp.concatenate([state_win[l], win_new], axis=1)[:, -win_rows:][None]
    conv_s = jnp.concatenate([state_dn_conv[l], dnqkv_s[:, None, :]], axis=1)[:, -(DN_CONV - 1):][None]

    return (y_p, y_s, kv_p, kv_s, win_out_p, win_out_s, conv_p, conv_s, s_p[None], s_s[None])
```

```python
import functools
import math

import numpy as np
import jax
import jax.numpy as jnp
from jax import lax
from jax.experimental import pallas as pl
from jax.experimental.pallas import tpu as pltpu

F32 = jnp.float32
BF16 = jnp.bfloat16

NSA_HEADS = 8
NSA_KV_HEADS = 2
NSA_GROUP = NSA_HEADS // NSA_KV_HEADS
HEAD_DIM = 64
ROPE_DIM = HEAD_DIM // 4
ROPE_HALF = ROPE_DIM // 2
ROPE_THETA = 500000.0
CMP_BLOCK = 32
CMP_STRIDE = 16
SEL_BLOCK = 64
SEL_SHIFT = 6
SEL_TOPN = 16
WINDOW = 512
FORCE_BONUS = 1000.0
DN_HEADS = 4
DN_HEAD_DIM = 128
DN_CONV = 4
DN_CHUNK = 64
N_GROUPS = 4
EXPERTS_PER_GROUP = 8
TOP_K = 2
EPS = 1e-6

LANES = 128
SUBLANES = 8
VMEM_LIMIT = 56 * 1024 * 1024
NEG = -1e30
Q_BLOCK = 128
SEL_CHUNK = 1024
DN_TB = 256
SHIFT_SLACK = 1.0 + 2.0 ** -5
SHIFT_MIN_SUM = 1e-20
HEAD_PERM = (0, 4, 1, 5, 2, 6, 3, 7)
SM_A, SM_B, SM_GATE = 0, DN_HEADS, 2 * DN_HEADS
EXPERT_LANE0 = 32


def _cparams(sem):
    return pltpu.CompilerParams(dimension_semantics=sem, vmem_limit_bytes=VMEM_LIMIT)


def _dot(a, b):
    return jnp.dot(a, b, preferred_element_type=F32)


def _dot_nt(a, b):
    return lax.dot_general(a, b, (((1,), (1,)), ((), ())), preferred_element_type=F32)


def _dot_tn(a, b):
    return lax.dot_general(a, b, (((0,), (0,)), ((), ())), preferred_element_type=F32)


def _split2(x):
    hi = x.astype(BF16)
    lo = (x - hi.astype(F32)).astype(BF16)
    return hi, lo


def _split3(x):
    hi = x.astype(BF16)
    r = x - hi.astype(F32)
    mid = r.astype(BF16)
    lo = (r - mid.astype(F32)).astype(BF16)
    return hi, mid, lo


def _dot3s(a, b):
    return _dot(a[0], b[0]) + (_dot(a[0], b[1]) + _dot(a[1], b[0]))


def _sigmoid(x):
    return 1.0 / (1.0 + jnp.exp(-x))


def _seg_sumsq(x, bd):
    hi, lo = _split2(x * x)
    return _dot(hi, bd) + _dot(lo, bd)


def _rope128(x, c, s1, s2):
    return x * c + pltpu.roll(x, LANES - ROPE_HALF, 1) * s1 + pltpu.roll(x, ROPE_HALF, 1) * s2


def _head_norm_rope(v, gain, bd, c, s1, s2):
    ss = _seg_sumsq(v, bd) * (1.0 / HEAD_DIM)
    return _rope128(v * lax.rsqrt(ss + EPS) * gain, c, s1, s2)


def _masked_softmax(s, valid):
    s = jnp.where(valid, s, NEG)
    m = jnp.max(s, axis=-1, keepdims=True)
    e = jnp.where(valid, jnp.exp(s - m), 0.0)
    l = jnp.sum(e, axis=-1, keepdims=True)
    return e * (1.0 / jnp.maximum(l, 1e-30))


def _topk_mask(score, k):
    lane = lax.broadcasted_iota(jnp.int32, score.shape, 1).astype(F32)
    big = float(score.shape[-1])

    def body(_, carry):
        sc, sel = carry
        m = jnp.max(sc, axis=-1, keepdims=True)
        idx = jnp.min(jnp.where(sc == m, lane, big), axis=-1, keepdims=True)
        hit = lane == idx
        sel = jnp.where(hit, jnp.maximum(sel, jnp.where(m > -jnp.inf, 1.0, 0.0)), sel)
        sc = jnp.where(hit, -jnp.inf, sc)
        return sc, sel

    _, sel = lax.fori_loop(0, k, body, (score, jnp.zeros(score.shape, F32)))
    return sel


def _topk_mask_rows(score, k):
    idx = lax.broadcasted_iota(jnp.int32, score.shape, 0).astype(F32)
    big = float(score.shape[0])

    def body(_, carry):
        sc, sel = carry
        m = jnp.max(sc, axis=0, keepdims=True)
        first = jnp.min(jnp.where(sc == m, idx, big), axis=0, keepdims=True)
        hit = idx == first
        sel = jnp.where(hit, jnp.maximum(sel, jnp.where(m > -jnp.inf, 1.0, 0.0)), sel)
        sc = jnp.where(hit, -jnp.inf, sc)
        return sc, sel

    _, sel = lax.fori_loop(0, k, body, (score, jnp.zeros(score.shape, F32)))
    return sel


def _proj_layout(dnw, d):
    sizes = dict(q=NSA_HEADS * HEAD_DIM, kvc=2 * LANES, ks=LANES, vs=LANES, kw=LANES, vw=LANES,
                 small=LANES, dnqkv=3 * dnw, dnz=dnw, mg=2 * d)
    offs, o = {}, 0
    for name, n in sizes.items():
        offs[name] = (o, o + n)
        o += n
    return offs, o


def _proj_kernel(x_ref, gin_ref, w_ref, c_ref, s1_ref, s2_ref, qg_ref, kg_ref, bd_ref,
                 q_out, rows_out, win_out, kvb_out, small_out, dnqkv_out, dnz_out, mg_out, *, offs):
    x = x_ref[...]
    ms = jnp.mean(x * x, axis=-1, keepdims=True)
    hb = ((x * lax.rsqrt(ms + EPS)) * gin_ref[...]).astype(BF16)

    def mm(name, lo=0, hi=None):
        a, b = offs[name]
        hi = b - a if hi is None else hi
        return _dot(hb, w_ref[:, a + lo:a + hi])

    c, s1, s2 = c_ref[...], s1_ref[...], s2_ref[...]
    bd = bd_ref[...]
    q_all = mm("q")
    for j in range(NSA_HEADS * HEAD_DIM // LANES):
        qj = _head_norm_rope(q_all[:, LANES * j:LANES * (j + 1)], qg_ref[...], bd, c, s1, s2)
        q_out[:, LANES * j:LANES * (j + 1)] = (qj * (HEAD_DIM ** -0.5)).astype(BF16)
    rows_out[:, 0:2 * LANES] = mm("kvc")
    five = _dot(hb, w_ref[:, offs["ks"][0]:offs["small"][1]])
    ks = _head_norm_rope(five[:, 0:LANES], kg_ref[0:1, :], bd, c, s1, s2)
    vs = five[:, LANES:2 * LANES]
    rows_out[:, 2 * LANES:3 * LANES] = ks
    rows_out[:, 3 * LANES:4 * LANES] = vs
    kw = _head_norm_rope(five[:, 2 * LANES:3 * LANES], kg_ref[1:2, :], bd, c, s1, s2)
    vw = five[:, 3 * LANES:4 * LANES]
    win_out[:, 0:LANES] = kw
    win_out[:, LANES:2 * LANES] = vw
    kvb_out[:, 0:LANES] = ks.astype(BF16)
    kvb_out[:, LANES:2 * LANES] = vs.astype(BF16)
    kvb_out[:, 2 * LANES:3 * LANES] = kw.astype(BF16)
    kvb_out[:, 3 * LANES:4 * LANES] = vw.astype(BF16)
    small_out[...] = five[:, 4 * LANES:5 * LANES]
    dnqkv_out[...] = mm("dnqkv")
    dnz_out[...] = mm("dnz")
    mg_out[...] = _sigmoid(mm("mg"))


def _proj_stage(x2d, pw, tabs, tm, n_tab_blocks):
    m, d = x2d.shape
    offs, ncols = pw["offs"], pw["ncols"]
    dnw = offs["dnz"][1] - offs["dnz"][0]
    row = lambda w: pl.BlockSpec((tm, w), lambda i: (i, 0))
    full = lambda a: pl.BlockSpec(a.shape, lambda i: (0,) * a.ndim)
    tab = pl.BlockSpec((tm, LANES), lambda i: (i % n_tab_blocks, 0))
    out_shape = (
        jax.ShapeDtypeStruct((m, 4 * LANES), BF16),
        jax.ShapeDtypeStruct((m, 4 * LANES), F32),
        jax.ShapeDtypeStruct((m, 2 * LANES), F32),
        jax.ShapeDtypeStruct((m, 4 * LANES), BF16),
        jax.ShapeDtypeStruct((m, LANES), F32),
        jax.ShapeDtypeStruct((m, 3 * dnw), F32),
        jax.ShapeDtypeStruct((m, dnw), F32),
        jax.ShapeDtypeStruct((m, 2 * d), F32),
    )
    return pl.pallas_call(
        functools.partial(_proj_kernel, offs=offs),
        grid=(m // tm,),
        in_specs=[row(d), full(pw["gin"]), full(pw["w"]), tab, tab, tab,
                  full(pw["qg"]), full(pw["kg"]), full(pw["bd"])],
        out_specs=tuple(row(s.shape[1]) for s in out_shape),
        out_shape=out_shape,
        compiler_params=_cparams(("parallel",)),
    )(x2d, pw["gin"], pw["w"], tabs[0], tabs[1], tabs[2], pw["qg"], pw["kg"], pw["bd"])


def _compress_body(xk_ref, xv_ref, w_ref, p_ref, kg_ref, bd_ref, c_ref, s1_ref, s2_ref, out_ref):
    nb = xk_ref.shape[0] // CMP_STRIDE
    acc = [jnp.zeros((nb, LANES), F32) for _ in range(4)]
    for l in range(CMP_STRIDE):
        xs = (xk_ref[pl.ds(l, nb, stride=CMP_STRIDE), :], xv_ref[pl.ds(l, nb, stride=CMP_STRIDE), :])
        for j in range(4):
            acc[j] = acc[j] + _dot((xs[j // 2] + p_ref[j, l:l + 1, :]).astype(BF16), w_ref[j, l])
    kraw = acc[0] + pltpu.roll(acc[1], nb - 1, 0)
    vraw = acc[2] + pltpu.roll(acc[3], nb - 1, 0)
    kc = _head_norm_rope(kraw, kg_ref[...], bd_ref[...], c_ref[...], s1_ref[...], s2_ref[...])
    out_ref[:, 0:LANES] = kc.astype(BF16)
    out_ref[:, LANES:2 * LANES] = vraw.astype(BF16)


def _compress_prompt_kernel(xk_ref, xv_ref, *rest):
    _compress_body(xk_ref, xv_ref, *rest)


def _compress_prompt(rows3, cw, tabs):
    b, t, _ = rows3.shape
    nb = t // CMP_STRIDE
    full = lambda a: pl.BlockSpec(a.shape, lambda i: (0,) * a.ndim)
    consts = (cw["w"], cw["p"], cw["kg"], cw["bd"]) + tuple(tabs)
    return pl.pallas_call(
        _compress_prompt_kernel,
        grid=(b,),
        in_specs=[pl.BlockSpec((None, t, LANES), lambda i: (i, 0, 0)),
                  pl.BlockSpec((None, t, LANES), lambda i: (i, 0, 1))] + [full(a) for a in consts],
        out_specs=pl.BlockSpec((None, nb, 2 * LANES), lambda i: (i, 0, 0)),
        out_shape=jax.ShapeDtypeStruct((b, nb, 2 * LANES), BF16),
        compiler_params=_cparams(("parallel",)),
    )(rows3, rows3, *consts)


def _wait_all(buf_view, sem):
    pltpu.make_async_copy(buf_view, buf_view, sem).wait()


def _start_pages(pt_ref, cache_ref, buf_ref, sem, b, slot, n_pages, page, lane0):
    def start(p, c):
        for j in range(2):
            pltpu.make_async_copy(
                cache_ref.at[pt_ref[b, p], :, pl.ds(lane0 + j * LANES, LANES)],
                buf_ref.at[slot, j, pl.ds(pl.multiple_of(p * page, page), page), :], sem.at[slot]).start()
        return c

    lax.fori_loop(0, n_pages, start, 0)


def _prefetch_pages(pt_ref, cache_ref, buf_ref, sem, n_pages, page, lane0):
    b = pl.program_id(0)
    slot = b % 2

    @pl.when(b == 0)
    def _():
        _start_pages(pt_ref, cache_ref, buf_ref, sem, 0, 0, n_pages, page, lane0)

    @pl.when(b + 1 < pl.num_programs(0))
    def _():
        _start_pages(pt_ref, cache_ref, buf_ref, sem, b + 1, 1 - slot, n_pages, page, lane0)

    return slot


def _compress_sample_kernel(pt_ref, cache_ref, *rest, n_pages, page):
    *consts, out_ref, buf_ref, sem = rest
    slot = _prefetch_pages(pt_ref, cache_ref, buf_ref, sem, n_pages, page, 0)
    _wait_all(buf_ref.at[slot], sem.at[slot])
    _compress_body(buf_ref.at[slot, 0], buf_ref.at[slot, 1], *consts, out_ref)


def _compress_sample(cache3, page_table, cw, tabs):
    bd_, n_pages = page_table.shape
    page = cache3.shape[1]
    past = n_pages * page
    nb = past // CMP_STRIDE
    full = lambda a: pl.BlockSpec(a.shape, lambda i, pt: (0,) * a.ndim)
    consts = (cw["w"], cw["p"], cw["kg"], cw["bd"]) + tuple(tabs)
    grid_spec = pltpu.PrefetchScalarGridSpec(
        num_scalar_prefetch=1,
        grid=(bd_,),
        in_specs=[pl.BlockSpec(memory_space=pl.ANY)] + [full(a) for a in consts],
        out_specs=pl.BlockSpec((None, nb, 2 * LANES), lambda i, pt: (i, 0, 0)),
        scratch_shapes=[pltpu.VMEM((2, 2, past, LANES), F32), pltpu.SemaphoreType.DMA((2,))],
    )
    return pl.pallas_call(
        functools.partial(_compress_sample_kernel, n_pages=n_pages, page=page),
        grid_spec=grid_spec,
        out_shape=jax.ShapeDtypeStruct((bd_, nb, 2 * LANES), BF16),
        compiler_params=_cparams(("arbitrary",)),
    )(page_table, cache3, *consts)


def _stack_heads(q, lo_half):
    zero = jnp.zeros((), q.dtype)
    parts = []
    for g in range(NSA_KV_HEADS):
        for r in range(NSA_GROUP):
            col = q[:, LANES * r:LANES * (r + 1)]
            parts.append(jnp.where(lo_half if g == 0 else jnp.logical_not(lo_half), col, zero))
    return jnp.concatenate(parts, axis=0)


def _sel_scores(imp, pos_q, blk):
    cur = pos_q >> SEL_SHIFT
    readable = blk * SEL_BLOCK <= pos_q
    forced = jnp.logical_or(blk == 0, jnp.logical_or(blk == cur, blk == cur - 1))
    return jnp.where(readable, imp + jnp.where(forced, FORCE_BONUS, 0.0), -jnp.inf)


def _attn_prompt_kernel(q_ref, small_ref, kvb_ref, kcv_ref, ovl_ref, bd_ref, o_ref, kmx_ref, os_ref, *, n_cmp):
    qb = Q_BLOCK
    m8 = NSA_HEADS * qb
    i = pl.program_id(1)
    q0 = i * qb

    @pl.when(i == 0)
    def _():
        def body(c, mx):
            kf = kvb_ref[pl.ds(pl.multiple_of(c * SEL_CHUNK, SEL_CHUNK), SEL_CHUNK), 0:LANES].astype(F32)
            return jnp.maximum(mx, jnp.max(_seg_sumsq(kf, bd_ref[...]), axis=0, keepdims=True))
        kmx_ref[...] = lax.fori_loop(0, kvb_ref.shape[0] // SEL_CHUNK, body, jnp.zeros((1, LANES), F32))

    lane = lax.broadcasted_iota(jnp.int32, (qb, LANES), 1)
    lo_half = lane < HEAD_DIM
    q8 = _stack_heads(q_ref[...], lo_half)

    def qpos(n):
        return q0 + (lax.broadcasted_iota(jnp.int32, (m8, n), 0) & (qb - 1))

    def kidx(n):
        return lax.broadcasted_iota(jnp.int32, (m8, n), 1)

    nb = kcv_ref.shape[0]
    s = _dot_nt(q8, kcv_ref[:, 0:LANES])
    cidx = kidx(nb)
    valid = jnp.logical_and(cidx < n_cmp, cidx * CMP_STRIDE + (CMP_BLOCK - 1) <= qpos(nb))
    p = _masked_softmax(s, valid).astype(BF16)
    o_c = _dot(p, kcv_ref[:, LANES:2 * LANES])
    imp8 = _dot(p, ovl_ref[...])

    ns = ovl_ref.shape[1]
    imps = []
    for g in range(NSA_KV_HEADS):
        acc = imp8[(g * NSA_GROUP) * qb:(g * NSA_GROUP + 1) * qb]
        for r in range(1, NSA_GROUP):
            acc = acc + imp8[(g * NSA_GROUP + r) * qb:(g * NSA_GROUP + r + 1) * qb]
        imps.append(acc)
    imp = jnp.concatenate(imps, axis=0)
    pos2 = q0 + (lax.broadcasted_iota(jnp.int32, (2 * qb, ns), 0) & (qb - 1))
    blk2 = lax.broadcasted_iota(jnp.int32, (2 * qb, ns), 1)
    sel = jnp.transpose(_topk_mask_rows(jnp.transpose(_sel_scores(imp, pos2, blk2)), SEL_TOPN))
    sel8 = jnp.concatenate([sel[0:qb]] * NSA_GROUP + [sel[qb:2 * qb]] * NSA_GROUP, axis=0) > 0.0

    ck = SEL_CHUNK
    qpos_k = qpos(ck)
    kk = lax.broadcasted_iota(jnp.int32, (ck, ns), 0)
    ss = lax.broadcasted_iota(jnp.int32, (ck, ns), 1)
    n_chunks = (q0 + qb + ck - 1) // ck

    def scores(qaug, c, causal):
        k0 = pl.multiple_of(c * ck, ck)
        e = jnp.where(((k0 + kk) >> SEL_SHIFT) == ss, 1.0, 0.0).astype(BF16)
        sc = _dot_nt(qaug, jnp.concatenate([kvb_ref[pl.ds(k0, ck), 0:LANES], e], axis=1))
        if causal:
            sc = jnp.where(k0 + kidx(ck) <= qpos_k, sc, NEG)
        return sc, kvb_ref[pl.ds(k0, ck), LANES:2 * LANES]

    q8f = q8.astype(F32)
    qn2 = jnp.sum(q8f * q8f, axis=-1, keepdims=True)
    kmx = kmx_ref[...]
    half = NSA_GROUP * qb
    kq2 = jnp.concatenate([qn2[0:half] * kmx[:, 0:1], qn2[half:] * kmx[:, HEAD_DIM:HEAD_DIM + 1]], axis=0)
    shift = jnp.sqrt(kq2) * SHIFT_SLACK + 1e-6
    qaug = jnp.concatenate([q8, jnp.where(sel8, -shift, NEG).astype(BF16)], axis=1)

    def chunk_shifted(c, carry, causal):
        l, acc = carry
        sc, vmat = scores(qaug, c, causal)
        pe = jnp.exp(sc)
        return l + jnp.sum(pe, axis=-1, keepdims=True), acc + _dot(pe.astype(BF16), vmat)

    carry = lax.fori_loop(0, n_chunks - 1, functools.partial(chunk_shifted, causal=False),
                          (jnp.zeros((m8, 1), F32), jnp.zeros((m8, LANES), F32)))
    l_s, acc_s = chunk_shifted(n_chunks - 1, carry, True)
    os_ref[...] = acc_s * (1.0 / jnp.maximum(l_s, 1e-37))

    @pl.when(jnp.min(l_s) < SHIFT_MIN_SUM)
    def _():
        qaug_x = jnp.concatenate([q8, jnp.where(sel8, 0.0, NEG).astype(BF16)], axis=1)

        def chunk_online(c, carry, causal):
            m, l, acc = carry
            sc, vmat = scores(qaug_x, c, causal)
            m_new = jnp.maximum(m, jnp.max(sc, axis=-1, keepdims=True))
            alpha = jnp.exp(m - m_new)
            pe = jnp.exp(sc - m_new)
            return (m_new, alpha * l + jnp.sum(pe, axis=-1, keepdims=True),
                    alpha * acc + _dot(pe.astype(BF16), vmat))

        init = (jnp.full((m8, 1), NEG, F32), jnp.zeros((m8, 1), F32), jnp.zeros((m8, LANES), F32))
        carry_x = lax.fori_loop(0, n_chunks - 1, functools.partial(chunk_online, causal=False), init)
        _, l_x, acc_x = chunk_online(n_chunks - 1, carry_x, True)
        os_ref[...] = acc_x * (1.0 / l_x)

    o_s = os_ref[...]

    wk = WINDOW + qb
    ws = pl.multiple_of(jnp.maximum(q0 - WINDOW, 0), qb)
    s = _dot_nt(q8, kvb_ref[pl.ds(ws, wk), 2 * LANES:3 * LANES])
    dpos = qpos(wk) - (ws + kidx(wk))
    p = _masked_softmax(s, jnp.logical_and(dpos >= 0, dpos < WINDOW)).astype(BF16)
    o_w = _dot(p, kvb_ref[pl.ds(ws, wk), 3 * LANES:4 * LANES])

    sig = _sigmoid(small_ref[...])
    for r in range(NSA_GROUP):
        outs = []
        for g in range(NSA_KV_HEADS):
            h = g * NSA_GROUP + r
            rows = slice(h * qb, (h + 1) * qb)
            gl = SM_GATE + 3 * h
            outs.append(sig[:, gl:gl + 1] * o_c[rows] + sig[:, gl + 1:gl + 2] * o_s[rows]
                        + sig[:, gl + 2:gl + 3] * o_w[rows])
        o_ref[:, LANES * r:LANES * (r + 1)] = jnp.where(lo_half, outs[0], outs[1]).astype(BF16)


def _attn_prompt(q2d, small2d, kvb3, kcv3, ovl, n_cmp):
    b, t, _ = kvb3.shape
    nq = t // Q_BLOCK
    nb = kcv3.shape[1]
    bd = _block_diag_ones()
    return pl.pallas_call(
        functools.partial(_attn_prompt_kernel, n_cmp=n_cmp),
        grid=(b, nq),
        in_specs=[pl.BlockSpec((Q_BLOCK, 4 * LANES), lambda bi, i: (bi * nq + i, 0)),
                  pl.BlockSpec((Q_BLOCK, LANES), lambda bi, i: (bi * nq + i, 0)),
                  pl.BlockSpec((None, t, 4 * LANES), lambda bi, i: (bi, 0, 0)),
                  pl.BlockSpec((None, nb, 2 * LANES), lambda bi, i: (bi, 0, 0)),
                  pl.BlockSpec(ovl.shape, lambda bi, i: (0, 0)),
                  pl.BlockSpec(bd.shape, lambda bi, i: (0, 0))],
        out_specs=pl.BlockSpec((Q_BLOCK, 4 * LANES), lambda bi, i: (bi * nq + i, 0)),
        out_shape=jax.ShapeDtypeStruct((b * t, 4 * LANES), BF16),
        scratch_shapes=[pltpu.VMEM((1, LANES), F32), pltpu.VMEM((NSA_HEADS * Q_BLOCK, LANES), F32)],
        compiler_params=_cparams(("parallel", "arbitrary")),
    )(q2d, small2d, kvb3, kcv3, ovl, bd)


def _attn_sample_kernel(pt_ref, cache_ref, q_ref, small_ref, rows_ref, win_ref, kcv_ref, swin_ref, ovl_ref,
                        o_ref, buf_ref, sem, *, n_pages, page, n_cmp, n_sel, key_chunk):
    past = n_pages * page
    slot = _prefetch_pages(pt_ref, cache_ref, buf_ref, sem, n_pages, page, 2 * LANES)
    lane1 = lax.broadcasted_iota(jnp.int32, (1, LANES), 1)
    lo1 = lane1 < HEAD_DIM
    q8f = _stack_heads(q_ref[...].astype(F32), lo1)
    q8 = q8f.astype(BF16)
    nh = NSA_HEADS

    nb = kcv_ref.shape[0]
    s = _dot_nt(q8, kcv_ref[:, 0:LANES])
    p = _masked_softmax(s, lax.broadcasted_iota(jnp.int32, (nh, nb), 1) < n_cmp).astype(BF16)
    o_c = _dot(p, kcv_ref[:, LANES:2 * LANES])
    imp8 = _dot(p, ovl_ref[...])
    nsp = ovl_ref.shape[1]
    blk = lax.broadcasted_iota(jnp.int32, (1, nsp), 1)
    selbs = []
    for g in range(NSA_KV_HEADS):
        imp = jnp.sum(imp8[g * NSA_GROUP:(g + 1) * NSA_GROUP], axis=0, keepdims=True)
        score = jnp.where(blk < n_sel, _sel_scores(imp, past, blk), -jnp.inf)
        selb = jnp.where(_topk_mask(score, SEL_TOPN) > 0.0, 0.0, NEG)
        selbs += [selb] * NSA_GROUP
    selb8 = jnp.concatenate(selbs, axis=0)
    selb8_b = selb8.astype(BF16)

    _wait_all(buf_ref.at[slot], sem.at[slot])
    ks_new = rows_ref[:, 2 * LANES:3 * LANES].astype(BF16).astype(F32)
    vs_new = rows_ref[:, 3 * LANES:4 * LANES].astype(BF16).astype(F32)
    s_new = jnp.sum(q8f * ks_new, axis=-1, keepdims=True) + selb8[:, n_sel - 1:n_sel]
    ck = key_chunk
    n_ck = past // ck
    ss = lax.broadcasted_iota(jnp.int32, (nsp, ck), 0)
    kk = lax.broadcasted_iota(jnp.int32, (nsp, ck), 1)
    scores = []
    for c in range(n_ck):
        kmat = buf_ref[slot, 0, c * ck:(c + 1) * ck, :].astype(BF16)
        e = jnp.where(((c * ck + kk) >> SEL_SHIFT) == ss, 1.0, 0.0).astype(BF16)
        scores.append(_dot_nt(q8, kmat) + _dot(selb8_b, e))
    m = s_new
    for sc in scores:
        m = jnp.maximum(m, jnp.max(sc, axis=-1, keepdims=True))
    p_new = jnp.exp(s_new - m)
    l = p_new
    acc = p_new * vs_new
    for c, sc in enumerate(scores):
        pe = jnp.exp(sc - m)
        l = l + jnp.sum(pe, axis=-1, keepdims=True)
        acc = acc + _dot(pe.astype(BF16), buf_ref[slot, 1, c * ck:(c + 1) * ck, :].astype(BF16))
    o_s = acc * (1.0 / l)

    wr = swin_ref.shape[0]
    s = _dot_nt(q8, swin_ref[:, 0:LANES].astype(BF16))
    j = lax.broadcasted_iota(jnp.int32, (nh, wr), 1)
    valid = wr - j < WINDOW
    kw_new = win_ref[:, 0:LANES].astype(BF16).astype(F32)
    vw_new = win_ref[:, LANES:2 * LANES].astype(BF16).astype(F32)
    s_new = jnp.sum(q8f * kw_new, axis=-1, keepdims=True)
    s = jnp.where(valid, s, NEG)
    m = jnp.maximum(s_new, jnp.max(s, axis=-1, keepdims=True))
    pe = jnp.where(valid, jnp.exp(s - m), 0.0)
    p_new = jnp.exp(s_new - m)
    l = p_new + jnp.sum(pe, axis=-1, keepdims=True)
    o_w = (p_new * vw_new + _dot(pe.astype(BF16), swin_ref[:, LANES:2 * LANES].astype(BF16))) * (1.0 / l)

    sig = _sigmoid(small_ref[...])
    for r in range(NSA_GROUP):
        outs = []
        for g in range(NSA_KV_HEADS):
            h = g * NSA_GROUP + r
            gl = SM_GATE + 3 * h
            outs.append(sig[:, gl:gl + 1] * o_c[h:h + 1] + sig[:, gl + 1:gl + 2] * o_s[h:h + 1]
                        + sig[:, gl + 2:gl + 3] * o_w[h:h + 1])
        o_ref[:, LANES * r:LANES * (r + 1)] = jnp.where(lo1, outs[0], outs[1]).astype(BF16)


def _attn_sample(cache3, page_table, q2d, small2d, rows2d, win2d, kcv3, swin3, ovl, n_cmp, n_sel):
    bd_, n_pages = page_table.shape
    page = cache3.shape[1]
    past = n_pages * page
    key_chunk = math.gcd(past, 2048)
    per_seq = lambda a: pl.BlockSpec((None,) + a.shape[1:], lambda i, pt: (i,) + (0,) * (a.ndim - 1))
    q3, small3, rows3, win3 = (a[:, None, :] for a in (q2d, small2d, rows2d, win2d))
    grid_spec = pltpu.PrefetchScalarGridSpec(
        num_scalar_prefetch=1,
        grid=(bd_,),
        in_specs=[pl.BlockSpec(memory_space=pl.ANY), per_seq(q3), per_seq(small3), per_seq(rows3), per_seq(win3),
                  per_seq(kcv3), per_seq(swin3), pl.BlockSpec(ovl.shape, lambda i, pt: (0, 0))],
        out_specs=pl.BlockSpec((None, 1, 4 * LANES), lambda i, pt: (i, 0, 0)),
        scratch_shapes=[pltpu.VMEM((2, 2, past, LANES), F32), pltpu.SemaphoreType.DMA((2,))],
    )
    out = pl.pallas_call(
        functools.partial(_attn_sample_kernel, n_pages=n_pages, page=page, n_cmp=n_cmp, n_sel=n_sel,
                          key_chunk=key_chunk),
        grid_spec=grid_spec,
        out_shape=jax.ShapeDtypeStruct((bd_, 1, 4 * LANES), BF16),
        compiler_params=_cparams(("arbitrary",)),
    )(page_table, cache3, q3, small3, rows3, win3, kcv3, swin3, ovl)
    return out[:, 0, :]


def _dn_kernel(x_ref, small_ref, z_ref, cw_ref, c0_ref, s0_ref, alog_ref, dtb_ref, gn_ref,
               o_ref, sout_ref, s_scr, xs_scr, *, chunk, tb, t_valid):
    j = pl.program_id(1)
    dnw = z_ref.shape[1]
    hd = DN_HEAD_DIM
    hist = SUBLANES

    @pl.when(j == 0)
    def _():
        s_scr[...] = s0_ref[...]
        xs_scr[0:hist, :] = c0_ref[...]

    xs_scr[hist:hist + tb, :] = x_ref[...]
    conv = xs_scr[pl.ds(hist - (DN_CONV - 1), tb), :] * cw_ref[0:1, :]
    for jj in range(1, DN_CONV):
        conv = conv + xs_scr[pl.ds(hist - (DN_CONV - 1) + jj, tb), :] * cw_ref[jj:jj + 1, :]
    xs_scr[0:hist, :] = xs_scr[tb:tb + hist, :]
    act = conv * _sigmoid(conv)

    small = small_ref[...]
    tpos = j * tb + lax.broadcasted_iota(jnp.int32, (tb, LANES), 0)
    live = tpos < t_valid
    xg = small + dtb_ref[...]
    softplus = jnp.maximum(xg, 0.0) + jnp.log(1.0 + jnp.exp(-jnp.abs(xg)))
    g_all = jnp.where(live, -jnp.exp(alog_ref[...]) * softplus, 0.0)
    beta_all = jnp.where(live, _sigmoid(small), 0.0)

    ri = lax.broadcasted_iota(jnp.int32, (chunk, chunk), 0)
    ci = lax.broadcasted_iota(jnp.int32, (chunk, chunk), 1)
    incl = ri >= ci
    strict = ri > ci
    ltri = jnp.where(incl, 1.0, 0.0).astype(BF16)
    eye = jnp.where(ri == ci, 1.0, 0.0)
    lane_c = lax.broadcasted_iota(jnp.int32, (chunk, LANES), 1)
    n_sq = max(1, int(math.ceil(math.log2(chunk))))

    n_chunks = tb // chunk
    items = [(c, h) for c in range(n_chunks) for h in range(DN_HEADS)]

    gc_alls, gc3s = [], []
    for c in range(n_chunks):
        g3 = _split3(g_all[c * chunk:(c + 1) * chunk])
        gc_all = _dot(ltri, g3[0]) + (_dot(ltri, g3[1]) + _dot(ltri, g3[2]))
        gc_alls.append(gc_all)
        gc3s.append(_split3(gc_all))

    prep = []
    for c, h in items:
        rs = slice(c * chunk, (c + 1) * chunk)
        q = act[rs, h * hd:(h + 1) * hd]
        k = act[rs, dnw + h * hd:dnw + (h + 1) * hd]
        v = act[rs, 2 * dnw + h * hd:2 * dnw + (h + 1) * hd]
        q = q * lax.rsqrt(jnp.sum(q * q, axis=-1, keepdims=True) + EPS) * (hd ** -0.5)
        k = k * lax.rsqrt(jnp.sum(k * k, axis=-1, keepdims=True) + EPS)
        beta = beta_all[rs, SM_B + h:SM_B + h + 1]
        gc = gc_alls[c][:, SM_A + h:SM_A + h + 1]
        pick = jnp.where(lane_c == SM_A + h, 1.0, 0.0).astype(BF16)
        g3 = gc3s[c]
        gc_row = _dot_nt(pick, g3[0]) + (_dot_nt(pick, g3[1]) + _dot_nt(pick, g3[2]))
        dmask = jnp.where(incl, jnp.exp(jnp.where(incl, gc - gc_row, 0.0)), 0.0)
        kb = k * beta
        kbf = k.astype(BF16)
        a_strict = jnp.where(strict, _dot_nt(kb.astype(BF16), kbf) * dmask, 0.0)
        egc = jnp.exp(gc)
        g_last = gc[chunk - 1:chunk, :]
        prep.append(dict(
            vb=(v * beta).astype(BF16), kbg=(kb * egc).astype(BF16), qg=(q * egc).astype(BF16),
            qk=(_dot_nt(q.astype(BF16), kbf) * dmask).astype(BF16),
            kd=(k * jnp.exp(g_last - gc)).astype(BF16), decay=jnp.exp(g_last), n=-a_strict))

    tinv = [eye + p["n"] for p in prep]
    nsp = [_split2(p["n"]) for p in prep]
    for _ in range(n_sq - 1):
        nsp = [_split2(_dot3s(n, n)) for n in nsp]
        tinv = [t + _dot3s(_split2(t), n) for t, n in zip(tinv, nsp)]
    us, ws = [], []
    for p, t in zip(prep, tinv):
        tb16 = t.astype(BF16)
        us.append(_dot(tb16, p["vb"]))
        ws.append(_dot(tb16, p["kbg"]).astype(BF16))

    state = [s_scr[h] for h in range(DN_HEADS)]
    for i, (c, h) in enumerate(items):
        p = prep[i]
        rs = slice(c * chunk, (c + 1) * chunk)
        s_b = state[h].astype(BF16)
        v_new = (us[i] - _dot(ws[i], s_b)).astype(BF16)
        o = _dot(p["qg"], s_b) + _dot(p["qk"], v_new)
        state[h] = state[h] * p["decay"] + _dot_tn(p["kd"], v_new)
        on = o * lax.rsqrt(jnp.mean(o * o, axis=-1, keepdims=True) + EPS) * gn_ref[...]
        z = z_ref[rs, h * hd:(h + 1) * hd]
        o_ref[rs, h * hd:(h + 1) * hd] = (on * (z * _sigmoid(z))).astype(BF16)
    for h in range(DN_HEADS):
        s_scr[h] = state[h]

    @pl.when(j == pl.num_programs(1) - 1)
    def _():
        sout_ref[...] = s_scr[...]


def _dn_stage(x3, small3, z3, conv0, s0, dw, chunk, tb, t_valid):
    b, tpad, w3 = x3.shape
    dnw = w3 // 3
    nblk = tpad // tb
    full = lambda a: pl.BlockSpec(a.shape, lambda bi, j: (0,) * a.ndim)
    tok = lambda w: pl.BlockSpec((None, tb, w), lambda bi, j: (bi, j, 0))
    o, s_out = pl.pallas_call(
        functools.partial(_dn_kernel, chunk=chunk, tb=tb, t_valid=t_valid),
        grid=(b, nblk),
        in_specs=[tok(w3), tok(LANES), tok(dnw), full(dw["cw"]),
                  pl.BlockSpec((None, SUBLANES, w3), lambda bi, j: (bi, 0, 0)),
                  pl.BlockSpec((None, DN_HEADS, DN_HEAD_DIM, DN_HEAD_DIM), lambda bi, j: (bi, 0, 0, 0)),
                  full(dw["alog"]), full(dw["dtb"]), full(dw["gn"])],
        out_specs=(tok(dnw),
                   pl.BlockSpec((None, DN_HEADS, DN_HEAD_DIM, DN_HEAD_DIM), lambda bi, j: (bi, 0, 0, 0))),
        out_shape=(jax.ShapeDtypeStruct((b, tpad, dnw), BF16),
                   jax.ShapeDtypeStruct((b, DN_HEADS, DN_HEAD_DIM, DN_HEAD_DIM), F32)),
        scratch_shapes=[pltpu.VMEM((DN_HEADS, DN_HEAD_DIM, DN_HEAD_DIM), F32),
                        pltpu.VMEM((tb + SUBLANES, w3), F32)],
        compiler_params=_cparams(("parallel", "arbitrary")),
    )(x3, small3, z3, dw["cw"], conv0, s0, dw["alog"], dw["dtb"], dw["gn"])
    return o, s_out


def _merge_kernel(x_ref, oa_ref, ob_ref, mg_ref, wpa_ref, wpb_ref, wout_ref, gf_ref, wr_ref, br_ref,
                  x1_ref, h_ref, route_ref):
    d = x_ref.shape[1]
    mixed = mg_ref[:, 0:d] * _dot(oa_ref[...], wpa_ref[...]) + mg_ref[:, d:2 * d] * _dot(ob_ref[...], wpb_ref[...])
    x1 = x_ref[...] + _dot(mixed.astype(BF16), wout_ref[...])
    x1_ref[...] = x1
    h = (x1 * lax.rsqrt(jnp.mean(x1 * x1, axis=-1, keepdims=True) + EPS)) * gf_ref[...]
    h_ref[...] = h
    logits = _dot(h.astype(BF16), wr_ref[...]) + br_ref[...]
    lane = lax.broadcasted_iota(jnp.int32, logits.shape, 1)
    lanef = lane.astype(F32)
    big = float(LANES)
    is_g = lane < N_GROUPS
    lg = jnp.where(is_g, logits, NEG)
    eg = jnp.where(is_g, jnp.exp(lg - jnp.max(lg, axis=-1, keepdims=True)), 0.0)
    pg = eg / jnp.sum(eg, axis=-1, keepdims=True)
    p_top = jnp.max(pg, axis=-1, keepdims=True)
    grp = jnp.min(jnp.where(jnp.logical_and(is_g, pg == p_top), lanef, big), axis=-1, keepdims=True)
    e_lo = EXPERT_LANE0 + grp * EXPERTS_PER_GROUP
    in_grp = jnp.logical_and(lanef >= e_lo, lanef < e_lo + EXPERTS_PER_GROUP)
    le = jnp.where(in_grp, logits, -jnp.inf)
    v0 = jnp.max(le, axis=-1, keepdims=True)
    i0 = jnp.min(jnp.where(le == v0, lanef, big), axis=-1, keepdims=True)
    le1 = jnp.where(lanef == i0, -jnp.inf, le)
    v1 = jnp.max(le1, axis=-1, keepdims=True)
    i1 = jnp.min(jnp.where(le1 == v1, lanef, big), axis=-1, keepdims=True)
    e1 = jnp.exp(v1 - v0)
    den = 1.0 + e1
    w0 = p_top * (1.0 / den)
    w1 = p_top * (e1 / den)
    route = jnp.where(lane == 0, w0, jnp.where(lane == 1, w1, jnp.where(
        lane == 2, i0 - EXPERT_LANE0, jnp.where(lane == 3, i1 - EXPERT_LANE0, 0.0))))
    route_ref[...] = route


def _merge_stage(x2d, oa, ob, mg, mw, tm):
    m, d = x2d.shape
    row = lambda w: pl.BlockSpec((tm, w), lambda i: (i, 0))
    full = lambda a: pl.BlockSpec(a.shape, lambda i: (0,) * a.ndim)
    consts = (mw["wpa"], mw["wpb"], mw["wout"], mw["gf"], mw["wr"], mw["br"])
    return pl.pallas_call(
        _merge_kernel,
        grid=(m // tm,),
        in_specs=[row(d), row(oa.shape[1]), row(ob.shape[1]), row(2 * d)] + [full(a) for a in consts],
        out_specs=(row(d), row(d), row(LANES)),
        out_shape=(jax.ShapeDtypeStruct((m, d), F32), jax.ShapeDtypeStruct((m, d), F32),
                   jax.ShapeDtypeStruct((m, LANES), F32)),
        compiler_params=_cparams(("parallel",)),
    )(x2d, oa, ob, mg, *consts)


def _row_copy(src_ref, dst_ref, sem, src_row, dst_row):
    return pltpu.make_async_copy(src_ref.at[pl.ds(src_row, 1), :], dst_ref.at[pl.ds(dst_row, 1), :], sem)


def _expert_kernel(be_ref, lo_ref, tok_ref, nblk_ref, h_ref, wg_ref, wu_ref, wd_ref, y_ref, buf_ref, sem, *, blk):
    i = pl.program_id(0)
    n_used = nblk_ref[0]
    slot = i % 2

    @pl.when(i == 0)
    def _():
        def start(r, c):
            _row_copy(h_ref, buf_ref.at[0], sem.at[0], tok_ref[lo_ref[0] + r], r).start()
            return c
        lax.fori_loop(0, blk, start, 0, unroll=8)

    def step(prefetch):
        _wait_all(buf_ref.at[slot], sem.at[slot])
        if prefetch:
            lo = lo_ref[i + 1]
            for r in range(blk):
                _row_copy(h_ref, buf_ref.at[1 - slot], sem.at[1 - slot], tok_ref[lo + r], r).start()
        xb = buf_ref[slot].astype(BF16)
        gate = _dot(xb, wg_ref[...])
        up = _dot(xb, wu_ref[...])
        mid = (gate * _sigmoid(gate)) * up
        y_ref[...] = _dot(mid.astype(BF16), wd_ref[...])

    @pl.when(i + 1 < n_used)
    def _():
        step(True)

    @pl.when(i + 1 == n_used)
    def _():
        step(False)

    @pl.when(i >= n_used)
    def _():
        y_ref[...] = jnp.zeros(y_ref.shape, F32)


def _expert_stage(h2d, blk_expert, blk_lo, s_tok, n_used, ew, blk):
    m, d = h2d.shape
    n_blocks = blk_expert.shape[0]
    de = ew["wg"].shape[2]
    wspec = lambda s: pl.BlockSpec((None,) + s, lambda i, be, lo, tok, nb: (be[i], 0, 0))
    grid_spec = pltpu.PrefetchScalarGridSpec(
        num_scalar_prefetch=4,
        grid=(n_blocks,),
        in_specs=[pl.BlockSpec(memory_space=pl.ANY), wspec((d, de)), wspec((d, de)), wspec((de, d))],
        out_specs=pl.BlockSpec((blk, d), lambda i, be, lo, tok, nb: (i, 0)),
        scratch_shapes=[pltpu.VMEM((2, blk, d), F32), pltpu.SemaphoreType.DMA((2,))],
    )
    return pl.pallas_call(
        functools.partial(_expert_kernel, blk=blk),
        grid_spec=grid_spec,
        out_shape=jax.ShapeDtypeStruct((n_blocks * blk, d), F32),
        compiler_params=_cparams(("arbitrary",)),
    )(blk_expert, blk_lo, s_tok, n_used, h2d, ew["wg"], ew["wu"], ew["wd"])


def _combine_kernel(pos_ref, x1_ref, route_ref, y_ref, out_ref, buf_ref, sem, *, tm):
    i = pl.program_id(0)
    slot = i % 2

    @pl.when(i == 0)
    def _():
        def start(r, c):
            for k in range(TOP_K):
                _row_copy(y_ref, buf_ref.at[0, k], sem.at[0], pos_ref[r * TOP_K + k], r).start()
            return c
        lax.fori_loop(0, tm, start, 0, unroll=4)

    _wait_all(buf_ref.at[slot], sem.at[slot])

    @pl.when(i + 1 < pl.num_programs(0))
    def _():
        for r in range(tm):
            for k in range(TOP_K):
                _row_copy(y_ref, buf_ref.at[1 - slot, k], sem.at[1 - slot],
                          pos_ref[((i + 1) * tm + r) * TOP_K + k], r).start()

    w = route_ref[...]
    out_ref[...] = x1_ref[...] + (buf_ref[slot, 0] * w[:, 0:1] + buf_ref[slot, 1] * w[:, 1:2])


def _combine_stage(x1, route, y_sorted, pos, tm):
    m, d = x1.shape
    grid_spec = pltpu.PrefetchScalarGridSpec(
        num_scalar_prefetch=1,
        grid=(m // tm,),
        in_specs=[pl.BlockSpec((tm, d), lambda i, p: (i, 0)), pl.BlockSpec((tm, LANES), lambda i, p: (i, 0)),
                  pl.BlockSpec(memory_space=pl.ANY)],
        out_specs=pl.BlockSpec((tm, d), lambda i, p: (i, 0)),
        scratch_shapes=[pltpu.VMEM((2, TOP_K, tm, d), F32), pltpu.SemaphoreType.DMA((2,))],
    )
    return pl.pallas_call(
        functools.partial(_combine_kernel, tm=tm),
        grid_spec=grid_spec,
        out_shape=jax.ShapeDtypeStruct((m, d), F32),
        compiler_params=_cparams(("arbitrary",)),
    )(pos, x1, route, y_sorted)


def _moe_stage(x1, h2d, route, ew, blk, tm):
    m, d = h2d.shape
    n_exp = ew["wg"].shape[0]
    n_assign = m * TOP_K
    i32 = jnp.int32
    flat_e = route[:, TOP_K:2 * TOP_K].astype(i32).reshape(n_assign)
    order = jnp.argsort(flat_e).astype(i32)
    rank = jnp.argsort(order).astype(i32)
    counts = jnp.sum(flat_e[None, :] == jnp.arange(n_exp, dtype=i32)[:, None], axis=1, dtype=i32)
    padded = (counts + blk - 1) // blk * blk
    pad_end = jnp.cumsum(padded)
    pad_start = pad_end - padded
    start = jnp.cumsum(counts) - counts
    n_blocks = (n_assign + n_exp * (blk - 1) + blk - 1) // blk
    blk_first = jnp.arange(n_blocks, dtype=i32) * blk
    blk_expert = jnp.minimum(jnp.sum(pad_end[None, :] <= blk_first[:, None], axis=1, dtype=i32), n_exp - 1)
    delta = start - pad_start
    blk_lo = jnp.clip(blk_first + delta[blk_expert], 0, n_assign).astype(i32)
    s_tok = jnp.concatenate([order // TOP_K, jnp.zeros((blk,), i32)])
    onehot = flat_e[:, None] == jnp.arange(n_exp, dtype=i32)[None, :]
    pos = (rank - jnp.sum(jnp.where(onehot, delta[None, :], 0), axis=1, dtype=i32)).astype(i32)
    n_used = (pad_end[n_exp - 1:n_exp] // blk).astype(i32)
    y_sorted = _expert_stage(h2d, blk_expert, blk_lo, s_tok, n_used, ew, blk)
    return _combine_stage(x1, route, y_sorted, pos, tm)


def _rope_tables(pos):
    inv_freq = ROPE_THETA ** (-jnp.arange(ROPE_HALF, dtype=F32) / ROPE_HALF)
    ang = pos.astype(F32)[:, None] * inv_freq[None, :]
    cos, sin = jnp.cos(ang), jnp.sin(ang)
    n = pos.shape[0]
    zeros = lambda k: jnp.zeros((n, k), F32)
    c = jnp.concatenate([cos, cos, jnp.ones((n, HEAD_DIM - ROPE_DIM), F32)], axis=1)
    s1 = jnp.concatenate([-sin, zeros(HEAD_DIM - ROPE_HALF)], axis=1)
    s2 = jnp.concatenate([zeros(ROPE_HALF), sin, zeros(HEAD_DIM - ROPE_DIM)], axis=1)
    return tuple(jnp.tile(a, (1, LANES // HEAD_DIM)) for a in (c, s1, s2))


def _block_diag_ones():
    i = np.arange(LANES)
    return jnp.asarray((i[:, None] // HEAD_DIM == i[None, :] // HEAD_DIM).astype(np.float32), BF16)


def _pack_proj(w_in, norm_mix, q_norm, k_norm, dnw):
    d = w_in.shape[0]
    qc = NSA_HEADS * HEAD_DIM
    kvc = NSA_KV_HEADS * HEAD_DIM
    sizes = (qc,) + (kvc,) * 6 + (NSA_HEADS * 3, 3 * dnw, DN_HEADS, DN_HEADS, dnw, d, d)
    o = np.concatenate([[0], np.cumsum(sizes)])
    seg = lambda i: w_in[:, int(o[i]):int(o[i + 1])]
    wq = seg(0).reshape(d, NSA_HEADS, HEAD_DIM)[:, np.asarray(HEAD_PERM)].reshape(d, qc)
    small = jnp.concatenate([seg(9), seg(10), seg(7), jnp.zeros((d, LANES - 2 * DN_HEADS - 3 * NSA_HEADS), F32)], axis=1)
    w = jnp.concatenate([wq, seg(1), seg(2), seg(3), seg(4), seg(5), seg(6), small, seg(8), seg(11), seg(12), seg(13)],
                        axis=1).astype(BF16)
    offs, ncols = _proj_layout(dnw, d)
    assert ncols == w.shape[1]
    rep = LANES // HEAD_DIM
    return dict(w=w, offs=offs, ncols=ncols, gin=norm_mix[None, :], qg=jnp.tile(q_norm, rep)[None, :],
                kg=jnp.tile(k_norm[0:2], (1, rep)), bd=_block_diag_ones())


def _pack_compress(w_cmp, cmp_pos, k_norm):
    def bdiag(w):
        z = jnp.zeros_like(w)
        return jnp.concatenate([jnp.concatenate([w, z], axis=2), jnp.concatenate([z, w], axis=2)], axis=1)

    s = CMP_STRIDE
    halves = [(0, slice(0, s)), (0, slice(s, 2 * s)), (1, slice(0, s)), (1, slice(s, 2 * s))]
    return dict(w=jnp.stack([bdiag(w_cmp[i, sl]) for i, sl in halves]).astype(BF16),
                p=jnp.stack([jnp.tile(cmp_pos[i, sl], (1, LANES // HEAD_DIM)) for i, sl in halves]),
                kg=jnp.tile(k_norm[2], LANES // HEAD_DIM)[None, :], bd=_block_diag_ones())


def _overlap(nb, ns_pad, n_cmp, n_sel):
    cs = np.arange(nb)[:, None] * CMP_STRIDE
    ss = np.arange(ns_pad)[None, :] * SEL_BLOCK
    ov = (cs <= ss + SEL_BLOCK - 1) & (cs + CMP_BLOCK - 1 >= ss)
    ov &= (np.arange(nb)[:, None] < n_cmp) & (np.arange(ns_pad)[None, :] < n_sel)
    return jnp.asarray(ov.astype(np.float32), BF16)


def _pad_lanes(v, n=LANES):
    return jnp.zeros((1, n), F32).at[0, :v.shape[0]].set(v)


def _round_up(x, n):
    return (x + n - 1) // n * n


def kernel(x_prompt, x_sample, cache_nsa, page_table, state_win, state_dn_conv, state_dn_S, norm_mix, w_in, q_norm, k_norm, cmp_pos, w_cmp, dn_conv_w, dn_A_log, dn_dt_bias, dn_norm, w_proj_a, w_proj_b, w_out, norm_ffn, w_router_g, b_router_g, w_router_e, b_router_e, w_gate, w_up, w_down):
    b, t, d = x_prompt.shape
    bd_, tn, _ = x_sample.shape
    depth = w_in.shape[0]
    assert depth == 1 and tn == 1, "one layer, one new token per sample sequence"
    n_pool, page = cache_nsa.shape[1], cache_nsa.shape[2]
    n_pages = page_table.shape[1]
    past = n_pages * page
    win_rows = state_win.shape[2]
    dnw = dn_conv_w.shape[2] // 3
    assert t % Q_BLOCK == 0 and t >= WINDOW + Q_BLOCK and t % SEL_CHUNK == 0 and past % CMP_STRIDE == 0
    l = 0

    pw = _pack_proj(w_in[l], norm_mix[l], q_norm[l], k_norm[l], dnw)
    cw = _pack_compress(w_cmp[l], cmp_pos[l], k_norm[l])
    dw = dict(cw=dn_conv_w[l], alog=_pad_lanes(dn_A_log[l]), dtb=_pad_lanes(dn_dt_bias[l]), gn=dn_norm[l][None, :])
    perm = np.asarray(HEAD_PERM)
    wr = jnp.zeros((d, LANES), F32).at[:, 0:N_GROUPS].set(w_router_g[l])
    wr = wr.at[:, EXPERT_LANE0:EXPERT_LANE0 + w_router_e.shape[2]].set(w_router_e[l]).astype(BF16)
    br = jnp.zeros((1, LANES), F32).at[0, 0:N_GROUPS].set(b_router_g[l])
    br = br.at[0, EXPERT_LANE0:EXPERT_LANE0 + b_router_e.shape[1]].set(b_router_e[l])
    mw = dict(wpa=w_proj_a[l].reshape(NSA_HEADS, HEAD_DIM, d)[perm].reshape(NSA_HEADS * HEAD_DIM, d).astype(BF16),
              wpb=w_proj_b[l].astype(BF16), wout=w_out[l].astype(BF16), gf=norm_ffn[l][None, :], wr=wr, br=br)
    ew = dict(wg=w_gate[l].astype(BF16), wu=w_up[l].astype(BF16), wd=w_down[l].astype(BF16))

    tm = 256
    tabs_p = _rope_tables(jnp.arange(t))
    q_p, rows_p, win_p, kvb_p, small_p, dnqkv_p, dnz_p, mg_p = _proj_stage(
        x_prompt.reshape(b * t, d), pw, tabs_p, tm, t // tm)
    nb_p = t // CMP_STRIDE
    n_cmp_p = (t - CMP_BLOCK) // CMP_STRIDE + 1
    n_sel_p = -(-t // SEL_BLOCK)
    tabs_cp = _rope_tables(jnp.arange(nb_p) * CMP_STRIDE + CMP_BLOCK - 1)
    kcv_p = _compress_prompt(rows_p.reshape(b, t, 4 * LANES), cw, tabs_cp)
    ovl_p = _overlap(nb_p, _round_up(n_sel_p, LANES), n_cmp_p, n_sel_p)
    oa_p = _attn_prompt(q_p, small_p, kvb_p.reshape(b, t, 4 * LANES), kcv_p, ovl_p, n_cmp_p)
    tb = DN_TB
    ob_p, s_p = _dn_stage(dnqkv_p.reshape(b, t, 3 * dnw), small_p.reshape(b, t, LANES), dnz_p.reshape(b, t, dnw),
                          jnp.zeros((b, SUBLANES, 3 * dnw), F32),
                          jnp.zeros((b, DN_HEADS, DN_HEAD_DIM, DN_HEAD_DIM), F32), dw, DN_CHUNK, tb, t)
    x1_p, h_p, route_p = _merge_stage(x_prompt.reshape(b * t, d), oa_p, ob_p.reshape(b * t, dnw), mg_p, mw, tm)
    y_p = _moe_stage(x1_p, h_p, route_p, ew, 256, tm).reshape(b, t, d)
    kv_p = rows_p.reshape(1, b, t, 4, NSA_KV_HEADS, HEAD_DIM)
    win_all = win_p.reshape(b, t, 2, NSA_KV_HEADS, HEAD_DIM)
    win_out_p = jnp.pad(win_all, ((0, 0), (max(0, win_rows - t), 0), (0, 0), (0, 0), (0, 0)))[:, -win_rows:][None]
    conv_p = dnqkv_p.reshape(b, t, 3 * dnw)[:, t - (DN_CONV - 1):][None]

    tabs_s = _rope_tables(jnp.full((bd_,), past, jnp.int32))
    q_s, rows_s, win_s, _, small_s, dnqkv_s, dnz_s, mg_s = _proj_stage(x_sample.reshape(bd_, d), pw, tabs_s, bd_, 1)
    cache3 = cache_nsa[l].reshape(n_pool, page, 4 * LANES)
    nb_s = past // CMP_STRIDE
    n_cmp_s = (past + tn - CMP_BLOCK) // CMP_STRIDE + 1
    n_sel_s = -(-(past + tn) // SEL_BLOCK)
    assert n_cmp_s == nb_s - 1 and n_cmp_p == nb_p - 1 and (1 << SEL_SHIFT) == SEL_BLOCK
    tabs_cs = _rope_tables(jnp.arange(nb_s) * CMP_STRIDE + CMP_BLOCK - 1)
    kcv_s = _compress_sample(cache3, page_table, cw, tabs_cs)
    ovl_s = _overlap(nb_s, _round_up(n_sel_s, LANES), n_cmp_s, n_sel_s)
    swin3 = state_win[l].reshape(bd_, win_rows, 2 * LANES)
    oa_s = _attn_sample(cache3, page_table, q_s, small_s, rows_s, win_s, kcv_s, swin3, ovl_s, n_cmp_s, n_sel_s)
    pad_t = lambda a: jnp.pad(a[:, None, :], ((0, 0), (0, SUBLANES - tn), (0, 0)))
    conv0_s = jnp.pad(state_dn_conv[l], ((0, 0), (SUBLANES - (DN_CONV - 1), 0), (0, 0)))
    ob_s, s_s = _dn_stage(pad_t(dnqkv_s), pad_t(small_s), pad_t(dnz_s), conv0_s, state_dn_S[l], dw,
                          SUBLANES, SUBLANES, tn)
    x1_s, h_s, route_s = _merge_stage(x_sample.reshape(bd_, d), oa_s, ob_s[:, 0, :], mg_s, mw, bd_)
    y_s = _moe_stage(x1_s, h_s, route_s, ew, 64, bd_).reshape(bd_, tn, d)
    kv_s = rows_s.reshape(1, bd_, tn, 4, NSA_KV_HEADS, HEAD_DIM)
    win_new = win_s.reshape(bd_, tn, 2, NSA_KV_HEADS, HEAD_DIM)
    win_out_s = jnp.concatenate([state_win[l], win_new], axis=1)[:, -win_rows:][None]
    conv_s = jnp.concatenate([state_dn_conv[l], dnqkv_s[:, None, :]], axis=1)[:, -(DN_CONV - 1):][None]

    return (y_p, y_s, kv_p, kv_s, win_out_p, win_out_s, conv_p, conv_s, s_p[None], s_s[None])
```

```python
import functools
import math

import numpy as np
import jax
import jax.numpy as jnp
from jax import lax
from jax.experimental import pallas as pl
from jax.experimental.pallas import tpu as pltpu

F32 = jnp.float32
BF16 = jnp.bfloat16

NSA_HEADS = 8
NSA_KV_HEADS = 2
NSA_GROUP = NSA_HEADS // NSA_KV_HEADS
HEAD_DIM = 64
ROPE_DIM = HEAD_DIM // 4
ROPE_HALF = ROPE_DIM // 2
ROPE_THETA = 500000.0
CMP_BLOCK = 32
CMP_STRIDE = 16
SEL_BLOCK = 64
SEL_SHIFT = 6
SEL_TOPN = 16
WINDOW = 512
FORCE_BONUS = 1000.0
DN_HEADS = 4
DN_HEAD_DIM = 128
DN_CONV = 4
DN_CHUNK = 64
N_GROUPS = 4
EXPERTS_PER_GROUP = 8
TOP_K = 2
EPS = 1e-6

LANES = 128
SUBLANES = 8
VMEM_LIMIT = 56 * 1024 * 1024
NEG = -1e30
Q_BLOCK = 128
SEL_CHUNK = 1024
PROJ_TM = 512
DN_TB = 512
SHIFT_SLACK = 1.0 + 2.0 ** -5
SHIFT_MIN_SUM = 1e-20
HEAD_PERM = (0, 4, 1, 5, 2, 6, 3, 7)
SM_A, SM_B, SM_GATE = 0, DN_HEADS, 2 * DN_HEADS
EXPERT_LANE0 = 32


def _cparams(sem):
    return pltpu.CompilerParams(dimension_semantics=sem, vmem_limit_bytes=VMEM_LIMIT)


def _dot(a, b):
    return jnp.dot(a, b, preferred_element_type=F32)


def _dot_nt(a, b):
    return lax.dot_general(a, b, (((1,), (1,)), ((), ())), preferred_element_type=F32)


def _dot_tn(a, b):
    return lax.dot_general(a, b, (((0,), (0,)), ((), ())), preferred_element_type=F32)


def _split2(x):
    hi = x.astype(BF16)
    lo = (x - hi.astype(F32)).astype(BF16)
    return hi, lo


def _split3(x):
    hi = x.astype(BF16)
    r = x - hi.astype(F32)
    mid = r.astype(BF16)
    lo = (r - mid.astype(F32)).astype(BF16)
    return hi, mid, lo


def _dot3s(a, b):
    return _dot(a[0], b[0]) + (_dot(a[0], b[1]) + _dot(a[1], b[0]))


def _mm(a, b, precise, form=_dot):
    if not precise:
        return form(a.astype(BF16), b.astype(BF16))
    ah, al = _split2(a.astype(F32))
    bh, bl = _split2(b.astype(F32))
    return form(ah, bh) + (form(ah, bl) + form(al, bh))


def _sigmoid(x):
    return 1.0 / (1.0 + jnp.exp(-x))


def _seg_sumsq(x, bd):
    hi, lo = _split2(x * x)
    return _dot(hi, bd) + _dot(lo, bd)


def _rope128(x, c, s1, s2):
    return x * c + pltpu.roll(x, LANES - ROPE_HALF, 1) * s1 + pltpu.roll(x, ROPE_HALF, 1) * s2


def _head_norm_rope(v, gain, bd, c, s1, s2):
    ss = _seg_sumsq(v, bd) * (1.0 / HEAD_DIM)
    return _rope128(v * lax.rsqrt(ss + EPS) * gain, c, s1, s2)


def _masked_softmax(s, valid):
    s = jnp.where(valid, s, NEG)
    m = jnp.max(s, axis=-1, keepdims=True)
    e = jnp.where(valid, jnp.exp(s - m), 0.0)
    l = jnp.sum(e, axis=-1, keepdims=True)
    return e * (1.0 / jnp.maximum(l, 1e-30))


def _topk_mask(score, k):
    lane = lax.broadcasted_iota(jnp.int32, score.shape, 1).astype(F32)
    big = float(score.shape[-1])

    def body(_, carry):
        sc, sel = carry
        m = jnp.max(sc, axis=-1, keepdims=True)
        idx = jnp.min(jnp.where(sc == m, lane, big), axis=-1, keepdims=True)
        hit = lane == idx
        sel = jnp.where(hit, jnp.maximum(sel, jnp.where(m > -jnp.inf, 1.0, 0.0)), sel)
        sc = jnp.where(hit, -jnp.inf, sc)
        return sc, sel

    _, sel = lax.fori_loop(0, k, body, (score, jnp.zeros(score.shape, F32)))
    return sel


def _topk_mask_rows(score, k):
    idx = lax.broadcasted_iota(jnp.int32, score.shape, 0).astype(F32)
    big = float(score.shape[0])

    def body(_, carry):
        sc, sel = carry
        m = jnp.max(sc, axis=0, keepdims=True)
        first = jnp.min(jnp.where(sc == m, idx, big), axis=0, keepdims=True)
        hit = idx == first
        sel = jnp.where(hit, jnp.maximum(sel, jnp.where(m > -jnp.inf, 1.0, 0.0)), sel)
        sc = jnp.where(hit, -jnp.inf, sc)
        return sc, sel

    _, sel = lax.fori_loop(0, k, body, (score, jnp.zeros(score.shape, F32)))
    return sel


def _proj_layout(dnw, d):
    sizes = dict(q=NSA_HEADS * HEAD_DIM, kvc=2 * LANES, ks=LANES, vs=LANES, kw=LANES, vw=LANES,
                 small=LANES, dnqkv=3 * dnw, dnz=dnw, mg=2 * d)
    offs, o = {}, 0
    for name, n in sizes.items():
        offs[name] = (o, o + n)
        o += n
    return offs, o


def _proj_kernel(x_ref, gin_ref, w_ref, c_ref, s1_ref, s2_ref, qg_ref, kg_ref, bd_ref,
                 q_out, rows_out, win_out, kvb_out, small_out, dnqkv_out, dnz_out, mg_out, *, offs, precise):
    x = x_ref[...]
    ms = jnp.mean(x * x, axis=-1, keepdims=True)
    hb = (x * lax.rsqrt(ms + EPS)) * gin_ref[...]
    if not precise:
        hb = hb.astype(BF16)

    def mm(name, lo=0, hi=None):
        a, b = offs[name]
        hi = b - a if hi is None else hi
        return _mm(hb, w_ref[:, a + lo:a + hi], precise)

    c, s1, s2 = c_ref[...], s1_ref[...], s2_ref[...]
    bd = bd_ref[...]
    q_all = mm("q")
    for j in range(NSA_HEADS * HEAD_DIM // LANES):
        qj = _head_norm_rope(q_all[:, LANES * j:LANES * (j + 1)], qg_ref[...], bd, c, s1, s2)
        q_out[:, LANES * j:LANES * (j + 1)] = (qj * (HEAD_DIM ** -0.5)).astype(BF16)
    rows_out[:, 0:2 * LANES] = mm("kvc")
    five = _mm(hb, w_ref[:, offs["ks"][0]:offs["small"][1]], precise)
    ks = _head_norm_rope(five[:, 0:LANES], kg_ref[0:1, :], bd, c, s1, s2)
    vs = five[:, LANES:2 * LANES]
    rows_out[:, 2 * LANES:3 * LANES] = ks
    rows_out[:, 3 * LANES:4 * LANES] = vs
    kw = _head_norm_rope(five[:, 2 * LANES:3 * LANES], kg_ref[1:2, :], bd, c, s1, s2)
    vw = five[:, 3 * LANES:4 * LANES]
    win_out[:, 0:LANES] = kw
    win_out[:, LANES:2 * LANES] = vw
    kvb_out[:, 0:LANES] = ks.astype(BF16)
    kvb_out[:, LANES:2 * LANES] = vs.astype(BF16)
    kvb_out[:, 2 * LANES:3 * LANES] = kw.astype(BF16)
    kvb_out[:, 3 * LANES:4 * LANES] = vw.astype(BF16)
    small_out[...] = five[:, 4 * LANES:5 * LANES]
    dnqkv_out[...] = mm("dnqkv")
    dnz_out[...] = mm("dnz")
    mg_out[...] = _sigmoid(mm("mg"))


def _proj_stage(x2d, pw, tabs, tm, n_tab_blocks, precise=False):
    m, d = x2d.shape
    offs, ncols = pw["offs"], pw["ncols"]
    dnw = offs["dnz"][1] - offs["dnz"][0]
    row = lambda w: pl.BlockSpec((tm, w), lambda i: (i, 0))
    full = lambda a: pl.BlockSpec(a.shape, lambda i: (0,) * a.ndim)
    tab = pl.BlockSpec((tm, LANES), lambda i: (i % n_tab_blocks, 0))
    out_shape = (
        jax.ShapeDtypeStruct((m, 4 * LANES), BF16),
        jax.ShapeDtypeStruct((m, 4 * LANES), F32),
        jax.ShapeDtypeStruct((m, 2 * LANES), F32),
        jax.ShapeDtypeStruct((m, 4 * LANES), BF16),
        jax.ShapeDtypeStruct((m, LANES), F32),
        jax.ShapeDtypeStruct((m, 3 * dnw), F32),
        jax.ShapeDtypeStruct((m, dnw), F32),
        jax.ShapeDtypeStruct((m, 2 * d), F32),
    )
    return pl.pallas_call(
        functools.partial(_proj_kernel, offs=offs, precise=precise),
        grid=(m // tm,),
        in_specs=[row(d), full(pw["gin"]),
                  pl.BlockSpec(pw["w"].shape, lambda i: (0, 0), pipeline_mode=pl.Buffered(1)),
                  tab, tab, tab, full(pw["qg"]), full(pw["kg"]), full(pw["bd"])],
        out_specs=tuple(row(s.shape[1]) for s in out_shape),
        out_shape=out_shape,
        compiler_params=_cparams(("parallel",)),
    )(x2d, pw["gin"], pw["w"], tabs[0], tabs[1], tabs[2], pw["qg"], pw["kg"], pw["bd"])


def _compress_body(xk_ref, xv_ref, w_ref, p_ref, kg_ref, bd_ref, c_ref, s1_ref, s2_ref, out_ref):
    nb = xk_ref.shape[0] // CMP_STRIDE
    acc = [jnp.zeros((nb, LANES), F32) for _ in range(4)]
    for l in range(CMP_STRIDE):
        xs = (xk_ref[pl.ds(l, nb, stride=CMP_STRIDE), :], xv_ref[pl.ds(l, nb, stride=CMP_STRIDE), :])
        for j in range(4):
            acc[j] = acc[j] + _dot((xs[j // 2] + p_ref[j, l:l + 1, :]).astype(BF16), w_ref[j, l])
    kraw = acc[0] + pltpu.roll(acc[1], nb - 1, 0)
    vraw = acc[2] + pltpu.roll(acc[3], nb - 1, 0)
    kc = _head_norm_rope(kraw, kg_ref[...], bd_ref[...], c_ref[...], s1_ref[...], s2_ref[...])
    out_ref[:, 0:LANES] = kc.astype(BF16)
    out_ref[:, LANES:2 * LANES] = vraw.astype(BF16)


def _compress_prompt_kernel(xk_ref, xv_ref, *rest):
    _compress_body(xk_ref, xv_ref, *rest)


def _compress_prompt(rows3, cw, tabs):
    b, t, _ = rows3.shape
    nb = t // CMP_STRIDE
    full = lambda a: pl.BlockSpec(a.shape, lambda i: (0,) * a.ndim)
    consts = (cw["w"], cw["p"], cw["kg"], cw["bd"]) + tuple(tabs)
    return pl.pallas_call(
        _compress_prompt_kernel,
        grid=(b,),
        in_specs=[pl.BlockSpec((None, t, LANES), lambda i: (i, 0, 0)),
                  pl.BlockSpec((None, t, LANES), lambda i: (i, 0, 1))] + [full(a) for a in consts],
        out_specs=pl.BlockSpec((None, nb, 2 * LANES), lambda i: (i, 0, 0)),
        out_shape=jax.ShapeDtypeStruct((b, nb, 2 * LANES), BF16),
        compiler_params=_cparams(("parallel",)),
    )(rows3, rows3, *consts)


def _wait_all(buf_view, sem):
    pltpu.make_async_copy(buf_view, buf_view, sem).wait()


def _start_pages(pt_ref, cache_ref, buf_ref, sem, b, slot, n_pages, page, lane0):
    split = len(buf_ref.shape) == 4

    def start(p, c):
        rows = pl.ds(pl.multiple_of(p * page, page), page)
        if split:
            for j in range(2):
                pltpu.make_async_copy(cache_ref.at[pt_ref[b, p], :, pl.ds(lane0 + j * LANES, LANES)],
                                      buf_ref.at[slot, j, rows, :], sem.at[slot]).start()
        else:
            pltpu.make_async_copy(cache_ref.at[pt_ref[b, p], :, pl.ds(lane0, 2 * LANES)],
                                  buf_ref.at[slot, rows, :], sem.at[slot]).start()
        return c

    lax.fori_loop(0, n_pages, start, 0)


def _prefetch_pages(pt_ref, cache_ref, buf_ref, sem, n_pages, page, lane0):
    b = pl.program_id(0)
    slot = b % 2

    @pl.when(b == 0)
    def _():
        _start_pages(pt_ref, cache_ref, buf_ref, sem, 0, 0, n_pages, page, lane0)

    @pl.when(b + 1 < pl.num_programs(0))
    def _():
        _start_pages(pt_ref, cache_ref, buf_ref, sem, b + 1, 1 - slot, n_pages, page, lane0)

    return slot


def _compress_sample_kernel(pt_ref, cache_ref, *rest, n_pages, page):
    *consts, out_ref, buf_ref, sem = rest
    slot = _prefetch_pages(pt_ref, cache_ref, buf_ref, sem, n_pages, page, 0)
    _wait_all(buf_ref.at[slot], sem.at[slot])
    _compress_body(buf_ref.at[slot, 0], buf_ref.at[slot, 1], *consts, out_ref)


def _compress_sample(cache3, page_table, cw, tabs):
    bd_, n_pages = page_table.shape
    page = cache3.shape[1]
    past = n_pages * page
    nb = past // CMP_STRIDE
    full = lambda a: pl.BlockSpec(a.shape, lambda i, pt: (0,) * a.ndim)
    consts = (cw["w"], cw["p"], cw["kg"], cw["bd"]) + tuple(tabs)
    grid_spec = pltpu.PrefetchScalarGridSpec(
        num_scalar_prefetch=1,
        grid=(bd_,),
        in_specs=[pl.BlockSpec(memory_space=pl.ANY)] + [full(a) for a in consts],
        out_specs=pl.BlockSpec((None, nb, 2 * LANES), lambda i, pt: (i, 0, 0)),
        scratch_shapes=[pltpu.VMEM((2, 2, past, LANES), F32), pltpu.SemaphoreType.DMA((2,))],
    )
    return pl.pallas_call(
        functools.partial(_compress_sample_kernel, n_pages=n_pages, page=page),
        grid_spec=grid_spec,
        out_shape=jax.ShapeDtypeStruct((bd_, nb, 2 * LANES), BF16),
        compiler_params=_cparams(("arbitrary",)),
    )(page_table, cache3, *consts)


def _stack_heads(q, lo_half):
    zero = jnp.zeros((), q.dtype)
    parts = []
    for g in range(NSA_KV_HEADS):
        for r in range(NSA_GROUP):
            col = q[:, LANES * r:LANES * (r + 1)]
            parts.append(jnp.where(lo_half if g == 0 else jnp.logical_not(lo_half), col, zero))
    return jnp.concatenate(parts, axis=0)


def _sel_scores(imp, pos_q, blk):
    cur = pos_q >> SEL_SHIFT
    readable = blk * SEL_BLOCK <= pos_q
    forced = jnp.logical_or(blk == 0, jnp.logical_or(blk == cur, blk == cur - 1))
    return jnp.where(readable, imp + jnp.where(forced, FORCE_BONUS, 0.0), -jnp.inf)


def _attn_prompt_kernel(q_ref, small_ref, kvb_ref, kcv_ref, ovl_ref, bd_ref, o_ref, kmx_ref, os_ref, *, n_cmp):
    qb = Q_BLOCK
    m8 = NSA_HEADS * qb
    i = pl.program_id(1)
    q0 = i * qb

    @pl.when(i == 0)
    def _():
        def body(c, mx):
            kf = kvb_ref[pl.ds(pl.multiple_of(c * SEL_CHUNK, SEL_CHUNK), SEL_CHUNK), 0:LANES].astype(F32)
            return jnp.maximum(mx, jnp.max(_seg_sumsq(kf, bd_ref[...]), axis=0, keepdims=True))
        kmx_ref[...] = lax.fori_loop(0, kvb_ref.shape[0] // SEL_CHUNK, body, jnp.zeros((1, LANES), F32))

    lane = lax.broadcasted_iota(jnp.int32, (qb, LANES), 1)
    lo_half = lane < HEAD_DIM
    q8 = _stack_heads(q_ref[...], lo_half)

    def qpos(n):
        return q0 + (lax.broadcasted_iota(jnp.int32, (m8, n), 0) & (qb - 1))

    def kidx(n):
        return lax.broadcasted_iota(jnp.int32, (m8, n), 1)

    nb = kcv_ref.shape[0]
    s = _dot_nt(q8, kcv_ref[:, 0:LANES])
    cidx = kidx(nb)
    valid = jnp.logical_and(cidx < n_cmp, cidx * CMP_STRIDE + (CMP_BLOCK - 1) <= qpos(nb))
    p = _masked_softmax(s, valid).astype(BF16)
    o_c = _dot(p, kcv_ref[:, LANES:2 * LANES])
    imp8 = _dot(p, ovl_ref[...])

    ns = ovl_ref.shape[1]
    imps = []
    for g in range(NSA_KV_HEADS):
        acc = imp8[(g * NSA_GROUP) * qb:(g * NSA_GROUP + 1) * qb]
        for r in range(1, NSA_GROUP):
            acc = acc + imp8[(g * NSA_GROUP + r) * qb:(g * NSA_GROUP + r + 1) * qb]
        imps.append(acc)
    imp = jnp.concatenate(imps, axis=0)
    pos2 = q0 + (lax.broadcasted_iota(jnp.int32, (2 * qb, ns), 0) & (qb - 1))
    blk2 = lax.broadcasted_iota(jnp.int32, (2 * qb, ns), 1)
    sel = jnp.transpose(_topk_mask_rows(jnp.transpose(_sel_scores(imp, pos2, blk2)), SEL_TOPN))
    sel8 = jnp.concatenate([sel[0:qb]] * NSA_GROUP + [sel[qb:2 * qb]] * NSA_GROUP, axis=0) > 0.0

    ck = SEL_CHUNK
    qpos_k = qpos(ck)
    kk = lax.broadcasted_iota(jnp.int32, (ck, ns), 0)
    ss = lax.broadcasted_iota(jnp.int32, (ck, ns), 1)
    n_chunks = (q0 + qb + ck - 1) // ck

    def scores(qaug, c, causal):
        k0 = pl.multiple_of(c * ck, ck)
        e = jnp.where(((k0 + kk) >> SEL_SHIFT) == ss, 1.0, 0.0).astype(BF16)
        sc = _dot_nt(qaug, jnp.concatenate([kvb_ref[pl.ds(k0, ck), 0:LANES], e], axis=1))
        if causal:
            sc = jnp.where(k0 + kidx(ck) <= qpos_k, sc, NEG)
        return sc, kvb_ref[pl.ds(k0, ck), LANES:2 * LANES]

    q8f = q8.astype(F32)
    qn2 = jnp.sum(q8f * q8f, axis=-1, keepdims=True)
    kmx = kmx_ref[...]
    half = NSA_GROUP * qb
    kq2 = jnp.concatenate([qn2[0:half] * kmx[:, 0:1], qn2[half:] * kmx[:, HEAD_DIM:HEAD_DIM + 1]], axis=0)
    shift = jnp.sqrt(kq2) * SHIFT_SLACK + 1e-6
    qaug = jnp.concatenate([q8, jnp.where(sel8, -shift, NEG).astype(BF16)], axis=1)

    def chunk_shifted(c, carry, causal):
        l, acc = carry
        sc, vmat = scores(qaug, c, causal)
        pe = jnp.exp(sc)
        return l + jnp.sum(pe, axis=-1, keepdims=True), acc + _dot(pe.astype(BF16), vmat)

    carry = lax.fori_loop(0, n_chunks - 1, functools.partial(chunk_shifted, causal=False),
                          (jnp.zeros((m8, 1), F32), jnp.zeros((m8, LANES), F32)))
    l_s, acc_s = chunk_shifted(n_chunks - 1, carry, True)
    os_ref[...] = acc_s * (1.0 / jnp.maximum(l_s, 1e-37))

    @pl.when(jnp.min(l_s) < SHIFT_MIN_SUM)
    def _():
        qaug_x = jnp.concatenate([q8, jnp.where(sel8, 0.0, NEG).astype(BF16)], axis=1)

        def chunk_online(c, carry, causal):
            m, l, acc = carry
            sc, vmat = scores(qaug_x, c, causal)
            m_new = jnp.maximum(m, jnp.max(sc, axis=-1, keepdims=True))
            alpha = jnp.exp(m - m_new)
            pe = jnp.exp(sc - m_new)
            return (m_new, alpha * l + jnp.sum(pe, axis=-1, keepdims=True),
                    alpha * acc + _dot(pe.astype(BF16), vmat))

        init = (jnp.full((m8, 1), NEG, F32), jnp.zeros((m8, 1), F32), jnp.zeros((m8, LANES), F32))
        carry_x = lax.fori_loop(0, n_chunks - 1, functools.partial(chunk_online, causal=False), init)
        _, l_x, acc_x = chunk_online(n_chunks - 1, carry_x, True)
        os_ref[...] = acc_x * (1.0 / l_x)

    o_s = os_ref[...]

    wk = WINDOW + qb
    ws = pl.multiple_of(jnp.maximum(q0 - WINDOW, 0), qb)
    s = _dot_nt(q8, kvb_ref[pl.ds(ws, wk), 2 * LANES:3 * LANES])
    dpos = qpos(wk) - (ws + kidx(wk))
    p = _masked_softmax(s, jnp.logical_and(dpos >= 0, dpos < WINDOW)).astype(BF16)
    o_w = _dot(p, kvb_ref[pl.ds(ws, wk), 3 * LANES:4 * LANES])

    sig = _sigmoid(small_ref[...])
    for r in range(NSA_GROUP):
        outs = []
        for g in range(NSA_KV_HEADS):
            h = g * NSA_GROUP + r
            rows = slice(h * qb, (h + 1) * qb)
            gl = SM_GATE + 3 * h
            outs.append(sig[:, gl:gl + 1] * o_c[rows] + sig[:, gl + 1:gl + 2] * o_s[rows]
                        + sig[:, gl + 2:gl + 3] * o_w[rows])
        o_ref[:, LANES * r:LANES * (r + 1)] = jnp.where(lo_half, outs[0], outs[1]).astype(BF16)


def _attn_prompt(q2d, small2d, kvb3, kcv3, ovl, n_cmp):
    b, t, _ = kvb3.shape
    nq = t // Q_BLOCK
    nb = kcv3.shape[1]
    bd = _block_diag_ones()
    return pl.pallas_call(
        functools.partial(_attn_prompt_kernel, n_cmp=n_cmp),
        grid=(b, nq),
        in_specs=[pl.BlockSpec((Q_BLOCK, 4 * LANES), lambda bi, i: (bi * nq + i, 0)),
                  pl.BlockSpec((Q_BLOCK, LANES), lambda bi, i: (bi * nq + i, 0)),
                  pl.BlockSpec((None, t, 4 * LANES), lambda bi, i: (bi, 0, 0)),
                  pl.BlockSpec((None, nb, 2 * LANES), lambda bi, i: (bi, 0, 0)),
                  pl.BlockSpec(ovl.shape, lambda bi, i: (0, 0)),
                  pl.BlockSpec(bd.shape, lambda bi, i: (0, 0))],
        out_specs=pl.BlockSpec((Q_BLOCK, 4 * LANES), lambda bi, i: (bi * nq + i, 0)),
        out_shape=jax.ShapeDtypeStruct((b * t, 4 * LANES), BF16),
        scratch_shapes=[pltpu.VMEM((1, LANES), F32), pltpu.VMEM((NSA_HEADS * Q_BLOCK, LANES), F32)],
        compiler_params=_cparams(("parallel", "arbitrary")),
    )(q2d, small2d, kvb3, kcv3, ovl, bd)


def _attn_sample_kernel(pt_ref, cache_ref, q_ref, small_ref, rows_ref, win_ref, kcv_ref, swin_ref, ovl_ref,
                        o_ref, buf_ref, sem, *, n_pages, page, n_cmp, n_sel, key_chunk):
    past = n_pages * page
    slot = _prefetch_pages(pt_ref, cache_ref, buf_ref, sem, n_pages, page, 2 * LANES)
    lane1 = lax.broadcasted_iota(jnp.int32, (1, LANES), 1)
    lo1 = lane1 < HEAD_DIM
    q8f = _stack_heads(q_ref[...].astype(F32), lo1)
    q8 = q8f.astype(BF16)
    nh = NSA_HEADS

    nb = kcv_ref.shape[0]
    s = _dot_nt(q8, kcv_ref[:, 0:LANES])
    p = _masked_softmax(s, lax.broadcasted_iota(jnp.int32, (nh, nb), 1) < n_cmp).astype(BF16)
    o_c = _dot(p, kcv_ref[:, LANES:2 * LANES])
    imp8 = _dot(p, ovl_ref[...])
    nsp = ovl_ref.shape[1]
    blk = lax.broadcasted_iota(jnp.int32, (1, nsp), 1)
    selbs = []
    for g in range(NSA_KV_HEADS):
        imp = jnp.sum(imp8[g * NSA_GROUP:(g + 1) * NSA_GROUP], axis=0, keepdims=True)
        score = jnp.where(blk < n_sel, _sel_scores(imp, past, blk), -jnp.inf)
        selb = jnp.where(_topk_mask(score, SEL_TOPN) > 0.0, 0.0, NEG)
        selbs += [selb] * NSA_GROUP
    selb8 = jnp.concatenate(selbs, axis=0)
    selb8_b = selb8.astype(BF16)

    _wait_all(buf_ref.at[slot], sem.at[slot])
    ks_new = rows_ref[:, 2 * LANES:3 * LANES].astype(BF16).astype(F32)
    vs_new = rows_ref[:, 3 * LANES:4 * LANES].astype(BF16).astype(F32)
    s_new = jnp.sum(q8f * ks_new, axis=-1, keepdims=True) + selb8[:, n_sel - 1:n_sel]
    ck = key_chunk
    n_ck = past // ck
    ss = lax.broadcasted_iota(jnp.int32, (nsp, ck), 0)
    kk = lax.broadcasted_iota(jnp.int32, (nsp, ck), 1)
    scores = []
    for c in range(n_ck):
        kmat = buf_ref[slot, c * ck:(c + 1) * ck, 0:LANES].astype(BF16)
        e = jnp.where(((c * ck + kk) >> SEL_SHIFT) == ss, 1.0, 0.0).astype(BF16)
        scores.append(_dot_nt(q8, kmat) + _dot(selb8_b, e))
    m = s_new
    for sc in scores:
        m = jnp.maximum(m, jnp.max(sc, axis=-1, keepdims=True))
    p_new = jnp.exp(s_new - m)
    l = p_new
    acc = p_new * vs_new
    for c, sc in enumerate(scores):
        pe = jnp.exp(sc - m)
        l = l + jnp.sum(pe, axis=-1, keepdims=True)
        acc = acc + _dot(pe.astype(BF16), buf_ref[slot, c * ck:(c + 1) * ck, LANES:2 * LANES].astype(BF16))
    o_s = acc * (1.0 / l)

    wr = swin_ref.shape[0]
    s = _dot_nt(q8, swin_ref[:, 0:LANES].astype(BF16))
    j = lax.broadcasted_iota(jnp.int32, (nh, wr), 1)
    valid = wr - j < WINDOW
    kw_new = win_ref[:, 0:LANES].astype(BF16).astype(F32)
    vw_new = win_ref[:, LANES:2 * LANES].astype(BF16).astype(F32)
    s_new = jnp.sum(q8f * kw_new, axis=-1, keepdims=True)
    s = jnp.where(valid, s, NEG)
    m = jnp.maximum(s_new, jnp.max(s, axis=-1, keepdims=True))
    pe = jnp.where(valid, jnp.exp(s - m), 0.0)
    p_new = jnp.exp(s_new - m)
    l = p_new + jnp.sum(pe, axis=-1, keepdims=True)
    o_w = (p_new * vw_new + _dot(pe.astype(BF16), swin_ref[:, LANES:2 * LANES].astype(BF16))) * (1.0 / l)

    sig = _sigmoid(small_ref[...])
    for r in range(NSA_GROUP):
        outs = []
        for g in range(NSA_KV_HEADS):
            h = g * NSA_GROUP + r
            gl = SM_GATE + 3 * h
            outs.append(sig[:, gl:gl + 1] * o_c[h:h + 1] + sig[:, gl + 1:gl + 2] * o_s[h:h + 1]
                        + sig[:, gl + 2:gl + 3] * o_w[h:h + 1])
        o_ref[:, LANES * r:LANES * (r + 1)] = jnp.where(lo1, outs[0], outs[1])


def _attn_sample(cache3, page_table, q2d, small2d, rows2d, win2d, kcv3, swin3, ovl, n_cmp, n_sel):
    bd_, n_pages = page_table.shape
    page = cache3.shape[1]
    past = n_pages * page
    key_chunk = math.gcd(past, 2048)
    per_seq = lambda a: pl.BlockSpec((None,) + a.shape[1:], lambda i, pt: (i,) + (0,) * (a.ndim - 1))
    q3, small3, rows3, win3 = (a[:, None, :] for a in (q2d, small2d, rows2d, win2d))
    grid_spec = pltpu.PrefetchScalarGridSpec(
        num_scalar_prefetch=1,
        grid=(bd_,),
        in_specs=[pl.BlockSpec(memory_space=pl.ANY), per_seq(q3), per_seq(small3), per_seq(rows3), per_seq(win3),
                  per_seq(kcv3), per_seq(swin3), pl.BlockSpec(ovl.shape, lambda i, pt: (0, 0))],
        out_specs=pl.BlockSpec((None, 1, 4 * LANES), lambda i, pt: (i, 0, 0)),
        scratch_shapes=[pltpu.VMEM((2, past, 2 * LANES), F32), pltpu.SemaphoreType.DMA((2,))],
    )
    out = pl.pallas_call(
        functools.partial(_attn_sample_kernel, n_pages=n_pages, page=page, n_cmp=n_cmp, n_sel=n_sel,
                          key_chunk=key_chunk),
        grid_spec=grid_spec,
        out_shape=jax.ShapeDtypeStruct((bd_, 1, 4 * LANES), F32),
        compiler_params=_cparams(("arbitrary",)),
    )(page_table, cache3, q3, small3, rows3, win3, kcv3, swin3, ovl)
    return out[:, 0, :]


def _dn_kernel(x_ref, small_ref, z_ref, cw_ref, c0_ref, s0_ref, alog_ref, dtb_ref, gn_ref,
               o_ref, sout_ref, s_scr, xs_scr, *, chunk, tb, t_valid, precise):
    j = pl.program_id(1)
    dnw = z_ref.shape[1]
    hd = DN_HEAD_DIM
    hist = SUBLANES

    def opnd(x):
        return x if precise else x.astype(BF16)

    def mm(a, b, form=_dot):
        return _mm(a, b, precise, form)

    @pl.when(j == 0)
    def _():
        s_scr[...] = s0_ref[...]
        xs_scr[0:hist, :] = c0_ref[...]

    xs_scr[hist:hist + tb, :] = x_ref[...]
    conv = xs_scr[pl.ds(hist - (DN_CONV - 1), tb), :] * cw_ref[0:1, :]
    for jj in range(1, DN_CONV):
        conv = conv + xs_scr[pl.ds(hist - (DN_CONV - 1) + jj, tb), :] * cw_ref[jj:jj + 1, :]
    xs_scr[0:hist, :] = xs_scr[tb:tb + hist, :]
    act = conv * _sigmoid(conv)

    small = small_ref[...]
    tpos = j * tb + lax.broadcasted_iota(jnp.int32, (tb, LANES), 0)
    live = tpos < t_valid
    xg = small + dtb_ref[...]
    softplus = jnp.maximum(xg, 0.0) + jnp.log(1.0 + jnp.exp(-jnp.abs(xg)))
    g_all = jnp.where(live, -jnp.exp(alog_ref[...]) * softplus, 0.0)
    beta_all = jnp.where(live, _sigmoid(small), 0.0)

    ri = lax.broadcasted_iota(jnp.int32, (chunk, chunk), 0)
    ci = lax.broadcasted_iota(jnp.int32, (chunk, chunk), 1)
    incl = ri >= ci
    strict = ri > ci
    ltri = jnp.where(incl, 1.0, 0.0).astype(BF16)
    eye = jnp.where(ri == ci, 1.0, 0.0)
    lane_c = lax.broadcasted_iota(jnp.int32, (chunk, LANES), 1)
    n_sq = max(1, int(math.ceil(math.log2(chunk))))

    n_chunks = tb // chunk
    items = [(c, h) for c in range(n_chunks) for h in range(DN_HEADS)]

    gc_alls, gc3s = [], []
    for c in range(n_chunks):
        g3 = _split3(g_all[c * chunk:(c + 1) * chunk])
        gc_all = _dot(ltri, g3[0]) + (_dot(ltri, g3[1]) + _dot(ltri, g3[2]))
        gc_alls.append(gc_all)
        gc3s.append(_split3(gc_all))

    prep = []
    for c, h in items:
        rs = slice(c * chunk, (c + 1) * chunk)
        q = act[rs, h * hd:(h + 1) * hd]
        k = act[rs, dnw + h * hd:dnw + (h + 1) * hd]
        v = act[rs, 2 * dnw + h * hd:2 * dnw + (h + 1) * hd]
        q = q * lax.rsqrt(jnp.sum(q * q, axis=-1, keepdims=True) + EPS) * (hd ** -0.5)
        k = k * lax.rsqrt(jnp.sum(k * k, axis=-1, keepdims=True) + EPS)
        beta = beta_all[rs, SM_B + h:SM_B + h + 1]
        gc = gc_alls[c][:, SM_A + h:SM_A + h + 1]
        pick = jnp.where(lane_c == SM_A + h, 1.0, 0.0).astype(BF16)
        g3 = gc3s[c]
        gc_row = _dot_nt(pick, g3[0]) + (_dot_nt(pick, g3[1]) + _dot_nt(pick, g3[2]))
        dmask = jnp.where(incl, jnp.exp(jnp.where(incl, gc - gc_row, 0.0)), 0.0)
        kb = k * beta
        kop = opnd(k)
        a_strict = jnp.where(strict, mm(opnd(kb), kop, _dot_nt) * dmask, 0.0)
        egc = jnp.exp(gc)
        g_last = gc[chunk - 1:chunk, :]
        prep.append(dict(
            vb=opnd(v * beta), kbg=opnd(kb * egc), qg=opnd(q * egc),
            qk=opnd(mm(opnd(q), kop, _dot_nt) * dmask),
            kd=opnd(k * jnp.exp(g_last - gc)), decay=jnp.exp(g_last), n=-a_strict))

    tinv = [eye + p["n"] for p in prep]
    nsp = [opnd(p["n"]) for p in prep]
    for _ in range(n_sq - 1):
        nsp = [opnd(mm(n, n)) for n in nsp]
        tinv = [t + mm(opnd(t), n) for t, n in zip(tinv, nsp)]
    us, ws = [], []
    for p, t in zip(prep, tinv):
        top = opnd(t)
        us.append(mm(top, p["vb"]))
        ws.append(opnd(mm(top, p["kbg"])))

    state = [s_scr[h] for h in range(DN_HEADS)]
    for i, (c, h) in enumerate(items):
        p = prep[i]
        rs = slice(c * chunk, (c + 1) * chunk)
        s_b = opnd(state[h])
        v_new = opnd(us[i] - mm(ws[i], s_b))
        o = mm(p["qg"], s_b) + mm(p["qk"], v_new)
        state[h] = state[h] * p["decay"] + mm(p["kd"], v_new, _dot_tn)
        on = o * lax.rsqrt(jnp.mean(o * o, axis=-1, keepdims=True) + EPS) * gn_ref[...]
        z = z_ref[rs, h * hd:(h + 1) * hd]
        o_ref[rs, h * hd:(h + 1) * hd] = (on * (z * _sigmoid(z))).astype(o_ref.dtype)
    for h in range(DN_HEADS):
        s_scr[h] = state[h]

    @pl.when(j == pl.num_programs(1) - 1)
    def _():
        sout_ref[...] = s_scr[...]


def _dn_stage(x3, small3, z3, conv0, s0, dw, chunk, tb, t_valid, precise=False):
    b, tpad, w3 = x3.shape
    dnw = w3 // 3
    nblk = tpad // tb
    full = lambda a: pl.BlockSpec(a.shape, lambda bi, j: (0,) * a.ndim)
    tok = lambda w: pl.BlockSpec((None, tb, w), lambda bi, j: (bi, j, 0))
    o, s_out = pl.pallas_call(
        functools.partial(_dn_kernel, chunk=chunk, tb=tb, t_valid=t_valid, precise=precise),
        grid=(b, nblk),
        in_specs=[tok(w3), tok(LANES), tok(dnw), full(dw["cw"]),
                  pl.BlockSpec((None, SUBLANES, w3), lambda bi, j: (bi, 0, 0)),
                  pl.BlockSpec((None, DN_HEADS, DN_HEAD_DIM, DN_HEAD_DIM), lambda bi, j: (bi, 0, 0, 0)),
                  full(dw["alog"]), full(dw["dtb"]), full(dw["gn"])],
        out_specs=(tok(dnw),
                   pl.BlockSpec((None, DN_HEADS, DN_HEAD_DIM, DN_HEAD_DIM), lambda bi, j: (bi, 0, 0, 0))),
        out_shape=(jax.ShapeDtypeStruct((b, tpad, dnw), F32 if precise else BF16),
                   jax.ShapeDtypeStruct((b, DN_HEADS, DN_HEAD_DIM, DN_HEAD_DIM), F32)),
        scratch_shapes=[pltpu.VMEM((DN_HEADS, DN_HEAD_DIM, DN_HEAD_DIM), F32),
                        pltpu.VMEM((tb + SUBLANES, w3), F32)],
        compiler_params=_cparams(("parallel", "arbitrary")),
    )(x3, small3, z3, dw["cw"], conv0, s0, dw["alog"], dw["dtb"], dw["gn"])
    return o, s_out


def _merge_kernel(x_ref, oa_ref, ob_ref, mg_ref, wpa_ref, wpb_ref, wout_ref, gf_ref, wr_ref, br_ref,
                  x1_ref, h_ref, route_ref, *, precise):
    d = x_ref.shape[1]
    mixed = (mg_ref[:, 0:d] * _mm(oa_ref[...], wpa_ref[...], precise)
             + mg_ref[:, d:2 * d] * _mm(ob_ref[...], wpb_ref[...], precise))
    x1 = x_ref[...] + _mm(mixed, wout_ref[...], precise)
    x1_ref[...] = x1
    h = (x1 * lax.rsqrt(jnp.mean(x1 * x1, axis=-1, keepdims=True) + EPS)) * gf_ref[...]
    h_ref[...] = h
    logits = _mm(h, wr_ref[...], precise) + br_ref[...]
    lane = lax.broadcasted_iota(jnp.int32, logits.shape, 1)
    lanef = lane.astype(F32)
    big = float(LANES)
    is_g = lane < N_GROUPS
    lg = jnp.where(is_g, logits, NEG)
    eg = jnp.where(is_g, jnp.exp(lg - jnp.max(lg, axis=-1, keepdims=True)), 0.0)
    pg = eg / jnp.sum(eg, axis=-1, keepdims=True)
    p_top = jnp.max(pg, axis=-1, keepdims=True)
    grp = jnp.min(jnp.where(jnp.logical_and(is_g, pg == p_top), lanef, big), axis=-1, keepdims=True)
    e_lo = EXPERT_LANE0 + grp * EXPERTS_PER_GROUP
    in_grp = jnp.logical_and(lanef >= e_lo, lanef < e_lo + EXPERTS_PER_GROUP)
    le = jnp.where(in_grp, logits, -jnp.inf)
    v0 = jnp.max(le, axis=-1, keepdims=True)
    i0 = jnp.min(jnp.where(le == v0, lanef, big), axis=-1, keepdims=True)
    le1 = jnp.where(lanef == i0, -jnp.inf, le)
    v1 = jnp.max(le1, axis=-1, keepdims=True)
    i1 = jnp.min(jnp.where(le1 == v1, lanef, big), axis=-1, keepdims=True)
    e1 = jnp.exp(v1 - v0)
    den = 1.0 + e1
    w0 = p_top * (1.0 / den)
    w1 = p_top * (e1 / den)
    route = jnp.where(lane == 0, w0, jnp.where(lane == 1, w1, jnp.where(
        lane == 2, i0 - EXPERT_LANE0, jnp.where(lane == 3, i1 - EXPERT_LANE0, 0.0))))
    route_ref[...] = route


def _merge_stage(x2d, oa, ob, mg, mw, tm, precise=False):
    m, d = x2d.shape
    row = lambda w: pl.BlockSpec((tm, w), lambda i: (i, 0))
    full = lambda a: pl.BlockSpec(a.shape, lambda i: (0,) * a.ndim)
    consts = (mw["wpa"], mw["wpb"], mw["wout"], mw["gf"], mw["wr"], mw["br"])
    return pl.pallas_call(
        functools.partial(_merge_kernel, precise=precise),
        grid=(m // tm,),
        in_specs=[row(d), row(oa.shape[1]), row(ob.shape[1]), row(2 * d)] + [full(a) for a in consts],
        out_specs=(row(d), row(d), row(LANES)),
        out_shape=(jax.ShapeDtypeStruct((m, d), F32), jax.ShapeDtypeStruct((m, d), F32),
                   jax.ShapeDtypeStruct((m, LANES), F32)),
        compiler_params=_cparams(("parallel",)),
    )(x2d, oa, ob, mg, *consts)


def _row_copy(src_ref, dst_ref, sem, src_row, dst_row):
    return pltpu.make_async_copy(src_ref.at[pl.ds(src_row, 1), :], dst_ref.at[pl.ds(dst_row, 1), :], sem)


def _expert_kernel(be_ref, lo_ref, tok_ref, nblk_ref, h_ref, wg_ref, wu_ref, wd_ref, y_ref, buf_ref, sem, *, blk):
    i = pl.program_id(0)
    n_used = nblk_ref[0]
    slot = i % 2

    @pl.when(i == 0)
    def _():
        def start(r, c):
            _row_copy(h_ref, buf_ref.at[0], sem.at[0], tok_ref[lo_ref[0] + r], r).start()
            return c
        lax.fori_loop(0, blk, start, 0, unroll=8)

    def step(prefetch):
        _wait_all(buf_ref.at[slot], sem.at[slot])
        if prefetch:
            lo = lo_ref[i + 1]
            for r in range(blk):
                _row_copy(h_ref, buf_ref.at[1 - slot], sem.at[1 - slot], tok_ref[lo + r], r).start()
        xb = buf_ref[slot].astype(BF16)
        gate = _dot(xb, wg_ref[...])
        up = _dot(xb, wu_ref[...])
        mid = (gate * _sigmoid(gate)) * up
        y_ref[...] = _dot(mid.astype(BF16), wd_ref[...])

    @pl.when(i + 1 < n_used)
    def _():
        step(True)

    @pl.when(i + 1 == n_used)
    def _():
        step(False)

    @pl.when(i >= n_used)
    def _():
        y_ref[...] = jnp.zeros(y_ref.shape, F32)


def _expert_stage(h2d, blk_expert, blk_lo, s_tok, n_used, ew, blk):
    m, d = h2d.shape
    n_blocks = blk_expert.shape[0]
    de = ew["wg"].shape[2]
    wspec = lambda s: pl.BlockSpec((None,) + s, lambda i, be, lo, tok, nb: (be[i], 0, 0))
    grid_spec = pltpu.PrefetchScalarGridSpec(
        num_scalar_prefetch=4,
        grid=(n_blocks,),
        in_specs=[pl.BlockSpec(memory_space=pl.ANY), wspec((d, de)), wspec((d, de)), wspec((de, d))],
        out_specs=pl.BlockSpec((blk, d), lambda i, be, lo, tok, nb: (i, 0)),
        scratch_shapes=[pltpu.VMEM((2, blk, d), F32), pltpu.SemaphoreType.DMA((2,))],
    )
    return pl.pallas_call(
        functools.partial(_expert_kernel, blk=blk),
        grid_spec=grid_spec,
        out_shape=jax.ShapeDtypeStruct((n_blocks * blk, d), F32),
        compiler_params=_cparams(("arbitrary",)),
    )(blk_expert, blk_lo, s_tok, n_used, h2d, ew["wg"], ew["wu"], ew["wd"])


def _combine_kernel(pos_ref, x1_ref, route_ref, y_ref, out_ref, buf_ref, sem, *, tm):
    i = pl.program_id(0)
    slot = i % 2

    @pl.when(i == 0)
    def _():
        def start(r, c):
            for k in range(TOP_K):
                _row_copy(y_ref, buf_ref.at[0, k], sem.at[0], pos_ref[r * TOP_K + k], r).start()
            return c
        lax.fori_loop(0, tm, start, 0, unroll=4)

    _wait_all(buf_ref.at[slot], sem.at[slot])

    @pl.when(i + 1 < pl.num_programs(0))
    def _():
        for r in range(tm):
            for k in range(TOP_K):
                _row_copy(y_ref, buf_ref.at[1 - slot, k], sem.at[1 - slot],
                          pos_ref[((i + 1) * tm + r) * TOP_K + k], r).start()

    w = route_ref[...]
    out_ref[...] = x1_ref[...] + (buf_ref[slot, 0] * w[:, 0:1] + buf_ref[slot, 1] * w[:, 1:2])


def _combine_stage(x1, route, y_sorted, pos, tm):
    m, d = x1.shape
    grid_spec = pltpu.PrefetchScalarGridSpec(
        num_scalar_prefetch=1,
        grid=(m // tm,),
        in_specs=[pl.BlockSpec((tm, d), lambda i, p: (i, 0)), pl.BlockSpec((tm, LANES), lambda i, p: (i, 0)),
                  pl.BlockSpec(memory_space=pl.ANY)],
        out_specs=pl.BlockSpec((tm, d), lambda i, p: (i, 0)),
        scratch_shapes=[pltpu.VMEM((2, TOP_K, tm, d), F32), pltpu.SemaphoreType.DMA((2,))],
    )
    return pl.pallas_call(
        functools.partial(_combine_kernel, tm=tm),
        grid_spec=grid_spec,
        out_shape=jax.ShapeDtypeStruct((m, d), F32),
        compiler_params=_cparams(("arbitrary",)),
    )(pos, x1, route, y_sorted)


def _moe_stage(x1, h2d, route, ew, blk, tm):
    m, d = h2d.shape
    n_exp = ew["wg"].shape[0]
    n_assign = m * TOP_K
    i32 = jnp.int32
    flat_e = route[:, TOP_K:2 * TOP_K].astype(i32).reshape(n_assign)
    order = jnp.argsort(flat_e).astype(i32)
    rank = jnp.argsort(order).astype(i32)
    counts = jnp.sum(flat_e[None, :] == jnp.arange(n_exp, dtype=i32)[:, None], axis=1, dtype=i32)
    padded = (counts + blk - 1) // blk * blk
    pad_end = jnp.cumsum(padded)
    pad_start = pad_end - padded
    start = jnp.cumsum(counts) - counts
    n_blocks = (n_assign + n_exp * (blk - 1) + blk - 1) // blk
    blk_first = jnp.arange(n_blocks, dtype=i32) * blk
    blk_expert = jnp.minimum(jnp.sum(pad_end[None, :] <= blk_first[:, None], axis=1, dtype=i32), n_exp - 1)
    delta = start - pad_start
    blk_lo = jnp.clip(blk_first + delta[blk_expert], 0, n_assign).astype(i32)
    s_tok = jnp.concatenate([order // TOP_K, jnp.zeros((blk,), i32)])
    onehot = flat_e[:, None] == jnp.arange(n_exp, dtype=i32)[None, :]
    pos = (rank - jnp.sum(jnp.where(onehot, delta[None, :], 0), axis=1, dtype=i32)).astype(i32)
    n_used = (pad_end[n_exp - 1:n_exp] // blk).astype(i32)
    y_sorted = _expert_stage(h2d, blk_expert, blk_lo, s_tok, n_used, ew, blk)
    return _combine_stage(x1, route, y_sorted, pos, tm)


def _rope_tables(pos):
    inv_freq = ROPE_THETA ** (-jnp.arange(ROPE_HALF, dtype=F32) / ROPE_HALF)
    ang = pos.astype(F32)[:, None] * inv_freq[None, :]
    cos, sin = jnp.cos(ang), jnp.sin(ang)
    n = pos.shape[0]
    zeros = lambda k: jnp.zeros((n, k), F32)
    c = jnp.concatenate([cos, cos, jnp.ones((n, HEAD_DIM - ROPE_DIM), F32)], axis=1)
    s1 = jnp.concatenate([-sin, zeros(HEAD_DIM - ROPE_HALF)], axis=1)
    s2 = jnp.concatenate([zeros(ROPE_HALF), sin, zeros(HEAD_DIM - ROPE_DIM)], axis=1)
    return tuple(jnp.tile(a, (1, LANES // HEAD_DIM)) for a in (c, s1, s2))


def _block_diag_ones():
    i = np.arange(LANES)
    return jnp.asarray((i[:, None] // HEAD_DIM == i[None, :] // HEAD_DIM).astype(np.float32), BF16)


def _pack_proj(w_in, norm_mix, q_norm, k_norm, dnw):
    d = w_in.shape[0]
    qc = NSA_HEADS * HEAD_DIM
    kvc = NSA_KV_HEADS * HEAD_DIM
    sizes = (qc,) + (kvc,) * 6 + (NSA_HEADS * 3, 3 * dnw, DN_HEADS, DN_HEADS, dnw, d, d)
    o = np.concatenate([[0], np.cumsum(sizes)])
    seg = lambda i: w_in[:, int(o[i]):int(o[i + 1])]
    wq = seg(0).reshape(d, NSA_HEADS, HEAD_DIM)[:, np.asarray(HEAD_PERM)].reshape(d, qc)
    small = jnp.concatenate([seg(9), seg(10), seg(7), jnp.zeros((d, LANES - 2 * DN_HEADS - 3 * NSA_HEADS), F32)], axis=1)
    w = jnp.concatenate([wq, seg(1), seg(2), seg(3), seg(4), seg(5), seg(6), small, seg(8), seg(11), seg(12), seg(13)],
                        axis=1)
    offs, ncols = _proj_layout(dnw, d)
    assert ncols == w.shape[1]
    rep = LANES // HEAD_DIM
    return dict(w=w.astype(BF16), w32=w, offs=offs, ncols=ncols, gin=norm_mix[None, :],
                qg=jnp.tile(q_norm, rep)[None, :], kg=jnp.tile(k_norm[0:2], (1, rep)), bd=_block_diag_ones())


def _pack_compress(w_cmp, cmp_pos, k_norm):
    def bdiag(w):
        z = jnp.zeros_like(w)
        return jnp.concatenate([jnp.concatenate([w, z], axis=2), jnp.concatenate([z, w], axis=2)], axis=1)

    s = CMP_STRIDE
    halves = [(0, slice(0, s)), (0, slice(s, 2 * s)), (1, slice(0, s)), (1, slice(s, 2 * s))]
    return dict(w=jnp.stack([bdiag(w_cmp[i, sl]) for i, sl in halves]).astype(BF16),
                p=jnp.stack([jnp.tile(cmp_pos[i, sl], (1, LANES // HEAD_DIM)) for i, sl in halves]),
                kg=jnp.tile(k_norm[2], LANES // HEAD_DIM)[None, :], bd=_block_diag_ones())


def _overlap(nb, ns_pad, n_cmp, n_sel):
    cs = np.arange(nb)[:, None] * CMP_STRIDE
    ss = np.arange(ns_pad)[None, :] * SEL_BLOCK
    ov = (cs <= ss + SEL_BLOCK - 1) & (cs + CMP_BLOCK - 1 >= ss)
    ov &= (np.arange(nb)[:, None] < n_cmp) & (np.arange(ns_pad)[None, :] < n_sel)
    return jnp.asarray(ov.astype(np.float32), BF16)


def _pad_lanes(v, n=LANES):
    return jnp.zeros((1, n), F32).at[0, :v.shape[0]].set(v)


def _round_up(x, n):
    return (x + n - 1) // n * n


def kernel(x_prompt, x_sample, cache_nsa, page_table, state_win, state_dn_conv, state_dn_S, norm_mix, w_in, q_norm, k_norm, cmp_pos, w_cmp, dn_conv_w, dn_A_log, dn_dt_bias, dn_norm, w_proj_a, w_proj_b, w_out, norm_ffn, w_router_g, b_router_g, w_router_e, b_router_e, w_gate, w_up, w_down):
    b, t, d = x_prompt.shape
    bd_, tn, _ = x_sample.shape
    depth = w_in.shape[0]
    assert depth == 1 and tn == 1, "one layer, one new token per sample sequence"
    n_pool, page = cache_nsa.shape[1], cache_nsa.shape[2]
    n_pages = page_table.shape[1]
    past = n_pages * page
    win_rows = state_win.shape[2]
    dnw = dn_conv_w.shape[2] // 3
    assert t % Q_BLOCK == 0 and t >= WINDOW + Q_BLOCK and t % SEL_CHUNK == 0 and past % CMP_STRIDE == 0
    l = 0

    pw = _pack_proj(w_in[l], norm_mix[l], q_norm[l], k_norm[l], dnw)
    cw = _pack_compress(w_cmp[l], cmp_pos[l], k_norm[l])
    dw = dict(cw=dn_conv_w[l], alog=_pad_lanes(dn_A_log[l]), dtb=_pad_lanes(dn_dt_bias[l]), gn=dn_norm[l][None, :])
    perm = np.asarray(HEAD_PERM)
    wr = jnp.zeros((d, LANES), F32).at[:, 0:N_GROUPS].set(w_router_g[l])
    wr = wr.at[:, EXPERT_LANE0:EXPERT_LANE0 + w_router_e.shape[2]].set(w_router_e[l])
    br = jnp.zeros((1, LANES), F32).at[0, 0:N_GROUPS].set(b_router_g[l])
    br = br.at[0, EXPERT_LANE0:EXPERT_LANE0 + b_router_e.shape[1]].set(b_router_e[l])
    mw32 = dict(wpa=w_proj_a[l].reshape(NSA_HEADS, HEAD_DIM, d)[perm].reshape(NSA_HEADS * HEAD_DIM, d),
                wpb=w_proj_b[l], wout=w_out[l], gf=norm_ffn[l][None, :], wr=wr, br=br)
    mw = {k: (v.astype(BF16) if k in ("wpa", "wpb", "wout", "wr") else v) for k, v in mw32.items()}
    ew = dict(wg=w_gate[l].astype(BF16), wu=w_up[l].astype(BF16), wd=w_down[l].astype(BF16))

    tm = 256
    tabs_p = _rope_tables(jnp.arange(t))
    q_p, rows_p, win_p, kvb_p, small_p, dnqkv_p, dnz_p, mg_p = _proj_stage(
        x_prompt.reshape(b * t, d), pw, tabs_p, PROJ_TM, t // PROJ_TM)
    nb_p = t // CMP_STRIDE
    n_cmp_p = (t - CMP_BLOCK) // CMP_STRIDE + 1
    n_sel_p = -(-t // SEL_BLOCK)
    tabs_cp = _rope_tables(jnp.arange(nb_p) * CMP_STRIDE + CMP_BLOCK - 1)
    kcv_p = _compress_prompt(rows_p.reshape(b, t, 4 * LANES), cw, tabs_cp)
    ovl_p = _overlap(nb_p, _round_up(n_sel_p, LANES), n_cmp_p, n_sel_p)
    oa_p = _attn_prompt(q_p, small_p, kvb_p.reshape(b, t, 4 * LANES), kcv_p, ovl_p, n_cmp_p)
    tb = DN_TB
    ob_p, s_p = _dn_stage(dnqkv_p.reshape(b, t, 3 * dnw), small_p.reshape(b, t, LANES), dnz_p.reshape(b, t, dnw),
                          jnp.zeros((b, SUBLANES, 3 * dnw), F32),
                          jnp.zeros((b, DN_HEADS, DN_HEAD_DIM, DN_HEAD_DIM), F32), dw, DN_CHUNK, tb, t)
    x1_p, h_p, route_p = _merge_stage(x_prompt.reshape(b * t, d), oa_p, ob_p.reshape(b * t, dnw), mg_p, mw, tm)
    y_p = _moe_stage(x1_p, h_p, route_p, ew, 256, tm).reshape(b, t, d)
    kv_p = rows_p.reshape(1, b, t, 4, NSA_KV_HEADS, HEAD_DIM)
    win_all = win_p.reshape(b, t, 2, NSA_KV_HEADS, HEAD_DIM)
    win_out_p = jnp.pad(win_all, ((0, 0), (max(0, win_rows - t), 0), (0, 0), (0, 0), (0, 0)))[:, -win_rows:][None]
    conv_p = dnqkv_p.reshape(b, t, 3 * dnw)[:, t - (DN_CONV - 1):][None]

    tabs_s = _rope_tables(jnp.full((bd_,), past, jnp.int32))
    q_s, rows_s, win_s, _, small_s, dnqkv_s, dnz_s, mg_s = _proj_stage(
        x_sample.reshape(bd_, d), dict(pw, w=pw["w32"]), tabs_s, bd_, 1, precise=True)
    cache3 = cache_nsa[l].reshape(n_pool, page, 4 * LANES)
    nb_s = past // CMP_STRIDE
    n_cmp_s = (past + tn - CMP_BLOCK) // CMP_STRIDE + 1
    n_sel_s = -(-(past + tn) // SEL_BLOCK)
    assert n_cmp_s == nb_s - 1 and n_cmp_p == nb_p - 1 and (1 << SEL_SHIFT) == SEL_BLOCK
    tabs_cs = _rope_tables(jnp.arange(nb_s) * CMP_STRIDE + CMP_BLOCK - 1)
    kcv_s = _compress_sample(cache3, page_table, cw, tabs_cs)
    ovl_s = _overlap(nb_s, _round_up(n_sel_s, LANES), n_cmp_s, n_sel_s)
    swin3 = state_win[l].reshape(bd_, win_rows, 2 * LANES)
    oa_s = _attn_sample(cache3, page_table, q_s, small_s, rows_s, win_s, kcv_s, swin3, ovl_s, n_cmp_s, n_sel_s)
    pad_t = lambda a: jnp.pad(a[:, None, :], ((0, 0), (0, SUBLANES - tn), (0, 0)))
    conv0_s = jnp.pad(state_dn_conv[l], ((0, 0), (SUBLANES - (DN_CONV - 1), 0), (0, 0)))
    ob_s, s_s = _dn_stage(pad_t(dnqkv_s), pad_t(small_s), pad_t(dnz_s), conv0_s, state_dn_S[l], dw,
                          SUBLANES, SUBLANES, tn, precise=True)
    x1_s, h_s, route_s = _merge_stage(x_sample.reshape(bd_, d), oa_s, ob_s[:, 0, :], mg_s, mw32, bd_, precise=True)
    y_s = _moe_stage(x1_s, h_s, route_s, ew, 64, bd_).reshape(bd_, tn, d)
    kv_s = rows_s.reshape(1, bd_, tn, 4, NSA_KV_HEADS, HEAD_DIM)
    win_new = win_s.reshape(bd_, tn, 2, NSA_KV_HEADS, HEAD_DIM)
    win_out_s = jnp.concatenate([state_win[l], win_new], axis=1)[:, -win_rows:][None]
    conv_s = jnp.concatenate([state_dn_conv[l], dnqkv_s[:, None, :]], axis=1)[:, -(DN_CONV - 1):][None]

    return (y_p, y_s, kv_p, kv_s, win_out_p, win_out_s, conv_p, conv_s, s_p[None], s_s[None])
```

```python
import functools
import math

import numpy as np
import jax
import jax.numpy as jnp
from jax import lax
from jax.experimental import pallas as pl
from jax.experimental.pallas import tpu as pltpu

F32 = jnp.float32
BF16 = jnp.bfloat16

NSA_HEADS = 8
NSA_KV_HEADS = 2
NSA_GROUP = NSA_HEADS // NSA_KV_HEADS
HEAD_DIM = 64
ROPE_DIM = HEAD_DIM // 4
ROPE_HALF = ROPE_DIM // 2
ROPE_THETA = 500000.0
CMP_BLOCK = 32
CMP_STRIDE = 16
SEL_BLOCK = 64
SEL_SHIFT = 6
SEL_TOPN = 16
WINDOW = 512
FORCE_BONUS = 1000.0
DN_HEADS = 4
DN_HEAD_DIM = 128
DN_CONV = 4
DN_CHUNK = 64
N_GROUPS = 4
EXPERTS_PER_GROUP = 8
TOP_K = 2
EPS = 1e-6

LANES = 128
SUBLANES = 8
VMEM_LIMIT = 56 * 1024 * 1024
NEG = -1e30
Q_BLOCK = 128
SEL_CHUNK = 1024
PROJ_TM = 512
MERGE_TM = 512
DN_TB = 512
SHIFT_SLACK = 1.0 + 2.0 ** -5
SHIFT_MIN_SUM = 1e-20
HEAD_PERM = (0, 4, 1, 5, 2, 6, 3, 7)
SM_A, SM_B, SM_GATE = 0, DN_HEADS, 2 * DN_HEADS
EXPERT_LANE0 = 32


def _cparams(sem):
    return pltpu.CompilerParams(dimension_semantics=sem, vmem_limit_bytes=VMEM_LIMIT)


def _dot(a, b):
    return jnp.dot(a, b, preferred_element_type=F32)


def _dot_nt(a, b):
    return lax.dot_general(a, b, (((1,), (1,)), ((), ())), preferred_element_type=F32)


def _dot_tn(a, b):
    return lax.dot_general(a, b, (((0,), (0,)), ((), ())), preferred_element_type=F32)


def _split2(x):
    hi = x.astype(BF16)
    lo = (x - hi.astype(F32)).astype(BF16)
    return hi, lo


def _split3(x):
    hi = x.astype(BF16)
    r = x - hi.astype(F32)
    mid = r.astype(BF16)
    lo = (r - mid.astype(F32)).astype(BF16)
    return hi, mid, lo


def _dot3s(a, b):
    return _dot(a[0], b[0]) + (_dot(a[0], b[1]) + _dot(a[1], b[0]))


def _mm(a, b, precise, form=_dot):
    if not precise:
        return form(a.astype(BF16), b.astype(BF16))
    ah, al = _split2(a.astype(F32))
    bh, bl = _split2(b.astype(F32))
    return form(ah, bh) + (form(ah, bl) + form(al, bh))


def _sigmoid(x):
    return 1.0 / (1.0 + jnp.exp(-x))


def _seg_sumsq(x, bd):
    hi, lo = _split2(x * x)
    return _dot(hi, bd) + _dot(lo, bd)


def _rope128(x, c, s1, s2):
    return x * c + pltpu.roll(x, LANES - ROPE_HALF, 1) * s1 + pltpu.roll(x, ROPE_HALF, 1) * s2


def _head_norm_rope(v, gain, bd, c, s1, s2):
    ss = _seg_sumsq(v, bd) * (1.0 / HEAD_DIM)
    return _rope128(v * lax.rsqrt(ss + EPS) * gain, c, s1, s2)


def _masked_softmax(s, valid):
    s = jnp.where(valid, s, NEG)
    m = jnp.max(s, axis=-1, keepdims=True)
    e = jnp.where(valid, jnp.exp(s - m), 0.0)
    l = jnp.sum(e, axis=-1, keepdims=True)
    return e * (1.0 / jnp.maximum(l, 1e-30))


def _topk_mask(score, k):
    lane = lax.broadcasted_iota(jnp.int32, score.shape, 1).astype(F32)
    big = float(score.shape[-1])

    def body(_, carry):
        sc, sel = carry
        m = jnp.max(sc, axis=-1, keepdims=True)
        idx = jnp.min(jnp.where(sc == m, lane, big), axis=-1, keepdims=True)
        hit = lane == idx
        sel = jnp.where(hit, jnp.maximum(sel, jnp.where(m > -jnp.inf, 1.0, 0.0)), sel)
        sc = jnp.where(hit, -jnp.inf, sc)
        return sc, sel

    _, sel = lax.fori_loop(0, k, body, (score, jnp.zeros(score.shape, F32)))
    return sel


def _topk_mask_rows(score, k):
    idx = lax.broadcasted_iota(jnp.int32, score.shape, 0).astype(F32)
    big = float(score.shape[0])

    def body(_, carry):
        sc, sel = carry
        m = jnp.max(sc, axis=0, keepdims=True)
        first = jnp.min(jnp.where(sc == m, idx, big), axis=0, keepdims=True)
        hit = idx == first
        sel = jnp.where(hit, jnp.maximum(sel, jnp.where(m > -jnp.inf, 1.0, 0.0)), sel)
        sc = jnp.where(hit, -jnp.inf, sc)
        return sc, sel

    _, sel = lax.fori_loop(0, k, body, (score, jnp.zeros(score.shape, F32)))
    return sel


def _proj_layout(dnw, d):
    sizes = dict(q=NSA_HEADS * HEAD_DIM, kvc=2 * LANES, ks=LANES, vs=LANES, kw=LANES, vw=LANES,
                 small=LANES, dnqkv=3 * dnw, dnz=dnw, mg=2 * d)
    offs, o = {}, 0
    for name, n in sizes.items():
        offs[name] = (o, o + n)
        o += n
    return offs, o


def _proj_kernel(x_ref, gin_ref, w_ref, c_ref, s1_ref, s2_ref, qg_ref, kg_ref, bd_ref,
                 q_out, rows_out, win_out, kvb_out, small_out, dnqkv_out, dnz_out, mg_out, *, offs, precise):
    x = x_ref[...]
    ms = jnp.mean(x * x, axis=-1, keepdims=True)
    hb = (x * lax.rsqrt(ms + EPS)) * gin_ref[...]
    if not precise:
        hb = hb.astype(BF16)

    def mm(name, lo=0, hi=None):
        a, b = offs[name]
        hi = b - a if hi is None else hi
        return _mm(hb, w_ref[:, a + lo:a + hi], precise)

    c, s1, s2 = c_ref[...], s1_ref[...], s2_ref[...]
    bd = bd_ref[...]
    q_all = mm("q")
    for j in range(NSA_HEADS * HEAD_DIM // LANES):
        qj = _head_norm_rope(q_all[:, LANES * j:LANES * (j + 1)], qg_ref[...], bd, c, s1, s2)
        q_out[:, LANES * j:LANES * (j + 1)] = (qj * (HEAD_DIM ** -0.5)).astype(BF16)
    rows_out[:, 0:2 * LANES] = mm("kvc")
    five = _mm(hb, w_ref[:, offs["ks"][0]:offs["small"][1]], precise)
    ks = _head_norm_rope(five[:, 0:LANES], kg_ref[0:1, :], bd, c, s1, s2)
    vs = five[:, LANES:2 * LANES]
    rows_out[:, 2 * LANES:3 * LANES] = ks
    rows_out[:, 3 * LANES:4 * LANES] = vs
    kw = _head_norm_rope(five[:, 2 * LANES:3 * LANES], kg_ref[1:2, :], bd, c, s1, s2)
    vw = five[:, 3 * LANES:4 * LANES]
    win_out[:, 0:LANES] = kw
    win_out[:, LANES:2 * LANES] = vw
    kvb_out[:, 0:LANES] = ks.astype(BF16)
    kvb_out[:, LANES:2 * LANES] = vs.astype(BF16)
    kvb_out[:, 2 * LANES:3 * LANES] = kw.astype(BF16)
    kvb_out[:, 3 * LANES:4 * LANES] = vw.astype(BF16)
    small_out[...] = five[:, 4 * LANES:5 * LANES]
    dnqkv_out[...] = mm("dnqkv")
    dnz_out[...] = mm("dnz")
    mg_out[...] = _sigmoid(mm("mg"))


def _proj_stage(x2d, pw, tabs, tm, n_tab_blocks, precise=False):
    m, d = x2d.shape
    offs, ncols = pw["offs"], pw["ncols"]
    dnw = offs["dnz"][1] - offs["dnz"][0]
    row = lambda w: pl.BlockSpec((tm, w), lambda i: (i, 0))
    full = lambda a: pl.BlockSpec(a.shape, lambda i: (0,) * a.ndim)
    tab = pl.BlockSpec((tm, LANES), lambda i: (i % n_tab_blocks, 0))
    out_shape = (
        jax.ShapeDtypeStruct((m, 4 * LANES), BF16),
        jax.ShapeDtypeStruct((m, 4 * LANES), F32),
        jax.ShapeDtypeStruct((m, 2 * LANES), F32),
        jax.ShapeDtypeStruct((m, 4 * LANES), BF16),
        jax.ShapeDtypeStruct((m, LANES), F32),
        jax.ShapeDtypeStruct((m, 3 * dnw), F32),
        jax.ShapeDtypeStruct((m, dnw), F32),
        jax.ShapeDtypeStruct((m, 2 * d), F32),
    )
    return pl.pallas_call(
        functools.partial(_proj_kernel, offs=offs, precise=precise),
        grid=(m // tm,),
        in_specs=[row(d), full(pw["gin"]),
                  pl.BlockSpec(pw["w"].shape, lambda i: (0, 0), pipeline_mode=pl.Buffered(1)),
                  tab, tab, tab, full(pw["qg"]), full(pw["kg"]), full(pw["bd"])],
        out_specs=tuple(row(s.shape[1]) for s in out_shape),
        out_shape=out_shape,
        compiler_params=_cparams(("parallel",)),
    )(x2d, pw["gin"], pw["w"], tabs[0], tabs[1], tabs[2], pw["qg"], pw["kg"], pw["bd"])


def _compress_body(xk_ref, xv_ref, w_ref, p_ref, kg_ref, bd_ref, c_ref, s1_ref, s2_ref, out_ref):
    nb = xk_ref.shape[0] // CMP_STRIDE
    acc = [jnp.zeros((nb, LANES), F32) for _ in range(4)]
    for l in range(CMP_STRIDE):
        xs = (xk_ref[pl.ds(l, nb, stride=CMP_STRIDE), :], xv_ref[pl.ds(l, nb, stride=CMP_STRIDE), :])
        for j in range(4):
            acc[j] = acc[j] + _dot((xs[j // 2] + p_ref[j, l:l + 1, :]).astype(BF16), w_ref[j, l])
    kraw = acc[0] + pltpu.roll(acc[1], nb - 1, 0)
    vraw = acc[2] + pltpu.roll(acc[3], nb - 1, 0)
    kc = _head_norm_rope(kraw, kg_ref[...], bd_ref[...], c_ref[...], s1_ref[...], s2_ref[...])
    out_ref[:, 0:LANES] = kc.astype(BF16)
    out_ref[:, LANES:2 * LANES] = vraw.astype(BF16)


def _compress_prompt_kernel(xk_ref, xv_ref, *rest):
    _compress_body(xk_ref, xv_ref, *rest)


def _compress_prompt(rows3, cw, tabs):
    b, t, _ = rows3.shape
    nb = t // CMP_STRIDE
    full = lambda a: pl.BlockSpec(a.shape, lambda i: (0,) * a.ndim)
    consts = (cw["w"], cw["p"], cw["kg"], cw["bd"]) + tuple(tabs)
    return pl.pallas_call(
        _compress_prompt_kernel,
        grid=(b,),
        in_specs=[pl.BlockSpec((None, t, LANES), lambda i: (i, 0, 0)),
                  pl.BlockSpec((None, t, LANES), lambda i: (i, 0, 1))] + [full(a) for a in consts],
        out_specs=pl.BlockSpec((None, nb, 2 * LANES), lambda i: (i, 0, 0)),
        out_shape=jax.ShapeDtypeStruct((b, nb, 2 * LANES), BF16),
        compiler_params=_cparams(("parallel",)),
    )(rows3, rows3, *consts)


def _wait_all(buf_view, sem):
    pltpu.make_async_copy(buf_view, buf_view, sem).wait()


def _start_pages(pt_ref, cache_ref, buf_ref, sem, b, slot, n_pages, page, lane0):
    split = len(buf_ref.shape) == 4

    def start(p, c):
        rows = pl.ds(pl.multiple_of(p * page, page), page)
        if split:
            for j in range(2):
                pltpu.make_async_copy(cache_ref.at[pt_ref[b, p], :, pl.ds(lane0 + j * LANES, LANES)],
                                      buf_ref.at[slot, j, rows, :], sem.at[slot]).start()
        else:
            pltpu.make_async_copy(cache_ref.at[pt_ref[b, p], :, pl.ds(lane0, 2 * LANES)],
                                  buf_ref.at[slot, rows, :], sem.at[slot]).start()
        return c

    lax.fori_loop(0, n_pages, start, 0)


def _prefetch_pages(pt_ref, cache_ref, buf_ref, sem, n_pages, page, lane0):
    b = pl.program_id(0)
    slot = b % 2

    @pl.when(b == 0)
    def _():
        _start_pages(pt_ref, cache_ref, buf_ref, sem, 0, 0, n_pages, page, lane0)

    @pl.when(b + 1 < pl.num_programs(0))
    def _():
        _start_pages(pt_ref, cache_ref, buf_ref, sem, b + 1, 1 - slot, n_pages, page, lane0)

    return slot


def _compress_sample_kernel(pt_ref, cache_ref, *rest, n_pages, page):
    *consts, out_ref, buf_ref, sem = rest
    slot = _prefetch_pages(pt_ref, cache_ref, buf_ref, sem, n_pages, page, 0)
    _wait_all(buf_ref.at[slot], sem.at[slot])
    _compress_body(buf_ref.at[slot, 0], buf_ref.at[slot, 1], *consts, out_ref)


def _compress_sample(cache3, page_table, cw, tabs):
    bd_, n_pages = page_table.shape
    page = cache3.shape[1]
    past = n_pages * page
    nb = past // CMP_STRIDE
    full = lambda a: pl.BlockSpec(a.shape, lambda i, pt: (0,) * a.ndim)
    consts = (cw["w"], cw["p"], cw["kg"], cw["bd"]) + tuple(tabs)
    grid_spec = pltpu.PrefetchScalarGridSpec(
        num_scalar_prefetch=1,
        grid=(bd_,),
        in_specs=[pl.BlockSpec(memory_space=pl.ANY)] + [full(a) for a in consts],
        out_specs=pl.BlockSpec((None, nb, 2 * LANES), lambda i, pt: (i, 0, 0)),
        scratch_shapes=[pltpu.VMEM((2, 2, past, LANES), F32), pltpu.SemaphoreType.DMA((2,))],
    )
    return pl.pallas_call(
        functools.partial(_compress_sample_kernel, n_pages=n_pages, page=page),
        grid_spec=grid_spec,
        out_shape=jax.ShapeDtypeStruct((bd_, nb, 2 * LANES), BF16),
        compiler_params=_cparams(("arbitrary",)),
    )(page_table, cache3, *consts)


def _stack_heads(q, lo_half):
    zero = jnp.zeros((), q.dtype)
    parts = []
    for g in range(NSA_KV_HEADS):
        for r in range(NSA_GROUP):
            col = q[:, LANES * r:LANES * (r + 1)]
            parts.append(jnp.where(lo_half if g == 0 else jnp.logical_not(lo_half), col, zero))
    return jnp.concatenate(parts, axis=0)


def _sel_scores(imp, pos_q, blk):
    cur = pos_q >> SEL_SHIFT
    readable = blk * SEL_BLOCK <= pos_q
    forced = jnp.logical_or(blk == 0, jnp.logical_or(blk == cur, blk == cur - 1))
    return jnp.where(readable, imp + jnp.where(forced, FORCE_BONUS, 0.0), -jnp.inf)


def _attn_prompt_kernel(q_ref, small_ref, kvb_ref, kcv_ref, ovl_ref, bd_ref, o_ref, kmx_ref, os_ref, *, n_cmp):
    qb = Q_BLOCK
    m8 = NSA_HEADS * qb
    i = pl.program_id(1)
    q0 = i * qb

    @pl.when(i == 0)
    def _():
        def body(c, mx):
            kf = kvb_ref[pl.ds(pl.multiple_of(c * SEL_CHUNK, SEL_CHUNK), SEL_CHUNK), 0:LANES].astype(F32)
            return jnp.maximum(mx, jnp.max(_seg_sumsq(kf, bd_ref[...]), axis=0, keepdims=True))
        kmx_ref[...] = lax.fori_loop(0, kvb_ref.shape[0] // SEL_CHUNK, body, jnp.zeros((1, LANES), F32))

    lane = lax.broadcasted_iota(jnp.int32, (qb, LANES), 1)
    lo_half = lane < HEAD_DIM
    q8 = _stack_heads(q_ref[...], lo_half)

    def qpos(n):
        return q0 + (lax.broadcasted_iota(jnp.int32, (m8, n), 0) & (qb - 1))

    def kidx(n):
        return lax.broadcasted_iota(jnp.int32, (m8, n), 1)

    nb = kcv_ref.shape[0]
    s = _dot_nt(q8, kcv_ref[:, 0:LANES])
    cidx = kidx(nb)
    valid = jnp.logical_and(cidx < n_cmp, cidx * CMP_STRIDE + (CMP_BLOCK - 1) <= qpos(nb))
    p = _masked_softmax(s, valid).astype(BF16)
    o_c = _dot(p, kcv_ref[:, LANES:2 * LANES])
    imp8 = _dot(p, ovl_ref[...])

    ns = ovl_ref.shape[1]
    imps = []
    for g in range(NSA_KV_HEADS):
        acc = imp8[(g * NSA_GROUP) * qb:(g * NSA_GROUP + 1) * qb]
        for r in range(1, NSA_GROUP):
            acc = acc + imp8[(g * NSA_GROUP + r) * qb:(g * NSA_GROUP + r + 1) * qb]
        imps.append(acc)
    imp = jnp.concatenate(imps, axis=0)
    pos2 = q0 + (lax.broadcasted_iota(jnp.int32, (2 * qb, ns), 0) & (qb - 1))
    blk2 = lax.broadcasted_iota(jnp.int32, (2 * qb, ns), 1)
    sel = jnp.transpose(_topk_mask_rows(jnp.transpose(_sel_scores(imp, pos2, blk2)), SEL_TOPN))
    sel8 = jnp.concatenate([sel[0:qb]] * NSA_GROUP + [sel[qb:2 * qb]] * NSA_GROUP, axis=0) > 0.0

    ck = SEL_CHUNK
    qpos_k = qpos(ck)
    kk = lax.broadcasted_iota(jnp.int32, (ck, ns), 0)
    ss = lax.broadcasted_iota(jnp.int32, (ck, ns), 1)
    n_chunks = (q0 + qb + ck - 1) // ck

    def scores(qaug, c, causal):
        k0 = pl.multiple_of(c * ck, ck)
        e = jnp.where(((k0 + kk) >> SEL_SHIFT) == ss, 1.0, 0.0).astype(BF16)
        sc = _dot_nt(qaug, jnp.concatenate([kvb_ref[pl.ds(k0, ck), 0:LANES], e], axis=1))
        if causal:
            sc = jnp.where(k0 + kidx(ck) <= qpos_k, sc, NEG)
        return sc, kvb_ref[pl.ds(k0, ck), LANES:2 * LANES]

    q8f = q8.astype(F32)
    qn2 = jnp.sum(q8f * q8f, axis=-1, keepdims=True)
    kmx = kmx_ref[...]
    half = NSA_GROUP * qb
    kq2 = jnp.concatenate([qn2[0:half] * kmx[:, 0:1], qn2[half:] * kmx[:, HEAD_DIM:HEAD_DIM + 1]], axis=0)
    shift = jnp.sqrt(kq2) * SHIFT_SLACK + 1e-6
    qaug = jnp.concatenate([q8, jnp.where(sel8, -shift, NEG).astype(BF16)], axis=1)

    def chunk_shifted(c, carry, causal):
        l, acc = carry
        sc, vmat = scores(qaug, c, causal)
        pe = jnp.exp(sc)
        return l + jnp.sum(pe, axis=-1, keepdims=True), acc + _dot(pe.astype(BF16), vmat)

    carry = lax.fori_loop(0, n_chunks - 1, functools.partial(chunk_shifted, causal=False),
                          (jnp.zeros((m8, 1), F32), jnp.zeros((m8, LANES), F32)))
    l_s, acc_s = chunk_shifted(n_chunks - 1, carry, True)
    os_ref[...] = acc_s * (1.0 / jnp.maximum(l_s, 1e-37))

    @pl.when(jnp.min(l_s) < SHIFT_MIN_SUM)
    def _():
        qaug_x = jnp.concatenate([q8, jnp.where(sel8, 0.0, NEG).astype(BF16)], axis=1)

        def chunk_online(c, carry, causal):
            m, l, acc = carry
            sc, vmat = scores(qaug_x, c, causal)
            m_new = jnp.maximum(m, jnp.max(sc, axis=-1, keepdims=True))
            alpha = jnp.exp(m - m_new)
            pe = jnp.exp(sc - m_new)
            return (m_new, alpha * l + jnp.sum(pe, axis=-1, keepdims=True),
                    alpha * acc + _dot(pe.astype(BF16), vmat))

        init = (jnp.full((m8, 1), NEG, F32), jnp.zeros((m8, 1), F32), jnp.zeros((m8, LANES), F32))
        carry_x = lax.fori_loop(0, n_chunks - 1, functools.partial(chunk_online, causal=False), init)
        _, l_x, acc_x = chunk_online(n_chunks - 1, carry_x, True)
        os_ref[...] = acc_x * (1.0 / l_x)

    o_s = os_ref[...]

    wk = WINDOW + qb
    ws = pl.multiple_of(jnp.maximum(q0 - WINDOW, 0), qb)
    s = _dot_nt(q8, kvb_ref[pl.ds(ws, wk), 2 * LANES:3 * LANES])
    dpos = qpos(wk) - (ws + kidx(wk))
    p = _masked_softmax(s, jnp.logical_and(dpos >= 0, dpos < WINDOW)).astype(BF16)
    o_w = _dot(p, kvb_ref[pl.ds(ws, wk), 3 * LANES:4 * LANES])

    sig = _sigmoid(small_ref[...])
    for r in range(NSA_GROUP):
        outs = []
        for g in range(NSA_KV_HEADS):
            h = g * NSA_GROUP + r
            rows = slice(h * qb, (h + 1) * qb)
            gl = SM_GATE + 3 * h
            outs.append(sig[:, gl:gl + 1] * o_c[rows] + sig[:, gl + 1:gl + 2] * o_s[rows]
                        + sig[:, gl + 2:gl + 3] * o_w[rows])
        o_ref[:, LANES * r:LANES * (r + 1)] = jnp.where(lo_half, outs[0], outs[1]).astype(BF16)


def _attn_prompt(q2d, small2d, kvb3, kcv3, ovl, n_cmp):
    b, t, _ = kvb3.shape
    nq = t // Q_BLOCK
    nb = kcv3.shape[1]
    bd = _block_diag_ones()
    return pl.pallas_call(
        functools.partial(_attn_prompt_kernel, n_cmp=n_cmp),
        grid=(b, nq),
        in_specs=[pl.BlockSpec((Q_BLOCK, 4 * LANES), lambda bi, i: (bi * nq + i, 0)),
                  pl.BlockSpec((Q_BLOCK, LANES), lambda bi, i: (bi * nq + i, 0)),
                  pl.BlockSpec((None, t, 4 * LANES), lambda bi, i: (bi, 0, 0)),
                  pl.BlockSpec((None, nb, 2 * LANES), lambda bi, i: (bi, 0, 0)),
                  pl.BlockSpec(ovl.shape, lambda bi, i: (0, 0)),
                  pl.BlockSpec(bd.shape, lambda bi, i: (0, 0))],
        out_specs=pl.BlockSpec((Q_BLOCK, 4 * LANES), lambda bi, i: (bi * nq + i, 0)),
        out_shape=jax.ShapeDtypeStruct((b * t, 4 * LANES), BF16),
        scratch_shapes=[pltpu.VMEM((1, LANES), F32), pltpu.VMEM((NSA_HEADS * Q_BLOCK, LANES), F32)],
        compiler_params=_cparams(("parallel", "arbitrary")),
    )(q2d, small2d, kvb3, kcv3, ovl, bd)


def _attn_sample_kernel(pt_ref, cache_ref, q_ref, small_ref, rows_ref, win_ref, kcv_ref, swin_ref, ovl_ref,
                        o_ref, buf_ref, sem, *, n_pages, page, n_cmp, n_sel, key_chunk):
    past = n_pages * page
    slot = _prefetch_pages(pt_ref, cache_ref, buf_ref, sem, n_pages, page, 2 * LANES)
    lane1 = lax.broadcasted_iota(jnp.int32, (1, LANES), 1)
    lo1 = lane1 < HEAD_DIM
    q8f = _stack_heads(q_ref[...].astype(F32), lo1)
    q8 = q8f.astype(BF16)
    nh = NSA_HEADS

    nb = kcv_ref.shape[0]
    s = _dot_nt(q8, kcv_ref[:, 0:LANES])
    p = _masked_softmax(s, lax.broadcasted_iota(jnp.int32, (nh, nb), 1) < n_cmp).astype(BF16)
    o_c = _dot(p, kcv_ref[:, LANES:2 * LANES])
    imp8 = _dot(p, ovl_ref[...])
    nsp = ovl_ref.shape[1]
    blk = lax.broadcasted_iota(jnp.int32, (1, nsp), 1)
    selbs = []
    for g in range(NSA_KV_HEADS):
        imp = jnp.sum(imp8[g * NSA_GROUP:(g + 1) * NSA_GROUP], axis=0, keepdims=True)
        score = jnp.where(blk < n_sel, _sel_scores(imp, past, blk), -jnp.inf)
        selb = jnp.where(_topk_mask(score, SEL_TOPN) > 0.0, 0.0, NEG)
        selbs += [selb] * NSA_GROUP
    selb8 = jnp.concatenate(selbs, axis=0)
    selb8_b = selb8.astype(BF16)

    _wait_all(buf_ref.at[slot], sem.at[slot])
    ks_new = rows_ref[:, 2 * LANES:3 * LANES].astype(BF16).astype(F32)
    vs_new = rows_ref[:, 3 * LANES:4 * LANES].astype(BF16).astype(F32)
    s_new = jnp.sum(q8f * ks_new, axis=-1, keepdims=True) + selb8[:, n_sel - 1:n_sel]
    ck = key_chunk
    n_ck = past // ck
    ss = lax.broadcasted_iota(jnp.int32, (nsp, ck), 0)
    kk = lax.broadcasted_iota(jnp.int32, (nsp, ck), 1)
    scores = []
    for c in range(n_ck):
        kmat = buf_ref[slot, c * ck:(c + 1) * ck, 0:LANES].astype(BF16)
        e = jnp.where(((c * ck + kk) >> SEL_SHIFT) == ss, 1.0, 0.0).astype(BF16)
        scores.append(_dot_nt(q8, kmat) + _dot(selb8_b, e))
    m = s_new
    for sc in scores:
        m = jnp.maximum(m, jnp.max(sc, axis=-1, keepdims=True))
    p_new = jnp.exp(s_new - m)
    l = p_new
    acc = p_new * vs_new
    for c, sc in enumerate(scores):
        pe = jnp.exp(sc - m)
        l = l + jnp.sum(pe, axis=-1, keepdims=True)
        acc = acc + _dot(pe.astype(BF16), buf_ref[slot, c * ck:(c + 1) * ck, LANES:2 * LANES].astype(BF16))
    o_s = acc * (1.0 / l)

    wr = swin_ref.shape[0]
    s = _dot_nt(q8, swin_ref[:, 0:LANES].astype(BF16))
    j = lax.broadcasted_iota(jnp.int32, (nh, wr), 1)
    valid = wr - j < WINDOW
    kw_new = win_ref[:, 0:LANES].astype(BF16).astype(F32)
    vw_new = win_ref[:, LANES:2 * LANES].astype(BF16).astype(F32)
    s_new = jnp.sum(q8f * kw_new, axis=-1, keepdims=True)
    s = jnp.where(valid, s, NEG)
    m = jnp.maximum(s_new, jnp.max(s, axis=-1, keepdims=True))
    pe = jnp.where(valid, jnp.exp(s - m), 0.0)
    p_new = jnp.exp(s_new - m)
    l = p_new + jnp.sum(pe, axis=-1, keepdims=True)
    o_w = (p_new * vw_new + _dot(pe.astype(BF16), swin_ref[:, LANES:2 * LANES].astype(BF16))) * (1.0 / l)

    sig = _sigmoid(small_ref[...])
    for r in range(NSA_GROUP):
        outs = []
        for g in range(NSA_KV_HEADS):
            h = g * NSA_GROUP + r
            gl = SM_GATE + 3 * h
            outs.append(sig[:, gl:gl + 1] * o_c[h:h + 1] + sig[:, gl + 1:gl + 2] * o_s[h:h + 1]
                        + sig[:, gl + 2:gl + 3] * o_w[h:h + 1])
        o_ref[:, LANES * r:LANES * (r + 1)] = jnp.where(lo1, outs[0], outs[1])


def _attn_sample(cache3, page_table, q2d, small2d, rows2d, win2d, kcv3, swin3, ovl, n_cmp, n_sel):
    bd_, n_pages = page_table.shape
    page = cache3.shape[1]
    past = n_pages * page
    key_chunk = math.gcd(past, 2048)
    per_seq = lambda a: pl.BlockSpec((None,) + a.shape[1:], lambda i, pt: (i,) + (0,) * (a.ndim - 1))
    q3, small3, rows3, win3 = (a[:, None, :] for a in (q2d, small2d, rows2d, win2d))
    grid_spec = pltpu.PrefetchScalarGridSpec(
        num_scalar_prefetch=1,
        grid=(bd_,),
        in_specs=[pl.BlockSpec(memory_space=pl.ANY), per_seq(q3), per_seq(small3), per_seq(rows3), per_seq(win3),
                  per_seq(kcv3), per_seq(swin3), pl.BlockSpec(ovl.shape, lambda i, pt: (0, 0))],
        out_specs=pl.BlockSpec((None, 1, 4 * LANES), lambda i, pt: (i, 0, 0)),
        scratch_shapes=[pltpu.VMEM((2, past, 2 * LANES), F32), pltpu.SemaphoreType.DMA((2,))],
    )
    out = pl.pallas_call(
        functools.partial(_attn_sample_kernel, n_pages=n_pages, page=page, n_cmp=n_cmp, n_sel=n_sel,
                          key_chunk=key_chunk),
        grid_spec=grid_spec,
        out_shape=jax.ShapeDtypeStruct((bd_, 1, 4 * LANES), F32),
        compiler_params=_cparams(("arbitrary",)),
    )(page_table, cache3, q3, small3, rows3, win3, kcv3, swin3, ovl)
    return out[:, 0, :]


def _dn_kernel(x_ref, small_ref, z_ref, cw_ref, c0_ref, s0_ref, alog_ref, dtb_ref, gn_ref,
               o_ref, sout_ref, s_scr, xs_scr, *, chunk, tb, t_valid, precise):
    j = pl.program_id(1)
    dnw = z_ref.shape[1]
    hd = DN_HEAD_DIM
    hist = SUBLANES

    def opnd(x):
        return x if precise else x.astype(BF16)

    def mm(a, b, form=_dot):
        return _mm(a, b, precise, form)

    @pl.when(j == 0)
    def _():
        s_scr[...] = s0_ref[...]
        xs_scr[0:hist, :] = c0_ref[...]

    xs_scr[hist:hist + tb, :] = x_ref[...]
    conv = xs_scr[pl.ds(hist - (DN_CONV - 1), tb), :] * cw_ref[0:1, :]
    for jj in range(1, DN_CONV):
        conv = conv + xs_scr[pl.ds(hist - (DN_CONV - 1) + jj, tb), :] * cw_ref[jj:jj + 1, :]
    xs_scr[0:hist, :] = xs_scr[tb:tb + hist, :]
    act = conv * _sigmoid(conv)

    small = small_ref[...]
    tpos = j * tb + lax.broadcasted_iota(jnp.int32, (tb, LANES), 0)
    live = tpos < t_valid
    xg = small + dtb_ref[...]
    softplus = jnp.maximum(xg, 0.0) + jnp.log(1.0 + jnp.exp(-jnp.abs(xg)))
    g_all = jnp.where(live, -jnp.exp(alog_ref[...]) * softplus, 0.0)
    beta_all = jnp.where(live, _sigmoid(small), 0.0)

    ri = lax.broadcasted_iota(jnp.int32, (chunk, chunk), 0)
    ci = lax.broadcasted_iota(jnp.int32, (chunk, chunk), 1)
    incl = ri >= ci
    strict = ri > ci
    ltri = jnp.where(incl, 1.0, 0.0).astype(BF16)
    eye = jnp.where(ri == ci, 1.0, 0.0)
    lane_c = lax.broadcasted_iota(jnp.int32, (chunk, LANES), 1)
    n_sq = max(1, int(math.ceil(math.log2(chunk))))

    n_chunks = tb // chunk
    items = [(c, h) for c in range(n_chunks) for h in range(DN_HEADS)]

    gc_alls, gc3s = [], []
    for c in range(n_chunks):
        g3 = _split3(g_all[c * chunk:(c + 1) * chunk])
        gc_all = _dot(ltri, g3[0]) + (_dot(ltri, g3[1]) + _dot(ltri, g3[2]))
        gc_alls.append(gc_all)
        gc3s.append(_split3(gc_all))

    prep = []
    for c, h in items:
        rs = slice(c * chunk, (c + 1) * chunk)
        q = act[rs, h * hd:(h + 1) * hd]
        k = act[rs, dnw + h * hd:dnw + (h + 1) * hd]
        v = act[rs, 2 * dnw + h * hd:2 * dnw + (h + 1) * hd]
        q = q * lax.rsqrt(jnp.sum(q * q, axis=-1, keepdims=True) + EPS) * (hd ** -0.5)
        k = k * lax.rsqrt(jnp.sum(k * k, axis=-1, keepdims=True) + EPS)
        beta = beta_all[rs, SM_B + h:SM_B + h + 1]
        gc = gc_alls[c][:, SM_A + h:SM_A + h + 1]
        pick = jnp.where(lane_c == SM_A + h, 1.0, 0.0).astype(BF16)
        g3 = gc3s[c]
        gc_row = _dot_nt(pick, g3[0]) + (_dot_nt(pick, g3[1]) + _dot_nt(pick, g3[2]))
        dmask = jnp.where(incl, jnp.exp(jnp.where(incl, gc - gc_row, 0.0)), 0.0)
        kb = k * beta
        kop = opnd(k)
        a_strict = jnp.where(strict, mm(opnd(kb), kop, _dot_nt) * dmask, 0.0)
        egc = jnp.exp(gc)
        g_last = gc[chunk - 1:chunk, :]
        prep.append(dict(
            vb=opnd(v * beta), kbg=opnd(kb * egc), qg=opnd(q * egc),
            qk=opnd(mm(opnd(q), kop, _dot_nt) * dmask),
            kd=opnd(k * jnp.exp(g_last - gc)), decay=jnp.exp(g_last), n=-a_strict))

    tinv = [eye + p["n"] for p in prep]
    nsp = [opnd(p["n"]) for p in prep]
    for _ in range(n_sq - 1):
        nsp = [opnd(mm(n, n)) for n in nsp]
        tinv = [t + mm(opnd(t), n) for t, n in zip(tinv, nsp)]
    us, ws = [], []
    for p, t in zip(prep, tinv):
        top = opnd(t)
        us.append(mm(top, p["vb"]))
        ws.append(opnd(mm(top, p["kbg"])))

    state = [s_scr[h] for h in range(DN_HEADS)]
    for i, (c, h) in enumerate(items):
        p = prep[i]
        rs = slice(c * chunk, (c + 1) * chunk)
        s_b = opnd(state[h])
        v_new = opnd(us[i] - mm(ws[i], s_b))
        o = mm(p["qg"], s_b) + mm(p["qk"], v_new)
        state[h] = state[h] * p["decay"] + mm(p["kd"], v_new, _dot_tn)
        on = o * lax.rsqrt(jnp.mean(o * o, axis=-1, keepdims=True) + EPS) * gn_ref[...]
        z = z_ref[rs, h * hd:(h + 1) * hd]
        o_ref[rs, h * hd:(h + 1) * hd] = (on * (z * _sigmoid(z))).astype(o_ref.dtype)
    for h in range(DN_HEADS):
        s_scr[h] = state[h]

    @pl.when(j == pl.num_programs(1) - 1)
    def _():
        sout_ref[...] = s_scr[...]


def _dn_stage(x3, small3, z3, conv0, s0, dw, chunk, tb, t_valid, precise=False):
    b, tpad, w3 = x3.shape
    dnw = w3 // 3
    nblk = tpad // tb
    full = lambda a: pl.BlockSpec(a.shape, lambda bi, j: (0,) * a.ndim)
    tok = lambda w: pl.BlockSpec((None, tb, w), lambda bi, j: (bi, j, 0))
    o, s_out = pl.pallas_call(
        functools.partial(_dn_kernel, chunk=chunk, tb=tb, t_valid=t_valid, precise=precise),
        grid=(b, nblk),
        in_specs=[tok(w3), tok(LANES), tok(dnw), full(dw["cw"]),
                  pl.BlockSpec((None, SUBLANES, w3), lambda bi, j: (bi, 0, 0)),
                  pl.BlockSpec((None, DN_HEADS, DN_HEAD_DIM, DN_HEAD_DIM), lambda bi, j: (bi, 0, 0, 0)),
                  full(dw["alog"]), full(dw["dtb"]), full(dw["gn"])],
        out_specs=(tok(dnw),
                   pl.BlockSpec((None, DN_HEADS, DN_HEAD_DIM, DN_HEAD_DIM), lambda bi, j: (bi, 0, 0, 0))),
        out_shape=(jax.ShapeDtypeStruct((b, tpad, dnw), F32 if precise else BF16),
                   jax.ShapeDtypeStruct((b, DN_HEADS, DN_HEAD_DIM, DN_HEAD_DIM), F32)),
        scratch_shapes=[pltpu.VMEM((DN_HEADS, DN_HEAD_DIM, DN_HEAD_DIM), F32),
                        pltpu.VMEM((tb + SUBLANES, w3), F32)],
        compiler_params=_cparams(("parallel", "arbitrary")),
    )(x3, small3, z3, dw["cw"], conv0, s0, dw["alog"], dw["dtb"], dw["gn"])
    return o, s_out


def _merge_kernel(x_ref, oa_ref, ob_ref, mg_ref, wpa_ref, wpb_ref, wout_ref, gf_ref, wr_ref, br_ref,
                  x1_ref, h_ref, route_ref, *, precise):
    d = x_ref.shape[1]
    mixed = (mg_ref[:, 0:d] * _mm(oa_ref[...], wpa_ref[...], precise)
             + mg_ref[:, d:2 * d] * _mm(ob_ref[...], wpb_ref[...], precise))
    x1 = x_ref[...] + _mm(mixed, wout_ref[...], precise)
    x1_ref[...] = x1
    h = (x1 * lax.rsqrt(jnp.mean(x1 * x1, axis=-1, keepdims=True) + EPS)) * gf_ref[...]
    h_ref[...] = h
    logits = _mm(h, wr_ref[...], precise) + br_ref[...]
    lane = lax.broadcasted_iota(jnp.int32, logits.shape, 1)
    lanef = lane.astype(F32)
    big = float(LANES)
    is_g = lane < N_GROUPS
    lg = jnp.where(is_g, logits, NEG)
    eg = jnp.where(is_g, jnp.exp(lg - jnp.max(lg, axis=-1, keepdims=True)), 0.0)
    pg = eg / jnp.sum(eg, axis=-1, keepdims=True)
    p_top = jnp.max(pg, axis=-1, keepdims=True)
    grp = jnp.min(jnp.where(jnp.logical_and(is_g, pg == p_top), lanef, big), axis=-1, keepdims=True)
    e_lo = EXPERT_LANE0 + grp * EXPERTS_PER_GROUP
    in_grp = jnp.logical_and(lanef >= e_lo, lanef < e_lo + EXPERTS_PER_GROUP)
    le = jnp.where(in_grp, logits, -jnp.inf)
    v0 = jnp.max(le, axis=-1, keepdims=True)
    i0 = jnp.min(jnp.where(le == v0, lanef, big), axis=-1, keepdims=True)
    le1 = jnp.where(lanef == i0, -jnp.inf, le)
    v1 = jnp.max(le1, axis=-1, keepdims=True)
    i1 = jnp.min(jnp.where(le1 == v1, lanef, big), axis=-1, keepdims=True)
    e1 = jnp.exp(v1 - v0)
    den = 1.0 + e1
    w0 = p_top * (1.0 / den)
    w1 = p_top * (e1 / den)
    route = jnp.where(lane == 0, w0, jnp.where(lane == 1, w1, jnp.where(
        lane == 2, i0 - EXPERT_LANE0, jnp.where(lane == 3, i1 - EXPERT_LANE0, 0.0))))
    route_ref[...] = route


def _merge_stage(x2d, oa, ob, mg, mw, tm, precise=False):
    m, d = x2d.shape
    row = lambda w: pl.BlockSpec((tm, w), lambda i: (i, 0))
    full = lambda a: pl.BlockSpec(a.shape, lambda i: (0,) * a.ndim)
    consts = (mw["wpa"], mw["wpb"], mw["wout"], mw["gf"], mw["wr"], mw["br"])
    return pl.pallas_call(
        functools.partial(_merge_kernel, precise=precise),
        grid=(m // tm,),
        in_specs=[row(d), row(oa.shape[1]), row(ob.shape[1]), row(2 * d)] + [full(a) for a in consts],
        out_specs=(row(d), row(d), row(LANES)),
        out_shape=(jax.ShapeDtypeStruct((m, d), F32), jax.ShapeDtypeStruct((m, d), F32),
                   jax.ShapeDtypeStruct((m, LANES), F32)),
        compiler_params=_cparams(("parallel",)),
    )(x2d, oa, ob, mg, *consts)


def _row_copy(src_ref, dst_ref, sem, src_row, dst_row):
    return pltpu.make_async_copy(src_ref.at[pl.ds(src_row, 1), :], dst_ref.at[pl.ds(dst_row, 1), :], sem)


def _expert_kernel(be_ref, lo_ref, tok_ref, nblk_ref, h_ref, wg_ref, wu_ref, wd_ref, y_ref, buf_ref, sem, *, blk):
    i = pl.program_id(0)
    n_used = nblk_ref[0]
    slot = i % 2

    @pl.when(i == 0)
    def _():
        def start(r, c):
            _row_copy(h_ref, buf_ref.at[0], sem.at[0], tok_ref[lo_ref[0] + r], r).start()
            return c
        lax.fori_loop(0, blk, start, 0, unroll=8)

    def step(prefetch):
        _wait_all(buf_ref.at[slot], sem.at[slot])
        if prefetch:
            lo = lo_ref[i + 1]
            for r in range(blk):
                _row_copy(h_ref, buf_ref.at[1 - slot], sem.at[1 - slot], tok_ref[lo + r], r).start()
        xb = buf_ref[slot].astype(BF16)
        gate = _dot(xb, wg_ref[...])
        up = _dot(xb, wu_ref[...])
        mid = (gate * _sigmoid(gate)) * up
        y_ref[...] = _dot(mid.astype(BF16), wd_ref[...])

    @pl.when(i + 1 < n_used)
    def _():
        step(True)

    @pl.when(i + 1 == n_used)
    def _():
        step(False)

    @pl.when(i >= n_used)
    def _():
        y_ref[...] = jnp.zeros(y_ref.shape, F32)


def _expert_stage(h2d, blk_expert, blk_lo, s_tok, n_used, ew, blk):
    m, d = h2d.shape
    n_blocks = blk_expert.shape[0]
    de = ew["wg"].shape[2]
    wspec = lambda s: pl.BlockSpec((None,) + s, lambda i, be, lo, tok, nb: (be[i], 0, 0))
    grid_spec = pltpu.PrefetchScalarGridSpec(
        num_scalar_prefetch=4,
        grid=(n_blocks,),
        in_specs=[pl.BlockSpec(memory_space=pl.ANY), wspec((d, de)), wspec((d, de)), wspec((de, d))],
        out_specs=pl.BlockSpec((blk, d), lambda i, be, lo, tok, nb: (i, 0)),
        scratch_shapes=[pltpu.VMEM((2, blk, d), F32), pltpu.SemaphoreType.DMA((2,))],
    )
    return pl.pallas_call(
        functools.partial(_expert_kernel, blk=blk),
        grid_spec=grid_spec,
        out_shape=jax.ShapeDtypeStruct((n_blocks * blk, d), F32),
        compiler_params=_cparams(("arbitrary",)),
    )(blk_expert, blk_lo, s_tok, n_used, h2d, ew["wg"], ew["wu"], ew["wd"])


def _combine_kernel(pos_ref, x1_ref, route_ref, y_ref, out_ref, buf_ref, sem, *, tm):
    i = pl.program_id(0)
    slot = i % 2

    @pl.when(i == 0)
    def _():
        def start(r, c):
            for k in range(TOP_K):
                _row_copy(y_ref, buf_ref.at[0, k], sem.at[0], pos_ref[r * TOP_K + k], r).start()
            return c
        lax.fori_loop(0, tm, start, 0, unroll=4)

    _wait_all(buf_ref.at[slot], sem.at[slot])

    @pl.when(i + 1 < pl.num_programs(0))
    def _():
        for r in range(tm):
            for k in range(TOP_K):
                _row_copy(y_ref, buf_ref.at[1 - slot, k], sem.at[1 - slot],
                          pos_ref[((i + 1) * tm + r) * TOP_K + k], r).start(priority=k % 2)

    w = route_ref[...]
    out_ref[...] = x1_ref[...] + (buf_ref[slot, 0] * w[:, 0:1] + buf_ref[slot, 1] * w[:, 1:2])


def _combine_stage(x1, route, y_sorted, pos, tm):
    m, d = x1.shape
    grid_spec = pltpu.PrefetchScalarGridSpec(
        num_scalar_prefetch=1,
        grid=(m // tm,),
        in_specs=[pl.BlockSpec((tm, d), lambda i, p: (i, 0)), pl.BlockSpec((tm, LANES), lambda i, p: (i, 0)),
                  pl.BlockSpec(memory_space=pl.ANY)],
        out_specs=pl.BlockSpec((tm, d), lambda i, p: (i, 0)),
        scratch_shapes=[pltpu.VMEM((2, TOP_K, tm, d), F32), pltpu.SemaphoreType.DMA((2,))],
    )
    return pl.pallas_call(
        functools.partial(_combine_kernel, tm=tm),
        grid_spec=grid_spec,
        out_shape=jax.ShapeDtypeStruct((m, d), F32),
        compiler_params=_cparams(("arbitrary",)),
    )(pos, x1, route, y_sorted)


def _moe_stage(x1, h2d, route, ew, blk, tm):
    m, d = h2d.shape
    n_exp = ew["wg"].shape[0]
    n_assign = m * TOP_K
    i32 = jnp.int32
    flat_e = route[:, TOP_K:2 * TOP_K].astype(i32).reshape(n_assign)
    order = jnp.argsort(flat_e).astype(i32)
    rank = jnp.argsort(order).astype(i32)
    counts = jnp.sum(flat_e[None, :] == jnp.arange(n_exp, dtype=i32)[:, None], axis=1, dtype=i32)
    padded = (counts + blk - 1) // blk * blk
    pad_end = jnp.cumsum(padded)
    pad_start = pad_end - padded
    start = jnp.cumsum(counts) - counts
    n_blocks = (n_assign + n_exp * (blk - 1) + blk - 1) // blk
    blk_first = jnp.arange(n_blocks, dtype=i32) * blk
    blk_expert = jnp.minimum(jnp.sum(pad_end[None, :] <= blk_first[:, None], axis=1, dtype=i32), n_exp - 1)
    delta = start - pad_start
    blk_lo = jnp.clip(blk_first + delta[blk_expert], 0, n_assign).astype(i32)
    s_tok = jnp.concatenate([order // TOP_K, jnp.zeros((blk,), i32)])
    onehot = flat_e[:, None] == jnp.arange(n_exp, dtype=i32)[None, :]
    pos = (rank - jnp.sum(jnp.where(onehot, delta[None, :], 0), axis=1, dtype=i32)).astype(i32)
    n_used = (pad_end[n_exp - 1:n_exp] // blk).astype(i32)
    y_sorted = _expert_stage(h2d, blk_expert, blk_lo, s_tok, n_used, ew, blk)
    return _combine_stage(x1, route, y_sorted, pos, tm)


def _rope_tables(pos):
    inv_freq = ROPE_THETA ** (-jnp.arange(ROPE_HALF, dtype=F32) / ROPE_HALF)
    ang = pos.astype(F32)[:, None] * inv_freq[None, :]
    cos, sin = jnp.cos(ang), jnp.sin(ang)
    n = pos.shape[0]
    zeros = lambda k: jnp.zeros((n, k), F32)
    c = jnp.concatenate([cos, cos, jnp.ones((n, HEAD_DIM - ROPE_DIM), F32)], axis=1)
    s1 = jnp.concatenate([-sin, zeros(HEAD_DIM - ROPE_HALF)], axis=1)
    s2 = jnp.concatenate([zeros(ROPE_HALF), sin, zeros(HEAD_DIM - ROPE_DIM)], axis=1)
    return tuple(jnp.tile(a, (1, LANES // HEAD_DIM)) for a in (c, s1, s2))


def _block_diag_ones():
    i = np.arange(LANES)
    return jnp.asarray((i[:, None] // HEAD_DIM == i[None, :] // HEAD_DIM).astype(np.float32), BF16)


def _pack_proj(w_in, norm_mix, q_norm, k_norm, dnw):
    d = w_in.shape[0]
    qc = NSA_HEADS * HEAD_DIM
    kvc = NSA_KV_HEADS * HEAD_DIM
    sizes = (qc,) + (kvc,) * 6 + (NSA_HEADS * 3, 3 * dnw, DN_HEADS, DN_HEADS, dnw, d, d)
    o = np.concatenate([[0], np.cumsum(sizes)])
    seg = lambda i: w_in[:, int(o[i]):int(o[i + 1])]
    wq = seg(0).reshape(d, NSA_HEADS, HEAD_DIM)[:, np.asarray(HEAD_PERM)].reshape(d, qc)
    small = jnp.concatenate([seg(9), seg(10), seg(7), jnp.zeros((d, LANES - 2 * DN_HEADS - 3 * NSA_HEADS), F32)], axis=1)
    w = jnp.concatenate([wq, seg(1), seg(2), seg(3), seg(4), seg(5), seg(6), small, seg(8), seg(11), seg(12), seg(13)],
                        axis=1)
    offs, ncols = _proj_layout(dnw, d)
    assert ncols == w.shape[1]
    rep = LANES // HEAD_DIM
    return dict(w=w.astype(BF16), w32=w, offs=offs, ncols=ncols, gin=norm_mix[None, :],
                qg=jnp.tile(q_norm, rep)[None, :], kg=jnp.tile(k_norm[0:2], (1, rep)), bd=_block_diag_ones())


def _pack_compress(w_cmp, cmp_pos, k_norm):
    def bdiag(w):
        z = jnp.zeros_like(w)
        return jnp.concatenate([jnp.concatenate([w, z], axis=2), jnp.concatenate([z, w], axis=2)], axis=1)

    s = CMP_STRIDE
    halves = [(0, slice(0, s)), (0, slice(s, 2 * s)), (1, slice(0, s)), (1, slice(s, 2 * s))]
    return dict(w=jnp.stack([bdiag(w_cmp[i, sl]) for i, sl in halves]).astype(BF16),
                p=jnp.stack([jnp.tile(cmp_pos[i, sl], (1, LANES // HEAD_DIM)) for i, sl in halves]),
                kg=jnp.tile(k_norm[2], LANES // HEAD_DIM)[None, :], bd=_block_diag_ones())


def _overlap(nb, ns_pad, n_cmp, n_sel):
    cs = np.arange(nb)[:, None] * CMP_STRIDE
    ss = np.arange(ns_pad)[None, :] * SEL_BLOCK
    ov = (cs <= ss + SEL_BLOCK - 1) & (cs + CMP_BLOCK - 1 >= ss)
    ov &= (np.arange(nb)[:, None] < n_cmp) & (np.arange(ns_pad)[None, :] < n_sel)
    return jnp.asarray(ov.astype(np.float32), BF16)


def _pad_lanes(v, n=LANES):
    return jnp.zeros((1, n), F32).at[0, :v.shape[0]].set(v)


def _round_up(x, n):
    return (x + n - 1) // n * n


def kernel(x_prompt, x_sample, cache_nsa, page_table, state_win, state_dn_conv, state_dn_S, norm_mix, w_in, q_norm, k_norm, cmp_pos, w_cmp, dn_conv_w, dn_A_log, dn_dt_bias, dn_norm, w_proj_a, w_proj_b, w_out, norm_ffn, w_router_g, b_router_g, w_router_e, b_router_e, w_gate, w_up, w_down):
    b, t, d = x_prompt.shape
    bd_, tn, _ = x_sample.shape
    depth = w_in.shape[0]
    assert depth == 1 and tn == 1, "one layer, one new token per sample sequence"
    n_pool, page = cache_nsa.shape[1], cache_nsa.shape[2]
    n_pages = page_table.shape[1]
    past = n_pages * page
    win_rows = state_win.shape[2]
    dnw = dn_conv_w.shape[2] // 3
    assert t % Q_BLOCK == 0 and t >= WINDOW + Q_BLOCK and t % SEL_CHUNK == 0 and past % CMP_STRIDE == 0
    l = 0

    pw = _pack_proj(w_in[l], norm_mix[l], q_norm[l], k_norm[l], dnw)
    cw = _pack_compress(w_cmp[l], cmp_pos[l], k_norm[l])
    dw = dict(cw=dn_conv_w[l], alog=_pad_lanes(dn_A_log[l]), dtb=_pad_lanes(dn_dt_bias[l]), gn=dn_norm[l][None, :])
    perm = np.asarray(HEAD_PERM)
    wr = jnp.zeros((d, LANES), F32).at[:, 0:N_GROUPS].set(w_router_g[l])
    wr = wr.at[:, EXPERT_LANE0:EXPERT_LANE0 + w_router_e.shape[2]].set(w_router_e[l])
    br = jnp.zeros((1, LANES), F32).at[0, 0:N_GROUPS].set(b_router_g[l])
    br = br.at[0, EXPERT_LANE0:EXPERT_LANE0 + b_router_e.shape[1]].set(b_router_e[l])
    mw32 = dict(wpa=w_proj_a[l].reshape(NSA_HEADS, HEAD_DIM, d)[perm].reshape(NSA_HEADS * HEAD_DIM, d),
                wpb=w_proj_b[l], wout=w_out[l], gf=norm_ffn[l][None, :], wr=wr, br=br)
    mw = {k: (v.astype(BF16) if k in ("wpa", "wpb", "wout", "wr") else v) for k, v in mw32.items()}
    ew = dict(wg=w_gate[l].astype(BF16), wu=w_up[l].astype(BF16), wd=w_down[l].astype(BF16))

    tm = 256
    tabs_p = _rope_tables(jnp.arange(t))
    q_p, rows_p, win_p, kvb_p, small_p, dnqkv_p, dnz_p, mg_p = _proj_stage(
        x_prompt.reshape(b * t, d), pw, tabs_p, PROJ_TM, t // PROJ_TM)
    nb_p = t // CMP_STRIDE
    n_cmp_p = (t - CMP_BLOCK) // CMP_STRIDE + 1
    n_sel_p = -(-t // SEL_BLOCK)
    tabs_cp = _rope_tables(jnp.arange(nb_p) * CMP_STRIDE + CMP_BLOCK - 1)
    kcv_p = _compress_prompt(rows_p.reshape(b, t, 4 * LANES), cw, tabs_cp)
    ovl_p = _overlap(nb_p, _round_up(n_sel_p, LANES), n_cmp_p, n_sel_p)
    oa_p = _attn_prompt(q_p, small_p, kvb_p.reshape(b, t, 4 * LANES), kcv_p, ovl_p, n_cmp_p)
    tb = DN_TB
    ob_p, s_p = _dn_stage(dnqkv_p.reshape(b, t, 3 * dnw), small_p.reshape(b, t, LANES), dnz_p.reshape(b, t, dnw),
                          jnp.zeros((b, SUBLANES, 3 * dnw), F32),
                          jnp.zeros((b, DN_HEADS, DN_HEAD_DIM, DN_HEAD_DIM), F32), dw, DN_CHUNK, tb, t)
    x1_p, h_p, route_p = _merge_stage(x_prompt.reshape(b * t, d), oa_p, ob_p.reshape(b * t, dnw), mg_p, mw, MERGE_TM)
    y_p = _moe_stage(x1_p, h_p, route_p, ew, 256, tm).reshape(b, t, d)
    kv_p = rows_p.reshape(1, b, t, 4, NSA_KV_HEADS, HEAD_DIM)
    win_all = win_p.reshape(b, t, 2, NSA_KV_HEADS, HEAD_DIM)
    win_out_p = jnp.pad(win_all, ((0, 0), (max(0, win_rows - t), 0), (0, 0), (0, 0), (0, 0)))[:, -win_rows:][None]
    conv_p = dnqkv_p.reshape(b, t, 3 * dnw)[:, t - (DN_CONV - 1):][None]

    tabs_s = _rope_tables(jnp.full((bd_,), past, jnp.int32))
    q_s, rows_s, win_s, _, small_s, dnqkv_s, dnz_s, mg_s = _proj_stage(
        x_sample.reshape(bd_, d), dict(pw, w=pw["w32"]), tabs_s, bd_, 1, precise=True)
    cache3 = cache_nsa[l].reshape(n_pool, page, 4 * LANES)
    nb_s = past // CMP_STRIDE
    n_cmp_s = (past + tn - CMP_BLOCK) // CMP_STRIDE + 1
    n_sel_s = -(-(past + tn) // SEL_BLOCK)
    assert n_cmp_s == nb_s - 1 and n_cmp_p == nb_p - 1 and (1 << SEL_SHIFT) == SEL_BLOCK
    tabs_cs = _rope_tables(jnp.arange(nb_s) * CMP_STRIDE + CMP_BLOCK - 1)
    kcv_s = _compress_sample(cache3, page_table, cw, tabs_cs)
    ovl_s = _overlap(nb_s, _round_up(n_sel_s, LANES), n_cmp_s, n_sel_s)
    swin3 = state_win[l].reshape(bd_, win_rows, 2 * LANES)
    oa_s = _attn_sample(cache3, page_table, q_s, small_s, rows_s, win_s, kcv_s, swin3, ovl_s, n_cmp_s, n_sel_s)
    pad_t = lambda a: jnp.pad(a[:, None, :], ((0, 0), (0, SUBLANES - tn), (0, 0)))
    conv0_s = jnp.pad(state_dn_conv[l], ((0, 0), (SUBLANES - (DN_CONV - 1), 0), (0, 0)))
    ob_s, s_s = _dn_stage(pad_t(dnqkv_s), pad_t(small_s), pad_t(dnz_s), conv0_s, state_dn_S[l], dw,
                          SUBLANES, SUBLANES, tn, precise=True)
    x1_s, h_s, route_s = _merge_stage(x_sample.reshape(bd_, d), oa_s, ob_s[:, 0, :], mg_s, mw32, bd_, precise=True)
    y_s = _moe_stage(x1_s, h_s, route_s, ew, 64, bd_).reshape(bd_, tn, d)
    kv_s = rows_s.reshape(1, bd_, tn, 4, NSA_KV_HEADS, HEAD_DIM)
    win_new = win_s.reshape(bd_, tn, 2, NSA_KV_HEADS, HEAD_DIM)
    win_out_s = jnp.concatenate([state_win[l], win_new], axis=1)[:, -win_rows:][None]
    conv_s = jnp.concatenate([state_dn_conv[l], dnqkv_s[:, None, :]], axis=1)[:, -(DN_CONV - 1):][None]

    return (y_p, y_s, kv_p, kv_s, win_out_p, win_out_s, conv_p, conv_s, s_p[None], s_s[None])
```

```python
import functools
import math

import numpy as np
import jax
import jax.numpy as jnp
from jax import lax
from jax.experimental import pallas as pl
from jax.experimental.pallas import tpu as pltpu

F32 = jnp.float32
BF16 = jnp.bfloat16

NSA_HEADS = 8
NSA_KV_HEADS = 2
NSA_GROUP = NSA_HEADS // NSA_KV_HEADS
HEAD_DIM = 64
ROPE_DIM = HEAD_DIM // 4
ROPE_HALF = ROPE_DIM // 2
ROPE_THETA = 500000.0
CMP_BLOCK = 32
CMP_STRIDE = 16
SEL_BLOCK = 64
SEL_SHIFT = 6
SEL_TOPN = 16
WINDOW = 512
FORCE_BONUS = 1000.0
DN_HEADS = 4
DN_HEAD_DIM = 128
DN_CONV = 4
DN_CHUNK = 64
N_GROUPS = 4
EXPERTS_PER_GROUP = 8
TOP_K = 2
EPS = 1e-6

LANES = 128
SUBLANES = 8
VMEM_LIMIT = 56 * 1024 * 1024
NEG = -1e30
Q_BLOCK = 128
SEL_CHUNK = 1024
PROJ_TM = 512
MERGE_TM = 512
DN_TB = 512
SHIFT_SLACK = 1.0 + 2.0 ** -5
SHIFT_MIN_SUM = 1e-20
HEAD_PERM = (0, 4, 1, 5, 2, 6, 3, 7)
SM_A, SM_B, SM_GATE = 0, DN_HEADS, 2 * DN_HEADS
EXPERT_LANE0 = 32


def _cparams(sem):
    return pltpu.CompilerParams(dimension_semantics=sem, vmem_limit_bytes=VMEM_LIMIT)


def _dot(a, b):
    return jnp.dot(a, b, preferred_element_type=F32)


def _dot_nt(a, b):
    return lax.dot_general(a, b, (((1,), (1,)), ((), ())), preferred_element_type=F32)


def _dot_tn(a, b):
    return lax.dot_general(a, b, (((0,), (0,)), ((), ())), preferred_element_type=F32)


def _split2(x):
    hi = x.astype(BF16)
    lo = (x - hi.astype(F32)).astype(BF16)
    return hi, lo


def _split3(x):
    hi = x.astype(BF16)
    r = x - hi.astype(F32)
    mid = r.astype(BF16)
    lo = (r - mid.astype(F32)).astype(BF16)
    return hi, mid, lo


def _dot3s(a, b):
    return _dot(a[0], b[0]) + (_dot(a[0], b[1]) + _dot(a[1], b[0]))


def _mm(a, b, precise, form=_dot):
    if not precise:
        return form(a.astype(BF16), b.astype(BF16))
    ah, al = _split2(a.astype(F32))
    bh, bl = _split2(b.astype(F32))
    return form(ah, bh) + (form(ah, bl) + form(al, bh))


def _sigmoid(x):
    return 1.0 / (1.0 + jnp.exp(-x))


def _seg_sumsq(x, bd):
    hi, lo = _split2(x * x)
    return _dot(hi, bd) + _dot(lo, bd)


def _rope128(x, c, s1, s2):
    return x * c + pltpu.roll(x, LANES - ROPE_HALF, 1) * s1 + pltpu.roll(x, ROPE_HALF, 1) * s2


def _head_norm_rope(v, gain, bd, c, s1, s2):
    ss = _seg_sumsq(v, bd) * (1.0 / HEAD_DIM)
    return _rope128(v * lax.rsqrt(ss + EPS) * gain, c, s1, s2)


def _masked_softmax(s, valid):
    s = jnp.where(valid, s, NEG)
    m = jnp.max(s, axis=-1, keepdims=True)
    e = jnp.where(valid, jnp.exp(s - m), 0.0)
    l = jnp.sum(e, axis=-1, keepdims=True)
    return e * (1.0 / jnp.maximum(l, 1e-30))


def _topk_mask(score, k):
    lane = lax.broadcasted_iota(jnp.int32, score.shape, 1).astype(F32)
    big = float(score.shape[-1])

    def body(_, carry):
        sc, sel = carry
        m = jnp.max(sc, axis=-1, keepdims=True)
        idx = jnp.min(jnp.where(sc == m, lane, big), axis=-1, keepdims=True)
        hit = lane == idx
        sel = jnp.where(hit, jnp.maximum(sel, jnp.where(m > -jnp.inf, 1.0, 0.0)), sel)
        sc = jnp.where(hit, -jnp.inf, sc)
        return sc, sel

    _, sel = lax.fori_loop(0, k, body, (score, jnp.zeros(score.shape, F32)))
    return sel


def _topk_mask_rows(score, k):
    idx = lax.broadcasted_iota(jnp.int32, score.shape, 0).astype(F32)
    big = float(score.shape[0])

    def body(_, carry):
        sc, sel = carry
        m = jnp.max(sc, axis=0, keepdims=True)
        first = jnp.min(jnp.where(sc == m, idx, big), axis=0, keepdims=True)
        hit = idx == first
        sel = jnp.where(hit, jnp.maximum(sel, jnp.where(m > -jnp.inf, 1.0, 0.0)), sel)
        sc = jnp.where(hit, -jnp.inf, sc)
        return sc, sel

    _, sel = lax.fori_loop(0, k, body, (score, jnp.zeros(score.shape, F32)))
    return sel


def _proj_layout(dnw, d):
    sizes = dict(q=NSA_HEADS * HEAD_DIM, kvc=2 * LANES, ks=LANES, vs=LANES, kw=LANES, vw=LANES,
                 small=LANES, dnqkv=3 * dnw, dnz=dnw, mg=2 * d)
    offs, o = {}, 0
    for name, n in sizes.items():
        offs[name] = (o, o + n)
        o += n
    return offs, o


def _proj_kernel(x_ref, gin_ref, w_ref, c_ref, s1_ref, s2_ref, qg_ref, kg_ref, bd_ref,
                 q_out, rows_out, win_out, kvb_out, small_out, dnqkv_out, dnz_out, mg_out, *, offs, precise):
    x = x_ref[...]
    ms = jnp.mean(x * x, axis=-1, keepdims=True)
    hb = (x * lax.rsqrt(ms + EPS)) * gin_ref[...]
    if not precise:
        hb = hb.astype(BF16)

    def mm(name, lo=0, hi=None):
        a, b = offs[name]
        hi = b - a if hi is None else hi
        return _mm(hb, w_ref[:, a + lo:a + hi], precise)

    c, s1, s2 = c_ref[...], s1_ref[...], s2_ref[...]
    bd = bd_ref[...]
    q_all = mm("q")
    for j in range(NSA_HEADS * HEAD_DIM // LANES):
        qj = _head_norm_rope(q_all[:, LANES * j:LANES * (j + 1)], qg_ref[...], bd, c, s1, s2)
        q_out[:, LANES * j:LANES * (j + 1)] = (qj * (HEAD_DIM ** -0.5)).astype(BF16)
    rows_out[:, 0:2 * LANES] = mm("kvc")
    five = _mm(hb, w_ref[:, offs["ks"][0]:offs["small"][1]], precise)
    ks = _head_norm_rope(five[:, 0:LANES], kg_ref[0:1, :], bd, c, s1, s2)
    vs = five[:, LANES:2 * LANES]
    rows_out[:, 2 * LANES:3 * LANES] = ks
    rows_out[:, 3 * LANES:4 * LANES] = vs
    kw = _head_norm_rope(five[:, 2 * LANES:3 * LANES], kg_ref[1:2, :], bd, c, s1, s2)
    vw = five[:, 3 * LANES:4 * LANES]
    win_out[:, 0:LANES] = kw
    win_out[:, LANES:2 * LANES] = vw
    kvb_out[:, 0:LANES] = ks.astype(BF16)
    kvb_out[:, LANES:2 * LANES] = vs.astype(BF16)
    kvb_out[:, 2 * LANES:3 * LANES] = kw.astype(BF16)
    kvb_out[:, 3 * LANES:4 * LANES] = vw.astype(BF16)
    small_out[...] = five[:, 4 * LANES:5 * LANES]
    dnqkv_out[...] = mm("dnqkv")
    dnz_out[...] = mm("dnz")
    mg_out[...] = _sigmoid(mm("mg"))


def _proj_stage(x2d, pw, tabs, tm, n_tab_blocks, precise=False):
    m, d = x2d.shape
    offs, ncols = pw["offs"], pw["ncols"]
    dnw = offs["dnz"][1] - offs["dnz"][0]
    row = lambda w: pl.BlockSpec((tm, w), lambda i: (i, 0))
    full = lambda a: pl.BlockSpec(a.shape, lambda i: (0,) * a.ndim)
    tab = pl.BlockSpec((tm, LANES), lambda i: (i % n_tab_blocks, 0))
    out_shape = (
        jax.ShapeDtypeStruct((m, 4 * LANES), BF16),
        jax.ShapeDtypeStruct((m, 4 * LANES), F32),
        jax.ShapeDtypeStruct((m, 2 * LANES), F32),
        jax.ShapeDtypeStruct((m, 4 * LANES), BF16),
        jax.ShapeDtypeStruct((m, LANES), F32),
        jax.ShapeDtypeStruct((m, 3 * dnw), F32),
        jax.ShapeDtypeStruct((m, dnw), F32),
        jax.ShapeDtypeStruct((m, 2 * d), F32),
    )
    return pl.pallas_call(
        functools.partial(_proj_kernel, offs=offs, precise=precise),
        grid=(m // tm,),
        in_specs=[row(d), full(pw["gin"]),
                  pl.BlockSpec(pw["w"].shape, lambda i: (0, 0), pipeline_mode=pl.Buffered(1)),
                  tab, tab, tab, full(pw["qg"]), full(pw["kg"]), full(pw["bd"])],
        out_specs=tuple(row(s.shape[1]) for s in out_shape),
        out_shape=out_shape,
        compiler_params=_cparams(("parallel",)),
    )(x2d, pw["gin"], pw["w"], tabs[0], tabs[1], tabs[2], pw["qg"], pw["kg"], pw["bd"])


def _compress_body(xk_ref, xv_ref, w_ref, p_ref, kg_ref, bd_ref, c_ref, s1_ref, s2_ref, out_ref):
    nb = xk_ref.shape[0] // CMP_STRIDE
    acc = [jnp.zeros((nb, LANES), F32) for _ in range(4)]
    for l in range(CMP_STRIDE):
        xs = (xk_ref[pl.ds(l, nb, stride=CMP_STRIDE), :], xv_ref[pl.ds(l, nb, stride=CMP_STRIDE), :])
        for j in range(4):
            acc[j] = acc[j] + _dot((xs[j // 2] + p_ref[j, l:l + 1, :]).astype(BF16), w_ref[j, l])
    kraw = acc[0] + pltpu.roll(acc[1], nb - 1, 0)
    vraw = acc[2] + pltpu.roll(acc[3], nb - 1, 0)
    kc = _head_norm_rope(kraw, kg_ref[...], bd_ref[...], c_ref[...], s1_ref[...], s2_ref[...])
    out_ref[:, 0:LANES] = kc.astype(BF16)
    out_ref[:, LANES:2 * LANES] = vraw.astype(BF16)


def _compress_prompt_kernel(xk_ref, xv_ref, *rest):
    _compress_body(xk_ref, xv_ref, *rest)


def _compress_prompt(rows3, cw, tabs):
    b, t, _ = rows3.shape
    nb = t // CMP_STRIDE
    full = lambda a: pl.BlockSpec(a.shape, lambda i: (0,) * a.ndim)
    consts = (cw["w"], cw["p"], cw["kg"], cw["bd"]) + tuple(tabs)
    return pl.pallas_call(
        _compress_prompt_kernel,
        grid=(b,),
        in_specs=[pl.BlockSpec((None, t, LANES), lambda i: (i, 0, 0)),
                  pl.BlockSpec((None, t, LANES), lambda i: (i, 0, 1))] + [full(a) for a in consts],
        out_specs=pl.BlockSpec((None, nb, 2 * LANES), lambda i: (i, 0, 0)),
        out_shape=jax.ShapeDtypeStruct((b, nb, 2 * LANES), BF16),
        compiler_params=_cparams(("parallel",)),
    )(rows3, rows3, *consts)


def _wait_all(buf_view, sem):
    pltpu.make_async_copy(buf_view, buf_view, sem).wait()


def _start_pages(pt_ref, cache_ref, buf_ref, sem, b, slot, n_pages, page, lane0):
    split = len(buf_ref.shape) == 4

    def start_page(p, priority):
        rows = pl.ds(pl.multiple_of(p * page, page), page)
        if split:
            for j in range(2):
                pltpu.make_async_copy(cache_ref.at[pt_ref[b, p], :, pl.ds(lane0 + j * LANES, LANES)],
                                      buf_ref.at[slot, j, rows, :], sem.at[slot]).start(priority=priority)
        else:
            pltpu.make_async_copy(cache_ref.at[pt_ref[b, p], :, pl.ds(lane0, 2 * LANES)],
                                  buf_ref.at[slot, rows, :], sem.at[slot]).start(priority=priority)

    def start_pair(i, c):
        start_page(2 * i, 0)
        start_page(2 * i + 1, 1)
        return c

    lax.fori_loop(0, n_pages // 2, start_pair, 0)
    if n_pages % 2:
        start_page(n_pages - 1, 0)


def _prefetch_pages(pt_ref, cache_ref, buf_ref, sem, n_pages, page, lane0):
    b = pl.program_id(0)
    slot = b % 2

    @pl.when(b == 0)
    def _():
        _start_pages(pt_ref, cache_ref, buf_ref, sem, 0, 0, n_pages, page, lane0)

    @pl.when(b + 1 < pl.num_programs(0))
    def _():
        _start_pages(pt_ref, cache_ref, buf_ref, sem, b + 1, 1 - slot, n_pages, page, lane0)

    return slot


def _compress_sample_kernel(pt_ref, cache_ref, *rest, n_pages, page):
    *consts, out_ref, buf_ref, sem = rest
    slot = _prefetch_pages(pt_ref, cache_ref, buf_ref, sem, n_pages, page, 0)
    _wait_all(buf_ref.at[slot], sem.at[slot])
    _compress_body(buf_ref.at[slot, 0], buf_ref.at[slot, 1], *consts, out_ref)


def _compress_sample(cache3, page_table, cw, tabs):
    bd_, n_pages = page_table.shape
    page = cache3.shape[1]
    past = n_pages * page
    nb = past // CMP_STRIDE
    full = lambda a: pl.BlockSpec(a.shape, lambda i, pt: (0,) * a.ndim)
    consts = (cw["w"], cw["p"], cw["kg"], cw["bd"]) + tuple(tabs)
    grid_spec = pltpu.PrefetchScalarGridSpec(
        num_scalar_prefetch=1,
        grid=(bd_,),
        in_specs=[pl.BlockSpec(memory_space=pl.ANY)] + [full(a) for a in consts],
        out_specs=pl.BlockSpec((None, nb, 2 * LANES), lambda i, pt: (i, 0, 0)),
        scratch_shapes=[pltpu.VMEM((2, 2, past, LANES), F32), pltpu.SemaphoreType.DMA((2,))],
    )
    return pl.pallas_call(
        functools.partial(_compress_sample_kernel, n_pages=n_pages, page=page),
        grid_spec=grid_spec,
        out_shape=jax.ShapeDtypeStruct((bd_, nb, 2 * LANES), BF16),
        compiler_params=_cparams(("arbitrary",)),
    )(page_table, cache3, *consts)


def _stack_heads(q, lo_half):
    zero = jnp.zeros((), q.dtype)
    parts = []
    for g in range(NSA_KV_HEADS):
        for r in range(NSA_GROUP):
            col = q[:, LANES * r:LANES * (r + 1)]
            parts.append(jnp.where(lo_half if g == 0 else jnp.logical_not(lo_half), col, zero))
    return jnp.concatenate(parts, axis=0)


def _sel_scores(imp, pos_q, blk):
    cur = pos_q >> SEL_SHIFT
    readable = blk * SEL_BLOCK <= pos_q
    forced = jnp.logical_or(blk == 0, jnp.logical_or(blk == cur, blk == cur - 1))
    return jnp.where(readable, imp + jnp.where(forced, FORCE_BONUS, 0.0), -jnp.inf)


def _attn_prompt_kernel(q_ref, small_ref, kvb_ref, kcv_ref, ovl_ref, bd_ref, o_ref, kmx_ref, os_ref, *, n_cmp):
    qb = Q_BLOCK
    m8 = NSA_HEADS * qb
    i = pl.program_id(1)
    q0 = i * qb

    @pl.when(i == 0)
    def _():
        def body(c, mx):
            kf = kvb_ref[pl.ds(pl.multiple_of(c * SEL_CHUNK, SEL_CHUNK), SEL_CHUNK), 0:LANES].astype(F32)
            return jnp.maximum(mx, jnp.max(_seg_sumsq(kf, bd_ref[...]), axis=0, keepdims=True))
        kmx_ref[...] = lax.fori_loop(0, kvb_ref.shape[0] // SEL_CHUNK, body, jnp.zeros((1, LANES), F32))

    lane = lax.broadcasted_iota(jnp.int32, (qb, LANES), 1)
    lo_half = lane < HEAD_DIM
    q8 = _stack_heads(q_ref[...], lo_half)

    def qpos(n):
        return q0 + (lax.broadcasted_iota(jnp.int32, (m8, n), 0) & (qb - 1))

    def kidx(n):
        return lax.broadcasted_iota(jnp.int32, (m8, n), 1)

    nb = kcv_ref.shape[0]
    s = _dot_nt(q8, kcv_ref[:, 0:LANES])
    cidx = kidx(nb)
    valid = jnp.logical_and(cidx < n_cmp, cidx * CMP_STRIDE + (CMP_BLOCK - 1) <= qpos(nb))
    p = _masked_softmax(s, valid).astype(BF16)
    o_c = _dot(p, kcv_ref[:, LANES:2 * LANES])
    imp8 = _dot(p, ovl_ref[...])

    ns = ovl_ref.shape[1]
    imps = []
    for g in range(NSA_KV_HEADS):
        acc = imp8[(g * NSA_GROUP) * qb:(g * NSA_GROUP + 1) * qb]
        for r in range(1, NSA_GROUP):
            acc = acc + imp8[(g * NSA_GROUP + r) * qb:(g * NSA_GROUP + r + 1) * qb]
        imps.append(acc)
    imp = jnp.concatenate(imps, axis=0)
    pos2 = q0 + (lax.broadcasted_iota(jnp.int32, (2 * qb, ns), 0) & (qb - 1))
    blk2 = lax.broadcasted_iota(jnp.int32, (2 * qb, ns), 1)
    sel = jnp.transpose(_topk_mask_rows(jnp.transpose(_sel_scores(imp, pos2, blk2)), SEL_TOPN))
    sel8 = jnp.concatenate([sel[0:qb]] * NSA_GROUP + [sel[qb:2 * qb]] * NSA_GROUP, axis=0) > 0.0

    ck = SEL_CHUNK
    qpos_k = qpos(ck)
    kk = lax.broadcasted_iota(jnp.int32, (ck, ns), 0)
    ss = lax.broadcasted_iota(jnp.int32, (ck, ns), 1)
    n_chunks = (q0 + qb + ck - 1) // ck

    def scores(qaug, c, causal):
        k0 = pl.multiple_of(c * ck, ck)
        e = jnp.where(((k0 + kk) >> SEL_SHIFT) == ss, 1.0, 0.0).astype(BF16)
        sc = _dot_nt(qaug, jnp.concatenate([kvb_ref[pl.ds(k0, ck), 0:LANES], e], axis=1))
        if causal:
            sc = jnp.where(k0 + kidx(ck) <= qpos_k, sc, NEG)
        return sc, kvb_ref[pl.ds(k0, ck), LANES:2 * LANES]

    q8f = q8.astype(F32)
    qn2 = jnp.sum(q8f * q8f, axis=-1, keepdims=True)
    kmx = kmx_ref[...]
    half = NSA_GROUP * qb
    kq2 = jnp.concatenate([qn2[0:half] * kmx[:, 0:1], qn2[half:] * kmx[:, HEAD_DIM:HEAD_DIM + 1]], axis=0)
    shift = jnp.sqrt(kq2) * SHIFT_SLACK + 1e-6
    qaug = jnp.concatenate([q8, jnp.where(sel8, -shift, NEG).astype(BF16)], axis=1)

    def chunk_shifted(c, carry, causal):
        l, acc = carry
        sc, vmat = scores(qaug, c, causal)
        pe = jnp.exp(sc)
        return l + jnp.sum(pe, axis=-1, keepdims=True), acc + _dot(pe.astype(BF16), vmat)

    carry = lax.fori_loop(0, n_chunks - 1, functools.partial(chunk_shifted, causal=False),
                          (jnp.zeros((m8, 1), F32), jnp.zeros((m8, LANES), F32)))
    l_s, acc_s = chunk_shifted(n_chunks - 1, carry, True)
    os_ref[...] = acc_s * (1.0 / jnp.maximum(l_s, 1e-37))

    @pl.when(jnp.min(l_s) < SHIFT_MIN_SUM)
    def _():
        qaug_x = jnp.concatenate([q8, jnp.where(sel8, 0.0, NEG).astype(BF16)], axis=1)

        def chunk_online(c, carry, causal):
            m, l, acc = carry
            sc, vmat = scores(qaug_x, c, causal)
            m_new = jnp.maximum(m, jnp.max(sc, axis=-1, keepdims=True))
            alpha = jnp.exp(m - m_new)
            pe = jnp.exp(sc - m_new)
            return (m_new, alpha * l + jnp.sum(pe, axis=-1, keepdims=True),
                    alpha * acc + _dot(pe.astype(BF16), vmat))

        init = (jnp.full((m8, 1), NEG, F32), jnp.zeros((m8, 1), F32), jnp.zeros((m8, LANES), F32))
        carry_x = lax.fori_loop(0, n_chunks - 1, functools.partial(chunk_online, causal=False), init)
        _, l_x, acc_x = chunk_online(n_chunks - 1, carry_x, True)
        os_ref[...] = acc_x * (1.0 / l_x)

    o_s = os_ref[...]

    wk = WINDOW + qb
    ws = pl.multiple_of(jnp.maximum(q0 - WINDOW, 0), qb)
    s = _dot_nt(q8, kvb_ref[pl.ds(ws, wk), 2 * LANES:3 * LANES])
    dpos = qpos(wk) - (ws + kidx(wk))
    p = _masked_softmax(s, jnp.logical_and(dpos >= 0, dpos < WINDOW)).astype(BF16)
    o_w = _dot(p, kvb_ref[pl.ds(ws, wk), 3 * LANES:4 * LANES])

    sig = _sigmoid(small_ref[...])
    for r in range(NSA_GROUP):
        outs = []
        for g in range(NSA_KV_HEADS):
            h = g * NSA_GROUP + r
            rows = slice(h * qb, (h + 1) * qb)
            gl = SM_GATE + 3 * h
            outs.append(sig[:, gl:gl + 1] * o_c[rows] + sig[:, gl + 1:gl + 2] * o_s[rows]
                        + sig[:, gl + 2:gl + 3] * o_w[rows])
        o_ref[:, LANES * r:LANES * (r + 1)] = jnp.where(lo_half, outs[0], outs[1]).astype(BF16)


def _attn_prompt(q2d, small2d, kvb3, kcv3, ovl, n_cmp):
    b, t, _ = kvb3.shape
    nq = t // Q_BLOCK
    nb = kcv3.shape[1]
    bd = _block_diag_ones()
    return pl.pallas_call(
        functools.partial(_attn_prompt_kernel, n_cmp=n_cmp),
        grid=(b, nq),
        in_specs=[pl.BlockSpec((Q_BLOCK, 4 * LANES), lambda bi, i: (bi * nq + i, 0)),
                  pl.BlockSpec((Q_BLOCK, LANES), lambda bi, i: (bi * nq + i, 0)),
                  pl.BlockSpec((None, t, 4 * LANES), lambda bi, i: (bi, 0, 0)),
                  pl.BlockSpec((None, nb, 2 * LANES), lambda bi, i: (bi, 0, 0)),
                  pl.BlockSpec(ovl.shape, lambda bi, i: (0, 0)),
                  pl.BlockSpec(bd.shape, lambda bi, i: (0, 0))],
        out_specs=pl.BlockSpec((Q_BLOCK, 4 * LANES), lambda bi, i: (bi * nq + i, 0)),
        out_shape=jax.ShapeDtypeStruct((b * t, 4 * LANES), BF16),
        scratch_shapes=[pltpu.VMEM((1, LANES), F32), pltpu.VMEM((NSA_HEADS * Q_BLOCK, LANES), F32)],
        compiler_params=_cparams(("parallel", "arbitrary")),
    )(q2d, small2d, kvb3, kcv3, ovl, bd)


def _attn_sample_kernel(pt_ref, cache_ref, q_ref, small_ref, rows_ref, win_ref, kcv_ref, swin_ref, ovl_ref,
                        o_ref, buf_ref, sem, *, n_pages, page, n_cmp, n_sel, key_chunk):
    past = n_pages * page
    slot = _prefetch_pages(pt_ref, cache_ref, buf_ref, sem, n_pages, page, 2 * LANES)
    lane1 = lax.broadcasted_iota(jnp.int32, (1, LANES), 1)
    lo1 = lane1 < HEAD_DIM
    q8f = _stack_heads(q_ref[...].astype(F32), lo1)
    q8 = q8f.astype(BF16)
    nh = NSA_HEADS

    nb = kcv_ref.shape[0]
    s = _dot_nt(q8, kcv_ref[:, 0:LANES])
    p = _masked_softmax(s, lax.broadcasted_iota(jnp.int32, (nh, nb), 1) < n_cmp).astype(BF16)
    o_c = _dot(p, kcv_ref[:, LANES:2 * LANES])
    imp8 = _dot(p, ovl_ref[...])
    nsp = ovl_ref.shape[1]
    blk = lax.broadcasted_iota(jnp.int32, (1, nsp), 1)
    selbs = []
    for g in range(NSA_KV_HEADS):
        imp = jnp.sum(imp8[g * NSA_GROUP:(g + 1) * NSA_GROUP], axis=0, keepdims=True)
        score = jnp.where(blk < n_sel, _sel_scores(imp, past, blk), -jnp.inf)
        selb = jnp.where(_topk_mask(score, SEL_TOPN) > 0.0, 0.0, NEG)
        selbs += [selb] * NSA_GROUP
    selb8 = jnp.concatenate(selbs, axis=0)
    selb8_b = selb8.astype(BF16)

    _wait_all(buf_ref.at[slot], sem.at[slot])
    ks_new = rows_ref[:, 2 * LANES:3 * LANES].astype(BF16).astype(F32)
    vs_new = rows_ref[:, 3 * LANES:4 * LANES].astype(BF16).astype(F32)
    s_new = jnp.sum(q8f * ks_new, axis=-1, keepdims=True) + selb8[:, n_sel - 1:n_sel]
    ck = key_chunk
    n_ck = past // ck
    ss = lax.broadcasted_iota(jnp.int32, (nsp, ck), 0)
    kk = lax.broadcasted_iota(jnp.int32, (nsp, ck), 1)
    scores = []
    for c in range(n_ck):
        kmat = buf_ref[slot, c * ck:(c + 1) * ck, 0:LANES].astype(BF16)
        e = jnp.where(((c * ck + kk) >> SEL_SHIFT) == ss, 1.0, 0.0).astype(BF16)
        scores.append(_dot_nt(q8, kmat) + _dot(selb8_b, e))
    m = s_new
    for sc in scores:
        m = jnp.maximum(m, jnp.max(sc, axis=-1, keepdims=True))
    p_new = jnp.exp(s_new - m)
    l = p_new
    acc = p_new * vs_new
    for c, sc in enumerate(scores):
        pe = jnp.exp(sc - m)
        l = l + jnp.sum(pe, axis=-1, keepdims=True)
        acc = acc + _dot(pe.astype(BF16), buf_ref[slot, c * ck:(c + 1) * ck, LANES:2 * LANES].astype(BF16))
    o_s = acc * (1.0 / l)

    wr = swin_ref.shape[0]
    s = _dot_nt(q8, swin_ref[:, 0:LANES].astype(BF16))
    j = lax.broadcasted_iota(jnp.int32, (nh, wr), 1)
    valid = wr - j < WINDOW
    kw_new = win_ref[:, 0:LANES].astype(BF16).astype(F32)
    vw_new = win_ref[:, LANES:2 * LANES].astype(BF16).astype(F32)
    s_new = jnp.sum(q8f * kw_new, axis=-1, keepdims=True)
    s = jnp.where(valid, s, NEG)
    m = jnp.maximum(s_new, jnp.max(s, axis=-1, keepdims=True))
    pe = jnp.where(valid, jnp.exp(s - m), 0.0)
    p_new = jnp.exp(s_new - m)
    l = p_new + jnp.sum(pe, axis=-1, keepdims=True)
    o_w = (p_new * vw_new + _dot(pe.astype(BF16), swin_ref[:, LANES:2 * LANES].astype(BF16))) * (1.0 / l)

    sig = _sigmoid(small_ref[...])
    for r in range(NSA_GROUP):
        outs = []
        for g in range(NSA_KV_HEADS):
            h = g * NSA_GROUP + r
            gl = SM_GATE + 3 * h
            outs.append(sig[:, gl:gl + 1] * o_c[h:h + 1] + sig[:, gl + 1:gl + 2] * o_s[h:h + 1]
                        + sig[:, gl + 2:gl + 3] * o_w[h:h + 1])
        o_ref[:, LANES * r:LANES * (r + 1)] = jnp.where(lo1, outs[0], outs[1])


def _attn_sample(cache3, page_table, q2d, small2d, rows2d, win2d, kcv3, swin3, ovl, n_cmp, n_sel):
    bd_, n_pages = page_table.shape
    page = cache3.shape[1]
    past = n_pages * page
    key_chunk = math.gcd(past, 2048)
    per_seq = lambda a: pl.BlockSpec((None,) + a.shape[1:], lambda i, pt: (i,) + (0,) * (a.ndim - 1))
    q3, small3, rows3, win3 = (a[:, None, :] for a in (q2d, small2d, rows2d, win2d))
    grid_spec = pltpu.PrefetchScalarGridSpec(
        num_scalar_prefetch=1,
        grid=(bd_,),
        in_specs=[pl.BlockSpec(memory_space=pl.ANY), per_seq(q3), per_seq(small3), per_seq(rows3), per_seq(win3),
                  per_seq(kcv3), per_seq(swin3), pl.BlockSpec(ovl.shape, lambda i, pt: (0, 0))],
        out_specs=pl.BlockSpec((None, 1, 4 * LANES), lambda i, pt: (i, 0, 0)),
        scratch_shapes=[pltpu.VMEM((2, past, 2 * LANES), F32), pltpu.SemaphoreType.DMA((2,))],
    )
    out = pl.pallas_call(
        functools.partial(_attn_sample_kernel, n_pages=n_pages, page=page, n_cmp=n_cmp, n_sel=n_sel,
                          key_chunk=key_chunk),
        grid_spec=grid_spec,
        out_shape=jax.ShapeDtypeStruct((bd_, 1, 4 * LANES), F32),
        compiler_params=_cparams(("arbitrary",)),
    )(page_table, cache3, q3, small3, rows3, win3, kcv3, swin3, ovl)
    return out[:, 0, :]


def _dn_kernel(x_ref, small_ref, z_ref, cw_ref, c0_ref, s0_ref, alog_ref, dtb_ref, gn_ref,
               o_ref, sout_ref, s_scr, xs_scr, *, chunk, tb, t_valid, precise):
    j = pl.program_id(1)
    dnw = z_ref.shape[1]
    hd = DN_HEAD_DIM
    hist = SUBLANES

    def opnd(x):
        return x if precise else x.astype(BF16)

    def mm(a, b, form=_dot):
        return _mm(a, b, precise, form)

    @pl.when(j == 0)
    def _():
        s_scr[...] = s0_ref[...]
        xs_scr[0:hist, :] = c0_ref[...]

    xs_scr[hist:hist + tb, :] = x_ref[...]
    conv = xs_scr[pl.ds(hist - (DN_CONV - 1), tb), :] * cw_ref[0:1, :]
    for jj in range(1, DN_CONV):
        conv = conv + xs_scr[pl.ds(hist - (DN_CONV - 1) + jj, tb), :] * cw_ref[jj:jj + 1, :]
    xs_scr[0:hist, :] = xs_scr[tb:tb + hist, :]
    act = conv * _sigmoid(conv)

    small = small_ref[...]
    tpos = j * tb + lax.broadcasted_iota(jnp.int32, (tb, LANES), 0)
    live = tpos < t_valid
    xg = small + dtb_ref[...]
    softplus = jnp.maximum(xg, 0.0) + jnp.log(1.0 + jnp.exp(-jnp.abs(xg)))
    g_all = jnp.where(live, -jnp.exp(alog_ref[...]) * softplus, 0.0)
    beta_all = jnp.where(live, _sigmoid(small), 0.0)

    ri = lax.broadcasted_iota(jnp.int32, (chunk, chunk), 0)
    ci = lax.broadcasted_iota(jnp.int32, (chunk, chunk), 1)
    incl = ri >= ci
    strict = ri > ci
    ltri = jnp.where(incl, 1.0, 0.0).astype(BF16)
    eye = jnp.where(ri == ci, 1.0, 0.0)
    lane_c = lax.broadcasted_iota(jnp.int32, (chunk, LANES), 1)
    n_sq = max(1, int(math.ceil(math.log2(chunk))))

    n_chunks = tb // chunk
    items = [(c, h) for c in range(n_chunks) for h in range(DN_HEADS)]

    gc_alls, gc3s = [], []
    for c in range(n_chunks):
        g3 = _split3(g_all[c * chunk:(c + 1) * chunk])
        gc_all = _dot(ltri, g3[0]) + (_dot(ltri, g3[1]) + _dot(ltri, g3[2]))
        gc_alls.append(gc_all)
        gc3s.append(_split3(gc_all))

    prep = []
    for c, h in items:
        rs = slice(c * chunk, (c + 1) * chunk)
        q = act[rs, h * hd:(h + 1) * hd]
        k = act[rs, dnw + h * hd:dnw + (h + 1) * hd]
        v = act[rs, 2 * dnw + h * hd:2 * dnw + (h + 1) * hd]
        q = q * lax.rsqrt(jnp.sum(q * q, axis=-1, keepdims=True) + EPS) * (hd ** -0.5)
        k = k * lax.rsqrt(jnp.sum(k * k, axis=-1, keepdims=True) + EPS)
        beta = beta_all[rs, SM_B + h:SM_B + h + 1]
        gc = gc_alls[c][:, SM_A + h:SM_A + h + 1]
        pick = jnp.where(lane_c == SM_A + h, 1.0, 0.0).astype(BF16)
        g3 = gc3s[c]
        gc_row = _dot_nt(pick, g3[0]) + (_dot_nt(pick, g3[1]) + _dot_nt(pick, g3[2]))
        dmask = jnp.where(incl, jnp.exp(jnp.where(incl, gc - gc_row, 0.0)), 0.0)
        kb = k * beta
        kop = opnd(k)
        a_strict = jnp.where(strict, mm(opnd(kb), kop, _dot_nt) * dmask, 0.0)
        egc = jnp.exp(gc)
        g_last = gc[chunk - 1:chunk, :]
        prep.append(dict(
            vb=opnd(v * beta), kbg=opnd(kb * egc), qg=opnd(q * egc),
            qk=opnd(mm(opnd(q), kop, _dot_nt) * dmask),
            kd=opnd(k * jnp.exp(g_last - gc)), decay=jnp.exp(g_last), n=-a_strict))

    tinv = [eye + p["n"] for p in prep]
    nsp = [opnd(p["n"]) for p in prep]
    for _ in range(n_sq - 1):
        nsp = [opnd(mm(n, n)) for n in nsp]
        tinv = [t + mm(opnd(t), n) for t, n in zip(tinv, nsp)]
    us, ws = [], []
    for p, t in zip(prep, tinv):
        top = opnd(t)
        us.append(mm(top, p["vb"]))
        ws.append(opnd(mm(top, p["kbg"])))

    state = [s_scr[h] for h in range(DN_HEADS)]
    for i, (c, h) in enumerate(items):
        p = prep[i]
        rs = slice(c * chunk, (c + 1) * chunk)
        s_b = opnd(state[h])
        v_new = opnd(us[i] - mm(ws[i], s_b))
        o = mm(p["qg"], s_b) + mm(p["qk"], v_new)
        state[h] = state[h] * p["decay"] + mm(p["kd"], v_new, _dot_tn)
        on = o * lax.rsqrt(jnp.mean(o * o, axis=-1, keepdims=True) + EPS) * gn_ref[...]
        z = z_ref[rs, h * hd:(h + 1) * hd]
        o_ref[rs, h * hd:(h + 1) * hd] = (on * (z * _sigmoid(z))).astype(o_ref.dtype)
    for h in range(DN_HEADS):
        s_scr[h] = state[h]

    @pl.when(j == pl.num_programs(1) - 1)
    def _():
        sout_ref[...] = s_scr[...]


def _dn_stage(x3, small3, z3, conv0, s0, dw, chunk, tb, t_valid, precise=False):
    b, tpad, w3 = x3.shape
    dnw = w3 // 3
    nblk = tpad // tb
    full = lambda a: pl.BlockSpec(a.shape, lambda bi, j: (0,) * a.ndim)
    tok = lambda w: pl.BlockSpec((None, tb, w), lambda bi, j: (bi, j, 0))
    o, s_out = pl.pallas_call(
        functools.partial(_dn_kernel, chunk=chunk, tb=tb, t_valid=t_valid, precise=precise),
        grid=(b, nblk),
        in_specs=[tok(w3), tok(LANES), tok(dnw), full(dw["cw"]),
                  pl.BlockSpec((None, SUBLANES, w3), lambda bi, j: (bi, 0, 0)),
                  pl.BlockSpec((None, DN_HEADS, DN_HEAD_DIM, DN_HEAD_DIM), lambda bi, j: (bi, 0, 0, 0)),
                  full(dw["alog"]), full(dw["dtb"]), full(dw["gn"])],
        out_specs=(tok(dnw),
                   pl.BlockSpec((None, DN_HEADS, DN_HEAD_DIM, DN_HEAD_DIM), lambda bi, j: (bi, 0, 0, 0))),
        out_shape=(jax.ShapeDtypeStruct((b, tpad, dnw), F32 if precise else BF16),
                   jax.ShapeDtypeStruct((b, DN_HEADS, DN_HEAD_DIM, DN_HEAD_DIM), F32)),
        scratch_shapes=[pltpu.VMEM((DN_HEADS, DN_HEAD_DIM, DN_HEAD_DIM), F32),
                        pltpu.VMEM((tb + SUBLANES, w3), F32)],
        compiler_params=_cparams(("parallel", "arbitrary")),
    )(x3, small3, z3, dw["cw"], conv0, s0, dw["alog"], dw["dtb"], dw["gn"])
    return o, s_out


def _merge_kernel(x_ref, oa_ref, ob_ref, mg_ref, wpa_ref, wpb_ref, wout_ref, gf_ref, wr_ref, br_ref,
                  x1_ref, h_ref, route_ref, *, precise):
    d = x_ref.shape[1]
    mixed = (mg_ref[:, 0:d] * _mm(oa_ref[...], wpa_ref[...], precise)
             + mg_ref[:, d:2 * d] * _mm(ob_ref[...], wpb_ref[...], precise))
    x1 = x_ref[...] + _mm(mixed, wout_ref[...], precise)
    x1_ref[...] = x1
    h = (x1 * lax.rsqrt(jnp.mean(x1 * x1, axis=-1, keepdims=True) + EPS)) * gf_ref[...]
    h_ref[...] = h
    logits = _mm(h, wr_ref[...], precise) + br_ref[...]
    lane = lax.broadcasted_iota(jnp.int32, logits.shape, 1)
    lanef = lane.astype(F32)
    big = float(LANES)
    is_g = lane < N_GROUPS
    lg = jnp.where(is_g, logits, NEG)
    eg = jnp.where(is_g, jnp.exp(lg - jnp.max(lg, axis=-1, keepdims=True)), 0.0)
    pg = eg / jnp.sum(eg, axis=-1, keepdims=True)
    p_top = jnp.max(pg, axis=-1, keepdims=True)
    grp = jnp.min(jnp.where(jnp.logical_and(is_g, pg == p_top), lanef, big), axis=-1, keepdims=True)
    e_lo = EXPERT_LANE0 + grp * EXPERTS_PER_GROUP
    in_grp = jnp.logical_and(lanef >= e_lo, lanef < e_lo + EXPERTS_PER_GROUP)
    le = jnp.where(in_grp, logits, -jnp.inf)
    v0 = jnp.max(le, axis=-1, keepdims=True)
    i0 = jnp.min(jnp.where(le == v0, lanef, big), axis=-1, keepdims=True)
    le1 = jnp.where(lanef == i0, -jnp.inf, le)
    v1 = jnp.max(le1, axis=-1, keepdims=True)
    i1 = jnp.min(jnp.where(le1 == v1, lanef, big), axis=-1, keepdims=True)
    e1 = jnp.exp(v1 - v0)
    den = 1.0 + e1
    w0 = p_top * (1.0 / den)
    w1 = p_top * (e1 / den)
    route = jnp.where(lane == 0, w0, jnp.where(lane == 1, w1, jnp.where(
        lane == 2, i0 - EXPERT_LANE0, jnp.where(lane == 3, i1 - EXPERT_LANE0, 0.0))))
    route_ref[...] = route


def _merge_stage(x2d, oa, ob, mg, mw, tm, precise=False):
    m, d = x2d.shape
    row = lambda w: pl.BlockSpec((tm, w), lambda i: (i, 0))
    full = lambda a: pl.BlockSpec(a.shape, lambda i: (0,) * a.ndim)
    consts = (mw["wpa"], mw["wpb"], mw["wout"], mw["gf"], mw["wr"], mw["br"])
    return pl.pallas_call(
        functools.partial(_merge_kernel, precise=precise),
        grid=(m // tm,),
        in_specs=[row(d), row(oa.shape[1]), row(ob.shape[1]), row(2 * d)] + [full(a) for a in consts],
        out_specs=(row(d), row(d), row(LANES)),
        out_shape=(jax.ShapeDtypeStruct((m, d), F32), jax.ShapeDtypeStruct((m, d), F32),
                   jax.ShapeDtypeStruct((m, LANES), F32)),
        compiler_params=_cparams(("parallel",)),
    )(x2d, oa, ob, mg, *consts)


def _row_copy(src_ref, dst_ref, sem, src_row, dst_row):
    return pltpu.make_async_copy(src_ref.at[pl.ds(src_row, 1), :], dst_ref.at[pl.ds(dst_row, 1), :], sem)


def _expert_kernel(be_ref, lo_ref, tok_ref, nblk_ref, h_ref, wg_ref, wu_ref, wd_ref, y_ref, buf_ref, sem, *, blk):
    i = pl.program_id(0)
    n_used = nblk_ref[0]
    slot = i % 2

    @pl.when(i == 0)
    def _():
        def start(r, c):
            _row_copy(h_ref, buf_ref.at[0], sem.at[0], tok_ref[lo_ref[0] + r], r).start()
            return c
        lax.fori_loop(0, blk, start, 0, unroll=8)

    def step(prefetch):
        _wait_all(buf_ref.at[slot], sem.at[slot])
        if prefetch:
            lo = lo_ref[i + 1]
            for r in range(blk):
                _row_copy(h_ref, buf_ref.at[1 - slot], sem.at[1 - slot], tok_ref[lo + r], r).start(priority=r % 2)
        xb = buf_ref[slot].astype(BF16)
        gate = _dot(xb, wg_ref[...])
        up = _dot(xb, wu_ref[...])
        mid = (gate * _sigmoid(gate)) * up
        y_ref[...] = _dot(mid.astype(BF16), wd_ref[...])

    @pl.when(i + 1 < n_used)
    def _():
        step(True)

    @pl.when(i + 1 == n_used)
    def _():
        step(False)

    @pl.when(i >= n_used)
    def _():
        y_ref[...] = jnp.zeros(y_ref.shape, F32)


def _expert_stage(h2d, blk_expert, blk_lo, s_tok, n_used, ew, blk):
    m, d = h2d.shape
    n_blocks = blk_expert.shape[0]
    de = ew["wg"].shape[2]
    wspec = lambda s: pl.BlockSpec((None,) + s, lambda i, be, lo, tok, nb: (be[i], 0, 0))
    grid_spec = pltpu.PrefetchScalarGridSpec(
        num_scalar_prefetch=4,
        grid=(n_blocks,),
        in_specs=[pl.BlockSpec(memory_space=pl.ANY), wspec((d, de)), wspec((d, de)), wspec((de, d))],
        out_specs=pl.BlockSpec((blk, d), lambda i, be, lo, tok, nb: (i, 0)),
        scratch_shapes=[pltpu.VMEM((2, blk, d), F32), pltpu.SemaphoreType.DMA((2,))],
    )
    return pl.pallas_call(
        functools.partial(_expert_kernel, blk=blk),
        grid_spec=grid_spec,
        out_shape=jax.ShapeDtypeStruct((n_blocks * blk, d), F32),
        compiler_params=_cparams(("arbitrary",)),
    )(blk_expert, blk_lo, s_tok, n_used, h2d, ew["wg"], ew["wu"], ew["wd"])


def _combine_kernel(pos_ref, x1_ref, route_ref, y_ref, out_ref, buf_ref, sem, *, tm):
    i = pl.program_id(0)
    slot = i % 2

    @pl.when(i == 0)
    def _():
        def start(r, c):
            for k in range(TOP_K):
                _row_copy(y_ref, buf_ref.at[0, k], sem.at[0], pos_ref[r * TOP_K + k], r).start()
            return c
        lax.fori_loop(0, tm, start, 0, unroll=4)

    _wait_all(buf_ref.at[slot], sem.at[slot])

    @pl.when(i + 1 < pl.num_programs(0))
    def _():
        for r in range(tm):
            for k in range(TOP_K):
                _row_copy(y_ref, buf_ref.at[1 - slot, k], sem.at[1 - slot],
                          pos_ref[((i + 1) * tm + r) * TOP_K + k], r).start(priority=k % 2)

    w = route_ref[...]
    out_ref[...] = x1_ref[...] + (buf_ref[slot, 0] * w[:, 0:1] + buf_ref[slot, 1] * w[:, 1:2])


def _combine_stage(x1, route, y_sorted, pos, tm):
    m, d = x1.shape
    grid_spec = pltpu.PrefetchScalarGridSpec(
        num_scalar_prefetch=1,
        grid=(m // tm,),
        in_specs=[pl.BlockSpec((tm, d), lambda i, p: (i, 0)), pl.BlockSpec((tm, LANES), lambda i, p: (i, 0)),
                  pl.BlockSpec(memory_space=pl.ANY)],
        out_specs=pl.BlockSpec((tm, d), lambda i, p: (i, 0)),
        scratch_shapes=[pltpu.VMEM((2, TOP_K, tm, d), F32), pltpu.SemaphoreType.DMA((2,))],
    )
    return pl.pallas_call(
        functools.partial(_combine_kernel, tm=tm),
        grid_spec=grid_spec,
        out_shape=jax.ShapeDtypeStruct((m, d), F32),
        compiler_params=_cparams(("arbitrary",)),
    )(pos, x1, route, y_sorted)


def _moe_stage(x1, h2d, route, ew, blk, tm):
    m, d = h2d.shape
    n_exp = ew["wg"].shape[0]
    n_assign = m * TOP_K
    i32 = jnp.int32
    flat_e = route[:, TOP_K:2 * TOP_K].astype(i32).reshape(n_assign)
    order = jnp.argsort(flat_e).astype(i32)
    rank = jnp.argsort(order).astype(i32)
    counts = jnp.sum(flat_e[None, :] == jnp.arange(n_exp, dtype=i32)[:, None], axis=1, dtype=i32)
    padded = (counts + blk - 1) // blk * blk
    pad_end = jnp.cumsum(padded)
    pad_start = pad_end - padded
    start = jnp.cumsum(counts) - counts
    n_blocks = (n_assign + n_exp * (blk - 1) + blk - 1) // blk
    blk_first = jnp.arange(n_blocks, dtype=i32) * blk
    blk_expert = jnp.minimum(jnp.sum(pad_end[None, :] <= blk_first[:, None], axis=1, dtype=i32), n_exp - 1)
    delta = start - pad_start
    blk_lo = jnp.clip(blk_first + delta[blk_expert], 0, n_assign).astype(i32)
    s_tok = jnp.concatenate([order // TOP_K, jnp.zeros((blk,), i32)])
    onehot = flat_e[:, None] == jnp.arange(n_exp, dtype=i32)[None, :]
    pos = (rank - jnp.sum(jnp.where(onehot, delta[None, :], 0), axis=1, dtype=i32)).astype(i32)
    n_used = (pad_end[n_exp - 1:n_exp] // blk).astype(i32)
    y_sorted = _expert_stage(h2d, blk_expert, blk_lo, s_tok, n_used, ew, blk)
    return _combine_stage(x1, route, y_sorted, pos, tm)


def _rope_tables(pos):
    inv_freq = ROPE_THETA ** (-jnp.arange(ROPE_HALF, dtype=F32) / ROPE_HALF)
    ang = pos.astype(F32)[:, None] * inv_freq[None, :]
    cos, sin = jnp.cos(ang), jnp.sin(ang)
    n = pos.shape[0]
    zeros = lambda k: jnp.zeros((n, k), F32)
    c = jnp.concatenate([cos, cos, jnp.ones((n, HEAD_DIM - ROPE_DIM), F32)], axis=1)
    s1 = jnp.concatenate([-sin, zeros(HEAD_DIM - ROPE_HALF)], axis=1)
    s2 = jnp.concatenate([zeros(ROPE_HALF), sin, zeros(HEAD_DIM - ROPE_DIM)], axis=1)
    return tuple(jnp.tile(a, (1, LANES // HEAD_DIM)) for a in (c, s1, s2))


def _block_diag_ones():
    i = np.arange(LANES)
    return jnp.asarray((i[:, None] // HEAD_DIM == i[None, :] // HEAD_DIM).astype(np.float32), BF16)


def _pack_proj(w_in, norm_mix, q_norm, k_norm, dnw):
    d = w_in.shape[0]
    qc = NSA_HEADS * HEAD_DIM
    kvc = NSA_KV_HEADS * HEAD_DIM
    sizes = (qc,) + (kvc,) * 6 + (NSA_HEADS * 3, 3 * dnw, DN_HEADS, DN_HEADS, dnw, d, d)
    o = np.concatenate([[0], np.cumsum(sizes)])
    seg = lambda i: w_in[:, int(o[i]):int(o[i + 1])]
    wq = seg(0).reshape(d, NSA_HEADS, HEAD_DIM)[:, np.asarray(HEAD_PERM)].reshape(d, qc)
    small = jnp.concatenate([seg(9), seg(10), seg(7), jnp.zeros((d, LANES - 2 * DN_HEADS - 3 * NSA_HEADS), F32)], axis=1)
    w = jnp.concatenate([wq, seg(1), seg(2), seg(3), seg(4), seg(5), seg(6), small, seg(8), seg(11), seg(12), seg(13)],
                        axis=1)
    offs, ncols = _proj_layout(dnw, d)
    assert ncols == w.shape[1]
    rep = LANES // HEAD_DIM
    return dict(w=w.astype(BF16), w32=w, offs=offs, ncols=ncols, gin=norm_mix[None, :],
                qg=jnp.tile(q_norm, rep)[None, :], kg=jnp.tile(k_norm[0:2], (1, rep)), bd=_block_diag_ones())


def _pack_compress(w_cmp, cmp_pos, k_norm):
    def bdiag(w):
        z = jnp.zeros_like(w)
        return jnp.concatenate([jnp.concatenate([w, z], axis=2), jnp.concatenate([z, w], axis=2)], axis=1)

    s = CMP_STRIDE
    halves = [(0, slice(0, s)), (0, slice(s, 2 * s)), (1, slice(0, s)), (1, slice(s, 2 * s))]
    return dict(w=jnp.stack([bdiag(w_cmp[i, sl]) for i, sl in halves]).astype(BF16),
                p=jnp.stack([jnp.tile(cmp_pos[i, sl], (1, LANES // HEAD_DIM)) for i, sl in halves]),
                kg=jnp.tile(k_norm[2], LANES // HEAD_DIM)[None, :], bd=_block_diag_ones())


def _overlap(nb, ns_pad, n_cmp, n_sel):
    cs = np.arange(nb)[:, None] * CMP_STRIDE
    ss = np.arange(ns_pad)[None, :] * SEL_BLOCK
    ov = (cs <= ss + SEL_BLOCK - 1) & (cs + CMP_BLOCK - 1 >= ss)
    ov &= (np.arange(nb)[:, None] < n_cmp) & (np.arange(ns_pad)[None, :] < n_sel)
    return jnp.asarray(ov.astype(np.float32), BF16)


def _pad_lanes(v, n=LANES):
    return jnp.zeros((1, n), F32).at[0, :v.shape[0]].set(v)


def _round_up(x, n):
    return (x + n - 1) // n * n


def kernel(x_prompt, x_sample, cache_nsa, page_table, state_win, state_dn_conv, state_dn_S, norm_mix, w_in, q_norm, k_norm, cmp_pos, w_cmp, dn_conv_w, dn_A_log, dn_dt_bias, dn_norm, w_proj_a, w_proj_b, w_out, norm_ffn, w_router_g, b_router_g, w_router_e, b_router_e, w_gate, w_up, w_down):
    b, t, d = x_prompt.shape
    bd_, tn, _ = x_sample.shape
    depth = w_in.shape[0]
    assert depth == 1 and tn == 1, "one layer, one new token per sample sequence"
    n_pool, page = cache_nsa.shape[1], cache_nsa.shape[2]
    n_pages = page_table.shape[1]
    past = n_pages * page
    win_rows = state_win.shape[2]
    dnw = dn_conv_w.shape[2] // 3
    assert t % Q_BLOCK == 0 and t >= WINDOW + Q_BLOCK and t % SEL_CHUNK == 0 and past % CMP_STRIDE == 0
    l = 0

    pw = _pack_proj(w_in[l], norm_mix[l], q_norm[l], k_norm[l], dnw)
    cw = _pack_compress(w_cmp[l], cmp_pos[l], k_norm[l])
    dw = dict(cw=dn_conv_w[l], alog=_pad_lanes(dn_A_log[l]), dtb=_pad_lanes(dn_dt_bias[l]), gn=dn_norm[l][None, :])
    perm = np.asarray(HEAD_PERM)
    wr = jnp.zeros((d, LANES), F32).at[:, 0:N_GROUPS].set(w_router_g[l])
    wr = wr.at[:, EXPERT_LANE0:EXPERT_LANE0 + w_router_e.shape[2]].set(w_router_e[l])
    br = jnp.zeros((1, LANES), F32).at[0, 0:N_GROUPS].set(b_router_g[l])
    br = br.at[0, EXPERT_LANE0:EXPERT_LANE0 + b_router_e.shape[1]].set(b_router_e[l])
    mw32 = dict(wpa=w_proj_a[l].reshape(NSA_HEADS, HEAD_DIM, d)[perm].reshape(NSA_HEADS * HEAD_DIM, d),
                wpb=w_proj_b[l], wout=w_out[l], gf=norm_ffn[l][None, :], wr=wr, br=br)
    mw = {k: (v.astype(BF16) if k in ("wpa", "wpb", "wout", "wr") else v) for k, v in mw32.items()}
    ew = dict(wg=w_gate[l].astype(BF16), wu=w_up[l].astype(BF16), wd=w_down[l].astype(BF16))

    tm = 256
    tabs_p = _rope_tables(jnp.arange(t))
    q_p, rows_p, win_p, kvb_p, small_p, dnqkv_p, dnz_p, mg_p = _proj_stage(
        x_prompt.reshape(b * t, d), pw, tabs_p, PROJ_TM, t // PROJ_TM)
    nb_p = t // CMP_STRIDE
    n_cmp_p = (t - CMP_BLOCK) // CMP_STRIDE + 1
    n_sel_p = -(-t // SEL_BLOCK)
    tabs_cp = _rope_tables(jnp.arange(nb_p) * CMP_STRIDE + CMP_BLOCK - 1)
    kcv_p = _compress_prompt(rows_p.reshape(b, t, 4 * LANES), cw, tabs_cp)
    ovl_p = _overlap(nb_p, _round_up(n_sel_p, LANES), n_cmp_p, n_sel_p)
    oa_p = _attn_prompt(q_p, small_p, kvb_p.reshape(b, t, 4 * LANES), kcv_p, ovl_p, n_cmp_p)
    tb = DN_TB
    ob_p, s_p = _dn_stage(dnqkv_p.reshape(b, t, 3 * dnw), small_p.reshape(b, t, LANES), dnz_p.reshape(b, t, dnw),
                          jnp.zeros((b, SUBLANES, 3 * dnw), F32),
                          jnp.zeros((b, DN_HEADS, DN_HEAD_DIM, DN_HEAD_DIM), F32), dw, DN_CHUNK, tb, t)
    x1_p, h_p, route_p = _merge_stage(x_prompt.reshape(b * t, d), oa_p, ob_p.reshape(b * t, dnw), mg_p, mw, MERGE_TM)
    y_p = _moe_stage(x1_p, h_p, route_p, ew, 256, tm).reshape(b, t, d)
    kv_p = rows_p.reshape(1, b, t, 4, NSA_KV_HEADS, HEAD_DIM)
    win_all = win_p.reshape(b, t, 2, NSA_KV_HEADS, HEAD_DIM)
    win_out_p = jnp.pad(win_all, ((0, 0), (max(0, win_rows - t), 0), (0, 0), (0, 0), (0, 0)))[:, -win_rows:][None]
    conv_p = dnqkv_p.reshape(b, t, 3 * dnw)[:, t - (DN_CONV - 1):][None]

    tabs_s = _rope_tables(jnp.full((bd_,), past, jnp.int32))
    q_s, rows_s, win_s, _, small_s, dnqkv_s, dnz_s, mg_s = _proj_stage(
        x_sample.reshape(bd_, d), dict(pw, w=pw["w32"]), tabs_s, bd_, 1, precise=True)
    cache3 = cache_nsa[l].reshape(n_pool, page, 4 * LANES)
    nb_s = past // CMP_STRIDE
    n_cmp_s = (past + tn - CMP_BLOCK) // CMP_STRIDE + 1
    n_sel_s = -(-(past + tn) // SEL_BLOCK)
    assert n_cmp_s == nb_s - 1 and n_cmp_p == nb_p - 1 and (1 << SEL_SHIFT) == SEL_BLOCK
    tabs_cs = _rope_tables(jnp.arange(nb_s) * CMP_STRIDE + CMP_BLOCK - 1)
    kcv_s = _compress_sample(cache3, page_table, cw, tabs_cs)
    ovl_s = _overlap(nb_s, _round_up(n_sel_s, LANES), n_cmp_s, n_sel_s)
    swin3 = state_win[l].reshape(bd_, win_rows, 2 * LANES)
    oa_s = _attn_sample(cache3, page_table, q_s, small_s, rows_s, win_s, kcv_s, swin3, ovl_s, n_cmp_s, n_sel_s)
    pad_t = lambda a: jnp.pad(a[:, None, :], ((0, 0), (0, SUBLANES - tn), (0, 0)))
    conv0_s = jnp.pad(state_dn_conv[l], ((0, 0), (SUBLANES - (DN_CONV - 1), 0), (0, 0)))
    ob_s, s_s = _dn_stage(pad_t(dnqkv_s), pad_t(small_s), pad_t(dnz_s), conv0_s, state_dn_S[l], dw,
                          SUBLANES, SUBLANES, tn, precise=True)
    x1_s, h_s, route_s = _merge_stage(x_sample.reshape(bd_, d), oa_s, ob_s[:, 0, :], mg_s, mw32, bd_, precise=True)
    y_s = _moe_stage(x1_s, h_s, route_s, ew, 64, bd_).reshape(bd_, tn, d)
    kv_s = rows_s.reshape(1, bd_, tn, 4, NSA_KV_HEADS, HEAD_DIM)
    win_new = win_s.reshape(bd_, tn, 2, NSA_KV_HEADS, HEAD_DIM)
    win_out_s = jnp.concatenate([state_win[l], win_new], axis=1)[:, -win_rows:][None]
    conv_s = jnp.concatenate([state_dn_conv[l], dnqkv_s[:, None, :]], axis=1)[:, -(DN_CONV - 1):][None]

    return (y_p, y_s, kv_p, kv_s, win_out_p, win_out_s, conv_p, conv_s, s_p[None], s_s[None])
```
